```python
import math
import jax, jax.numpy as jnp
from jax import lax
import numpy as np

D_MODEL = 2048
BATCH = 2
SEQ = 4096
DEPTH = 2
DEC_BATCH = 128
DEC_SEQ = 8
PAST_LEN = 8192
PAGE_SIZE = 128

HG_HEADS = 8
HG_KEY = 128
HG_VAL = 128
HG_WIDTH = HG_HEADS * HG_VAL
HG_CHUNK = 32
MLA_HEADS = 8
MLA_NOPE = 128
MLA_ROPE = 64
MLA_V = 128
MLA_Q_RANK = 768
MLA_KV_RANK = 512
MLA_SCALE = (MLA_NOPE + MLA_ROPE) ** -0.5
ROPE_THETA = 10000.0
Q_BLOCK = 128
CM_GROUPS = 8
CM_GROUP_DIM = 128
CM_CHUNK = 128
CM_WIDTH = CM_GROUPS * CM_GROUP_DIM
N_BRANCH = 3
BRANCH_WIDTH = 1024
D_FF = 5632
CONV_W = 3
PLE_DIM = 256
EPS = 1e-6
IN_SIZES = (HG_HEADS * HG_KEY, HG_HEADS * HG_KEY, HG_WIDTH, HG_WIDTH,
            MLA_Q_RANK, MLA_KV_RANK, MLA_ROPE, CM_WIDTH, CM_WIDTH)
IN_COLS = sum(IN_SIZES)

kernel_name = 'hybrid_hgrn2_mla_gmlp_decode_step'


def rmsnorm(x, g):
    xf = x.astype(jnp.float32)
    y = xf * lax.rsqrt(jnp.mean(xf * xf, axis=-1, keepdims=True) + EPS)
    return (y * g.astype(jnp.float32)).astype(x.dtype)


def rope_tables(pos):
    inv = 1.0 / (ROPE_THETA ** (jnp.arange(0, MLA_ROPE, 2, dtype=jnp.float32) / MLA_ROPE))
    ang = pos.astype(jnp.float32)[:, None] * inv[None, :]
    return jnp.cos(ang), jnp.sin(ang)


def apply_rope(x, cos, sin):
    x1, x2 = jnp.split(x.astype(jnp.float32), 2, axis=-1)
    return jnp.concatenate([x1 * cos - x2 * sin, x2 * cos + x1 * sin], axis=-1).astype(x.dtype)


def hgrn2_chunked(q, logf, k, v, s0):
    B, T, H, K = q.shape
    C = min(HG_CHUNK, T)
    n = -(-T // C)
    pad = n * C - T

    def prep(a):
        a = jnp.pad(a.astype(jnp.float32), ((0, 0), (0, pad), (0, 0), (0, 0)))
        return a.reshape(B, n, C, H, a.shape[-1]).transpose(1, 0, 3, 2, 4)

    qc, gc, kc, vc = prep(q), prep(logf), prep(k), prep(v)
    tri = jnp.tril(jnp.ones((C, C), dtype=bool))

    def step(S, inp):
        qi, gi, ki, vi = inp
        b = jnp.cumsum(gi, axis=2)
        diff = b[:, :, :, None, :] - b[:, :, None, :, :]
        decay = jnp.exp(jnp.where(tri[None, None, :, :, None], diff, -jnp.inf))
        A = jnp.einsum('bhtk,bhsk,bhtsk->bhts', qi, ki, decay)
        o = (jnp.einsum('bhts,bhsv->bhtv', A, vi)
             + jnp.einsum('bhtk,bhkv->bhtv', qi * jnp.exp(b), S))
        b_last = b[:, :, -1:, :]
        S = (jnp.exp(b_last[:, :, 0, :])[..., None] * S
             + jnp.einsum('bhsk,bhsv->bhkv', ki * jnp.exp(b_last - b), vi))
        return S, o

    S, o = lax.scan(step, s0.astype(jnp.float32), (qc, gc, kc, vc))
    o = o.transpose(1, 0, 3, 2, 4).reshape(B, n * C, H, -1)[:, :T]
    return o, S


def hgrn2_branch(hq, hf, hi, hg, lb, norm_g, s0):
    B, T, _ = hq.shape
    shp = (B, T, HG_HEADS, HG_KEY)
    q = jax.nn.silu(hq).reshape(shp)
    f = hf.astype(jnp.float32)
    lbf = lb.astype(jnp.float32)
    logf = jnp.logaddexp(jnp.log(lbf), jnp.log1p(-lbf) + jax.nn.log_sigmoid(f))
    k = (1.0 - lbf) * jax.nn.sigmoid(-f)
    v = hi.reshape(B, T, HG_HEADS, HG_VAL)
    o, s_new = hgrn2_chunked(q, logf.reshape(shp), k.reshape(shp), v, s0)
    gate = jax.nn.silu(hg.astype(jnp.float32)).reshape(B, T, HG_HEADS, HG_VAL)
    o = rmsnorm(o, norm_g.reshape(HG_HEADS, HG_VAL)) * gate
    return o.reshape(B, T, HG_WIDTH).astype(hq.dtype), s_new


def mla_qkv(qa, kva, kr, q_norm, kv_norm, w_q_b, w_kv_b, pos):
    B, T, _ = qa.shape
    q = (rmsnorm(qa, q_norm) @ w_q_b).reshape(B, T, MLA_HEADS, MLA_NOPE + MLA_ROPE)
    q_nope, q_rope = q[..., :MLA_NOPE], q[..., MLA_NOPE:]
    cos, sin = rope_tables(pos)
    q_rope = apply_rope(q_rope, cos[:, None, :], sin[:, None, :])
    k_rope = apply_rope(kr, cos, sin)
    c = rmsnorm(kva, kv_norm)
    w_kvb = w_kv_b.reshape(MLA_KV_RANK, MLA_HEADS, MLA_NOPE + MLA_V)
    q_lat = jnp.einsum('bthd,chd->bthc', q_nope, w_kvb[..., :MLA_NOPE])
    return q_lat, q_rope, c, k_rope, w_kvb[..., MLA_NOPE:]


def latent_scores(q_lat, q_rope, c, kr):
    s = jnp.einsum('bthc,bsc->bhts', q_lat, c) + jnp.einsum('bthr,bsr->bhts', q_rope, kr)
    return s.astype(jnp.float32) * MLA_SCALE


def mla_prompt(q_lat, q_rope, c, kr):
    B, S, H, R = q_lat.shape
    nb = S // Q_BLOCK
    qb = q_lat.reshape(B, nb, Q_BLOCK, H, R).transpose(1, 0, 2, 3, 4)
    rb = q_rope.reshape(B, nb, Q_BLOCK, H, MLA_ROPE).transpose(1, 0, 2, 3, 4)
    kpos = jnp.arange(S)

    def block(args):
        i, ql, qr = args
        qpos = i * Q_BLOCK + jnp.arange(Q_BLOCK)
        s = jnp.where(kpos[None, :] <= qpos[:, None], latent_scores(ql, qr, c, kr), -jnp.inf)
        p = jax.nn.softmax(s, axis=-1).astype(c.dtype)
        return jnp.einsum('bhts,bsc->bthc', p, c)

    o = lax.map(block, (jnp.arange(nb), qb, rb))
    return o.transpose(1, 0, 2, 3, 4).reshape(B, S, H, R)


def mla_sample(q_lat, q_rope, c_new, kr_new, c_past, kr_past):
    T = q_lat.shape[1]
    P = c_past.shape[1]
    s_past = latent_scores(q_lat, q_rope, c_past, kr_past)
    tri = jnp.tril(jnp.ones((T, T), dtype=bool))
    s_new = jnp.where(tri, latent_scores(q_lat, q_rope, c_new, kr_new), -jnp.inf)
    p = jax.nn.softmax(jnp.concatenate([s_past, s_new], axis=-1), axis=-1).astype(c_new.dtype)
    return (jnp.einsum('bhts,bsc->bthc', p[..., :P], c_past)
            + jnp.einsum('bhts,bsc->bthc', p[..., P:], c_new))


def chunk_mlp(u, v, v_norm, w_s, b_s):
    B, T, _ = u.shape
    vn = rmsnorm(v, v_norm)
    n = -(-T // CM_CHUNK)
    pad = n * CM_CHUNK - T
    vp = jnp.pad(vn, ((0, 0), (0, pad), (0, 0))).reshape(B, n, CM_CHUNK, CM_GROUPS, CM_GROUP_DIM)
    w = w_s * jnp.tril(jnp.ones((CM_CHUNK, CM_CHUNK), dtype=w_s.dtype))
    z = jnp.einsum('gts,bnsgc->bntgc', w, vp) + b_s.T[None, None, :, :, None]
    z = z.reshape(B, n * CM_CHUNK, CM_WIDTH)[:, :T]
    return u * z, vn


def conv_ffn(h, prev, w_up, conv_w, conv_b, w_down):
    T = h.shape[1]
    up = h @ w_up
    ext = jnp.concatenate([prev.astype(up.dtype), up], axis=1)
    y = conv_b
    for j in range(CONV_W):
        y = y + conv_w[j] * ext[:, j:j + T]
    gate, val = jnp.split(y, 2, axis=-1)
    out = (jax.nn.gelu(gate) * val) @ w_down
    return out, ext[:, -(CONV_W - 1):]


def trunk_layer(x, pe, pos, hg_s0, conv_prev, past_c, past_kr, lb, lp):
    B, T, _ = x.shape
    h = rmsnorm(x, lp['ln_mix_pre'])
    splits = [int(s) for s in np.cumsum(IN_SIZES)[:-1]]
    hq, hf, hi, hg, qa, kva, kr, gu, gv = jnp.split(h @ lp['w_in'], splits, axis=-1)
    o_a, hg_new = hgrn2_branch(hq, hf, hi, hg, lb, lp['hgrn_norm'], hg_s0)
    q_lat, q_rope, c, k_rope, w_uv = mla_qkv(qa, kva, kr, lp['mla_q_norm'], lp['mla_kv_norm'],
                                            lp['w_q_b'], lp['w_kv_b'], pos)
    if past_c is None:
        o_lat = mla_prompt(q_lat, q_rope, c, k_rope)
    else:
        o_lat = mla_sample(q_lat, q_rope, c, k_rope, past_c, past_kr)
    o_b = jnp.einsum('bthc,chv->bthv', o_lat, w_uv).reshape(B, T, MLA_HEADS * MLA_V)
    o_c, v_rows = chunk_mlp(jax.nn.gelu(gu), jax.nn.gelu(gv), lp['cm_v_norm'], lp['cm_w_s'], lp['cm_b'])
    branches = jnp.stack([o_a, o_b, o_c], axis=2)
    gates = jax.nn.sigmoid(h @ lp['w_merge_gate']).reshape(B, T, N_BRANCH, D_MODEL)
    merged = jnp.einsum('btgc,gcd->btgd', branches, lp['w_branch'])
    mix = jnp.sum(gates * merged, axis=2) @ lp['w_out']
    x = x + rmsnorm(mix, lp['ln_mix_post'])
    f, conv_new = conv_ffn(rmsnorm(x, lp['ln_ffn_pre']), conv_prev, lp['w_ffn_up'],
                           lp['ffn_conv_w'], lp['ffn_conv_b'], lp['w_ffn_down'])
    x = x + rmsnorm(f, lp['ln_ffn_post'])
    x = x + (pe @ lp['w_ple_proj']) * jax.nn.sigmoid(rmsnorm(x, lp['ln_ple']) @ lp['w_ple_gate'])
    return x, c, k_rope, hg_new, conv_new, v_rows


def setup_inputs(seed: int = 0) -> dict:
    key = jax.random.key(seed)
    ks = iter(jax.random.split(key, 48))
    def nrm(shape, scale):
        return jax.random.normal(next(ks), shape, jnp.float32) * scale
    def gain(shape):
        return 1.0 + nrm(shape, 0.02)
    n_pages = PAST_LEN // PAGE_SIZE
    n_pool = (DEC_BATCH * n_pages * 5) // 4
    L = DEPTH
    d = D_MODEL
    return {
        'x_prompt': nrm((BATCH, SEQ, d), 1.0),
        'x_sample': nrm((DEC_BATCH, DEC_SEQ, d), 1.0),
        'cache_latent': nrm((L, n_pool, PAGE_SIZE, MLA_KV_RANK), 1.0),
        'cache_krope': nrm((L, n_pool, PAGE_SIZE, MLA_ROPE), 1.0),
        'state_hgrn': nrm((L, DEC_BATCH, HG_HEADS, HG_KEY, HG_VAL), 0.5),
        'state_ffn_conv': nrm((L, DEC_BATCH, CONV_W - 1, 2 * D_FF), 1.0),
        'page_table': jax.random.permutation(next(ks), n_pool)[:DEC_BATCH * n_pages]
                      .reshape(DEC_BATCH, n_pages).astype(jnp.int32),
        'p_prompt': nrm((L, BATCH, SEQ, PLE_DIM), 1.0),
        'p_sample': nrm((L, DEC_BATCH, DEC_SEQ, PLE_DIM), 1.0),
        'ln_mix_pre': gain((L, d)),
        'ln_mix_post': gain((L, d)),
        'ln_ffn_pre': gain((L, d)),
        'ln_ffn_post': gain((L, d)),
        'ln_ple': gain((L, d)),
        'w_in': nrm((L, d, IN_COLS), d ** -0.5),
        'hgrn_lower_bounds': nrm((L, HG_HEADS * HG_KEY), 0.1),
        'hgrn_norm': gain((L, HG_WIDTH)),
        'mla_q_norm': gain((L, MLA_Q_RANK)),
        'mla_kv_norm': gain((L, MLA_KV_RANK)),
        'w_q_b': nrm((L, MLA_Q_RANK, MLA_HEADS * (MLA_NOPE + MLA_ROPE)), MLA_Q_RANK ** -0.5),
        'w_kv_b': nrm((L, MLA_KV_RANK, MLA_HEADS * (MLA_NOPE + MLA_V)), MLA_KV_RANK ** -0.5),
        'cm_v_norm': gain((L, CM_WIDTH)),
        'cm_w_s': nrm((L, CM_GROUPS, CM_CHUNK, CM_CHUNK), CM_CHUNK ** -0.5),
        'cm_b': gain((L, CM_GROUPS, CM_CHUNK)),
        'w_merge_gate': nrm((L, d, N_BRANCH * d), d ** -0.5),
        'w_branch': nrm((L, N_BRANCH, BRANCH_WIDTH, d), BRANCH_WIDTH ** -0.5),
        'w_out': nrm((L, d, d), d ** -0.5),
        'w_ffn_up': nrm((L, d, 2 * D_FF), d ** -0.5),
        'ffn_conv_w': nrm((L, CONV_W, 2 * D_FF), CONV_W ** -0.5),
        'ffn_conv_b': nrm((L, 2 * D_FF), 0.02),
        'w_ffn_down': nrm((L, D_FF, d), D_FF ** -0.5),
        'w_ple_proj': nrm((L, PLE_DIM, d), PLE_DIM ** -0.5),
        'w_ple_gate': nrm((L, d, d), d ** -0.5),
    }


def reference(x_prompt, x_sample, cache_latent, cache_krope, state_hgrn, state_ffn_conv, page_table,
              p_prompt, p_sample, ln_mix_pre, ln_mix_post, ln_ffn_pre, ln_ffn_post, ln_ple, w_in,
              hgrn_lower_bounds, hgrn_norm, mla_q_norm, mla_kv_norm, w_q_b, w_kv_b, cm_v_norm, cm_w_s,
              cm_b, w_merge_gate, w_branch, w_out, w_ffn_up, ffn_conv_w, ffn_conv_b, w_ffn_down,
              w_ple_proj, w_ple_gate):
    B, S, _ = x_prompt.shape
    DB, T, _ = x_sample.shape
    past_len = page_table.shape[1] * cache_latent.shape[2]
    pos_p = jnp.arange(S, dtype=jnp.int32)
    pos_s = past_len + jnp.arange(T, dtype=jnp.int32)
    lb_soft = jax.nn.softmax(hgrn_lower_bounds.astype(jnp.float32), axis=0)
    lbs = jnp.concatenate([jnp.zeros_like(lb_soft[:1]), jnp.cumsum(lb_soft[1:], axis=0)], axis=0)
    hg0 = jnp.zeros((B, HG_HEADS, HG_KEY, HG_VAL), jnp.float32)
    conv0 = jnp.zeros((B, CONV_W - 1, 2 * D_FF), x_prompt.dtype)
    xp, xs = x_prompt, x_sample
    lat_p, kr_p, hg_p, cv_p = [], [], [], []
    lat_s, kr_s, hg_s, cv_s, v_s = [], [], [], [], []
    for i in range(DEPTH):
        lp = dict(ln_mix_pre=ln_mix_pre[i], ln_mix_post=ln_mix_post[i], ln_ffn_pre=ln_ffn_pre[i],
                  ln_ffn_post=ln_ffn_post[i], ln_ple=ln_ple[i], w_in=w_in[i], hgrn_norm=hgrn_norm[i],
                  mla_q_norm=mla_q_norm[i], mla_kv_norm=mla_kv_norm[i], w_q_b=w_q_b[i], w_kv_b=w_kv_b[i],
                  cm_v_norm=cm_v_norm[i], cm_w_s=cm_w_s[i], cm_b=cm_b[i], w_merge_gate=w_merge_gate[i],
                  w_branch=w_branch[i], w_out=w_out[i], w_ffn_up=w_ffn_up[i], ffn_conv_w=ffn_conv_w[i],
                  ffn_conv_b=ffn_conv_b[i], w_ffn_down=w_ffn_down[i], w_ple_proj=w_ple_proj[i],
                  w_ple_gate=w_ple_gate[i])
        xp, c1, k1, h1, v1, _ = trunk_layer(xp, p_prompt[i], pos_p, hg0, conv0, None, None, lbs[i], lp)
        lat_p.append(c1); kr_p.append(k1); hg_p.append(h1); cv_p.append(v1)
        past_c = cache_latent[i][page_table].reshape(DB, -1, MLA_KV_RANK)
        past_kr = cache_krope[i][page_table].reshape(DB, -1, MLA_ROPE)
        xs, c2, k2, h2, v2, r2 = trunk_layer(xs, p_sample[i], pos_s, state_hgrn[i], state_ffn_conv[i],
                                             past_c, past_kr, lbs[i], lp)
        lat_s.append(c2); kr_s.append(k2); hg_s.append(h2); cv_s.append(v2); v_s.append(r2)
    return (xp, xs, jnp.stack(lat_p), jnp.stack(kr_p), jnp.stack(lat_s), jnp.stack(kr_s),
            jnp.stack(hg_p), jnp.stack(hg_s), jnp.stack(cv_p), jnp.stack(cv_s), jnp.stack(v_s))
```

```python
import functools
import math

import jax
import jax.numpy as jnp
import numpy as np
from jax import lax
from jax.experimental import pallas as pl
from jax.experimental.pallas import tpu as pltpu

F32 = jnp.float32
BF16 = jnp.bfloat16
HIGHEST = lax.Precision.HIGHEST

HG_HEADS = 8
HG_KEY = 128
HG_VAL = 128
MLA_HEADS = 8
MLA_NOPE = 128
MLA_ROPE = 64
MLA_V = 128
MLA_Q_RANK = 768
MLA_KV_RANK = 512
MLA_SCALE = (MLA_NOPE + MLA_ROPE) ** -0.5
ROPE_THETA = 10000.0
CM_GROUPS = 8
CM_CHUNK = 128
N_BRANCH = 3
BRANCH_WIDTH = 1024
CONV_W = 3
EPS = 1e-6

LANES = 128
SUBLANES = 8
VMEM_LIMIT = 56 * 1024 * 1024

COL_HQ, COL_HF, COL_HI, COL_HG, COL_GU, COL_GV = 0, 1024, 2048, 3072, 4096, 5120
COL_QA = 6144
COL_KR = 6912
COL_KVA = 7168
IN_COLS_PERM = 7680

NT_DIMS = (((1,), (1,)), ((), ()))
TN_DIMS = (((0,), (0,)), ((), ()))


def _params(sem):
    return pltpu.CompilerParams(dimension_semantics=sem, vmem_limit_bytes=VMEM_LIMIT)


def _rms(x, g):
    ms = jnp.mean(x * x, axis=-1, keepdims=True)
    return x * lax.rsqrt(ms + EPS) * g


def _gelu(x):
    return 0.5 * x * (1.0 + jnp.tanh(math.sqrt(2.0 / math.pi) * (x + 0.044715 * (x * x * x))))


def _sigmoid(x):
    return 1.0 / (1.0 + jnp.exp(-x))


def _silu(x):
    return x * _sigmoid(x)


def _tile(n, pref):
    t = min(pref, n)
    while n % t:
        t //= 2
    return t


def _norm_matmul_kernel(x_ref, g_ref, w_ref, o_ref, h_ref, *, act):
    @pl.when(pl.program_id(1) == 0)
    def _():
        h_ref[...] = _rms(x_ref[...], g_ref[...]).astype(BF16)

    acc = jnp.dot(h_ref[...], w_ref[...], preferred_element_type=F32)
    if act == "sigmoid":
        acc = _sigmoid(acc)
    o_ref[...] = acc.astype(o_ref.dtype)


def norm_matmul(x, g, w, *, act=None, out_dtype=F32, tm=1024, tn=512):
    m, k = x.shape
    n = w.shape[1]
    tm, tn = _tile(m, tm), _tile(n, tn)
    return pl.pallas_call(
        functools.partial(_norm_matmul_kernel, act=act),
        grid=(m // tm, n // tn),
        in_specs=[
            pl.BlockSpec((tm, k), lambda i, j: (i, 0)),
            pl.BlockSpec((1, k), lambda i, j: (0, 0)),
            pl.BlockSpec((k, tn), lambda i, j: (0, j)),
        ],
        out_specs=pl.BlockSpec((tm, tn), lambda i, j: (i, j)),
        out_shape=jax.ShapeDtypeStruct((m, n), out_dtype),
        scratch_shapes=[pltpu.VMEM((tm, k), BF16)],
        compiler_params=_params(("parallel", "arbitrary")),
        name="norm_matmul",
    )(x, g.reshape(1, k), w)


def _hgrn_kernel(*refs, C, nb, n_chunks, has_s0):
    if has_s0:
        hq_ref, hf_ref, hi_ref, hg_ref, lb_ref, ng_ref, s0_ref, o_ref, snew_ref, st_ref = refs
    else:
        hq_ref, hf_ref, hi_ref, hg_ref, lb_ref, ng_ref, o_ref, snew_ref, st_ref = refs
    R = nb * C
    ci = pl.program_id(2)

    @pl.when(ci == 0)
    def _():
        for n in range(nb):
            if has_s0:
                st_ref[n] = s0_ref[n, 0].T
            else:
                st_ref[n] = jnp.zeros((HG_VAL, HG_KEY), F32)

    lb = lb_ref[0:1, :]
    log_lb = lb_ref[1:2, :]
    log_1m_lb = lb_ref[2:3, :]
    f = hf_ref[...]
    q = _silu(hq_ref[...])
    v = hi_ref[...]
    log_sig = jnp.minimum(f, 0.0) - jnp.log1p(jnp.exp(-jnp.abs(f)))
    c2 = log_1m_lb + log_sig
    logf = jnp.maximum(log_lb, c2) + jnp.log1p(jnp.exp(-jnp.abs(log_lb - c2)))
    k = (1.0 - lb) * (1.0 / (1.0 + jnp.exp(f)))

    row = lax.broadcasted_iota(jnp.int32, (R, R), 0)
    col = lax.broadcasted_iota(jnp.int32, (R, R), 1)
    tri = jnp.where((col <= row) & ((row // C) == (col // C)), 1.0, 0.0).astype(F32)
    b = jnp.dot(tri, logf, precision=HIGHEST, preferred_element_type=F32)

    G = R // SUBLANES
    q3 = q.reshape(G, SUBLANES, HG_KEY)
    k3 = k.reshape(G, SUBLANES, HG_KEY)
    b3 = b.reshape(G, SUBLANES, HG_KEY)
    v3 = v.reshape(G, SUBLANES, HG_VAL)
    tpos = lax.broadcasted_iota(jnp.int32, (G, SUBLANES, 1), 1)
    o3 = jnp.zeros((G, SUBLANES, HG_VAL), F32)
    for s in range(SUBLANES):
        dec = jnp.exp(jnp.minimum(b3 - b3[:, s:s + 1, :], 0.0))
        a_s = jnp.sum(q3 * k3[:, s:s + 1, :] * dec, axis=-1, keepdims=True)
        o3 = o3 + jnp.where(tpos >= s, a_s, 0.0) * v3[:, s:s + 1, :]
    o = o3.reshape(R, HG_VAL)

    half = C // 2
    a_off = None
    rrow = lax.broadcasted_iota(jnp.int32, (R, 1), 0)
    while half >= SUBLANES:
        blk = 2 * half
        nblk = R // blk
        b4 = b.reshape(nblk, blk, HG_KEY)
        ref_b = jnp.broadcast_to(b4[:, half - 1:half, :], (nblk, blk, HG_KEY)).reshape(R, HG_KEY)
        second = (rrow % blk) >= half
        qe = jnp.where(second, q * jnp.exp(jnp.minimum(b - ref_b, 0.0)), 0.0)
        ke = jnp.where(second, 0.0, k * jnp.exp(jnp.minimum(ref_b - b, 0.0)))
        a_l = lax.dot_general(qe, ke, NT_DIMS, precision=HIGHEST, preferred_element_type=F32)
        if blk != R:
            a_l = jnp.where((row // blk) == (col // blk), a_l, 0.0)
        a_off = a_l if a_off is None else a_off + a_l
        half //= 2
    if a_off is not None:
        o = o + jnp.dot(a_off, v, precision=HIGHEST, preferred_element_type=F32)

    qb = q * jnp.exp(b)
    o_parts = []
    for n in range(nb):
        lo, hi = n * C, (n + 1) * C
        st = st_ref[n]
        o_parts.append(lax.dot_general(qb[lo:hi], st, NT_DIMS, precision=HIGHEST, preferred_element_type=F32))
        b_last = b[hi - 1:hi, :]
        kd = k[lo:hi] * jnp.exp(b_last - b[lo:hi])
        st_ref[n] = st * jnp.exp(b_last) + lax.dot_general(
            v[lo:hi], kd, TN_DIMS, precision=HIGHEST, preferred_element_type=F32)
    o = o + (o_parts[0] if nb == 1 else jnp.concatenate(o_parts, axis=0))

    o = _rms(o, ng_ref[...]) * _silu(hg_ref[...])
    o_ref[...] = o.astype(o_ref.dtype)

    @pl.when(ci == n_chunks - 1)
    def _():
        for n in range(nb):
            snew_ref[n, 0] = st_ref[n].T


def hgrn2(proj, row0, n_seq, seq_len, lb3, norm_g, s0, *, C, nb):
    R = nb * C
    n_chunks = seq_len // C
    assert seq_len % C == 0 and n_seq % nb == 0 and row0 % R == 0
    assert nb == 1 or n_chunks == 1
    rb0 = row0 // R
    rows_per_seqblock = nb * seq_len // R

    def rmap(sec):
        return lambda s, h, c: (rb0 + s * rows_per_seqblock + c, sec * HG_HEADS + h)

    in_specs = [pl.BlockSpec((R, HG_KEY), rmap(sec)) for sec in range(4)]
    in_specs += [pl.BlockSpec((3, HG_KEY), lambda s, h, c: (0, h)),
                 pl.BlockSpec((1, HG_VAL), lambda s, h, c: (0, h))]
    args = [proj, proj, proj, proj, lb3, norm_g.reshape(1, -1)]
    if s0 is not None:
        in_specs.append(pl.BlockSpec((nb, 1, HG_KEY, HG_VAL), lambda s, h, c: (s, h, 0, 0)))
        args.append(s0)
    return pl.pallas_call(
        functools.partial(_hgrn_kernel, C=C, nb=nb, n_chunks=n_chunks, has_s0=s0 is not None),
        grid=(n_seq // nb, HG_HEADS, n_chunks),
        in_specs=in_specs,
        out_specs=[pl.BlockSpec((R, HG_VAL), lambda s, h, c: (s * rows_per_seqblock + c, h)),
                   pl.BlockSpec((nb, 1, HG_KEY, HG_VAL), lambda s, h, c: (s, h, 0, 0))],
        out_shape=[jax.ShapeDtypeStruct((n_seq * seq_len, HG_HEADS * HG_VAL), BF16),
                   jax.ShapeDtypeStruct((n_seq, HG_HEADS, HG_KEY, HG_VAL), F32)],
        scratch_shapes=[pltpu.VMEM((nb, HG_VAL, HG_KEY), F32)],
        compiler_params=_params(("parallel", "parallel", "arbitrary")),
        name="hgrn2",
    )(*args)


def _rope_pair(slot, tab):
    prod = slot * tab
    return prod + pltpu.roll(prod, MLA_ROPE, 1)


def _mla_q_kernel(x_ref, g_ref, w_ref, tab_ref, o_ref):
    qa = x_ref[...][:, :MLA_Q_RANK]
    hn = _rms(qa, g_ref[...]).astype(BF16)
    acc = jnp.dot(hn, w_ref[...], preferred_element_type=F32)
    tab = tab_ref[...]
    lane = lax.broadcasted_iota(jnp.int32, tab.shape, 1)
    nope_w = MLA_HEADS * MLA_NOPE
    for h in range(MLA_HEADS):
        slot = acc[:, nope_w + h * LANES: nope_w + (h + 1) * LANES]
        qr = jnp.where(lane < MLA_ROPE, _rope_pair(slot, tab), 0.0)
        o_ref[:, h * 256: h * 256 + MLA_NOPE] = acc[:, h * MLA_NOPE:(h + 1) * MLA_NOPE].astype(o_ref.dtype)
        o_ref[:, h * 256 + MLA_NOPE:(h + 1) * 256] = qr.astype(o_ref.dtype)


def mla_q(proj, q_norm, w_q_ext, tab, *, tm=512):
    m = proj.shape[0]
    tm = _tile(m, tm)
    n = w_q_ext.shape[1]
    return pl.pallas_call(
        _mla_q_kernel,
        grid=(m // tm,),
        in_specs=[pl.BlockSpec((tm, 1024), lambda i: (i, COL_QA // 1024)),
                  pl.BlockSpec((1, MLA_Q_RANK), lambda i: (0, 0)),
                  pl.BlockSpec((MLA_Q_RANK, n), lambda i: (0, 0)),
                  pl.BlockSpec((tm, LANES), lambda i: (i, 0))],
        out_specs=pl.BlockSpec((tm, MLA_HEADS * 256), lambda i: (i, 0)),
        out_shape=jax.ShapeDtypeStruct((m, MLA_HEADS * 256), BF16),
        compiler_params=_params(("parallel",)),
        name="mla_q",
    )(proj, q_norm.reshape(1, -1), w_q_ext, tab)


def _mla_kv_kernel(kva_ref, kr_ref, g_ref, w_ref, tab_ref, c_ref, krot_ref, kfull_ref, v_ref):
    c = _rms(kva_ref[...], g_ref[...])
    c_ref[...] = c
    krkr = _rope_pair(kr_ref[...], tab_ref[...])
    krot_ref[...] = krkr[:, :MLA_ROPE]
    acc = jnp.dot(c.astype(BF16), w_ref[...], preferred_element_type=F32)
    krkr_b = krkr.astype(BF16)
    for h in range(MLA_HEADS):
        kfull_ref[:, h * 256: h * 256 + MLA_NOPE] = acc[:, h * 256: h * 256 + MLA_NOPE].astype(BF16)
        kfull_ref[:, h * 256 + MLA_NOPE:(h + 1) * 256] = krkr_b
        v_ref[:, h * MLA_V:(h + 1) * MLA_V] = acc[:, h * 256 + MLA_NOPE:(h + 1) * 256].astype(BF16)


def mla_kv(proj, kv_norm, w_kv_b, tab, *, tm=512):
    m = proj.shape[0]
    tm = _tile(m, tm)
    return pl.pallas_call(
        _mla_kv_kernel,
        grid=(m // tm,),
        in_specs=[pl.BlockSpec((tm, MLA_KV_RANK), lambda i: (i, COL_KVA // MLA_KV_RANK)),
                  pl.BlockSpec((tm, LANES), lambda i: (i, COL_KR // LANES)),
                  pl.BlockSpec((1, MLA_KV_RANK), lambda i: (0, 0)),
                  pl.BlockSpec((MLA_KV_RANK, MLA_HEADS * 256), lambda i: (0, 0)),
                  pl.BlockSpec((tm, LANES), lambda i: (i, 0))],
        out_specs=[pl.BlockSpec((tm, MLA_KV_RANK), lambda i: (i, 0)),
                   pl.BlockSpec((tm, MLA_ROPE), lambda i: (i, 0)),
                   pl.BlockSpec((tm, MLA_HEADS * 256), lambda i: (i, 0)),
                   pl.BlockSpec((tm, MLA_HEADS * MLA_V), lambda i: (i, 0))],
        out_shape=[jax.ShapeDtypeStruct((m, MLA_KV_RANK), F32),
                   jax.ShapeDtypeStruct((m, MLA_ROPE), F32),
                   jax.ShapeDtypeStruct((m, MLA_HEADS * 256), BF16),
                   jax.ShapeDtypeStruct((m, MLA_HEADS * MLA_V), BF16)],
        compiler_params=_params(("parallel",)),
        name="mla_kv",
    )(proj, proj, kv_norm.reshape(1, -1), w_kv_b, tab)


def _flash_kernel(q_ref, k_ref, v_ref, o_ref, m_ref, l_ref, acc_ref, *, tq, tk):
    qi = pl.program_id(2)
    ki = pl.program_id(3)
    last_k = ((qi + 1) * tq - 1) // tk

    @pl.when(ki == 0)
    def _():
        m_ref[...] = jnp.full(m_ref.shape, -jnp.inf, F32)
        l_ref[...] = jnp.zeros(l_ref.shape, F32)
        acc_ref[...] = jnp.zeros(acc_ref.shape, F32)

    @pl.when(ki <= last_k)
    def _():
        s = lax.dot_general(q_ref[...], k_ref[...], NT_DIMS, preferred_element_type=F32) * MLA_SCALE
        qpos = qi * tq + lax.broadcasted_iota(jnp.int32, (tq, tk), 0)
        kpos = ki * tk + lax.broadcasted_iota(jnp.int32, (tq, tk), 1)
        s = jnp.where(kpos <= qpos, s, -jnp.inf)
        m_prev = m_ref[...]
        m_new = jnp.maximum(m_prev, jnp.max(s, axis=-1, keepdims=True))
        alpha = jnp.exp(m_prev - m_new)
        p = jnp.exp(s - m_new)
        l_ref[...] = alpha * l_ref[...] + jnp.sum(p, axis=-1, keepdims=True)
        acc_ref[...] = alpha * acc_ref[...] + jnp.dot(p.astype(BF16), v_ref[...], preferred_element_type=F32)
        m_ref[...] = m_new

    @pl.when(ki == last_k)
    def _():
        o_ref[...] = (acc_ref[...] / l_ref[...]).astype(o_ref.dtype)


def mla_prompt_attention(q_full, k_full, v_all, n_batch, seq, *, tq=512, tk=512):
    tq, tk = _tile(seq, tq), _tile(seq, tk)
    nq, nk = seq // tq, seq // tk

    def kmap(b, h, qi, ki):
        return (b * nk + jnp.minimum(ki, ((qi + 1) * tq - 1) // tk), h)

    return pl.pallas_call(
        functools.partial(_flash_kernel, tq=tq, tk=tk),
        grid=(n_batch, MLA_HEADS, nq, nk),
        in_specs=[pl.BlockSpec((tq, 256), lambda b, h, qi, ki: (b * nq + qi, h)),
                  pl.BlockSpec((tk, 256), kmap),
                  pl.BlockSpec((tk, MLA_V), kmap)],
        out_specs=pl.BlockSpec((tq, MLA_V), lambda b, h, qi, ki: (b * nq + qi, h)),
        out_shape=jax.ShapeDtypeStruct((n_batch * seq, MLA_HEADS * MLA_V), BF16),
        scratch_shapes=[pltpu.VMEM((tq, 1), F32), pltpu.VMEM((tq, 1), F32), pltpu.VMEM((tq, MLA_V), F32)],
        compiler_params=_params(("parallel", "parallel", "parallel", "arbitrary")),
        name="mla_prompt_flash",
    )(q_full, k_full, v_all)


def _absorb_kernel(q_ref, w_ref, o_ref):
    qh = q_ref[...]
    q_lat = lax.dot_general(qh[:, :MLA_NOPE], w_ref[...], NT_DIMS, preferred_element_type=F32)
    out = jnp.concatenate([q_lat, qh[:, MLA_NOPE:].astype(F32)], axis=-1)
    o_ref[...] = out.reshape(o_ref.shape)


def mla_absorb(q_full, w_kv_b, row0, n_seq, t_len):
    rows = n_seq * t_len
    assert row0 % rows == 0
    width = MLA_KV_RANK + LANES
    return pl.pallas_call(
        _absorb_kernel,
        grid=(MLA_HEADS,),
        in_specs=[pl.BlockSpec((rows, 256), lambda h: (row0 // rows, h)),
                  pl.BlockSpec((MLA_KV_RANK, MLA_NOPE), lambda h: (0, 2 * h))],
        out_specs=pl.BlockSpec((n_seq, None, t_len, width), lambda h: (0, h, 0, 0)),
        out_shape=jax.ShapeDtypeStruct((n_seq, MLA_HEADS, t_len, width), F32),
        compiler_params=_params(("parallel",)),
        name="mla_absorb",
    )(q_full, w_kv_b)


def _paged_attn_kernel(pt_ref, q_ref, cnew_ref, krnew_ref, *rest, G, t_len, n_steps):
    c_pages = rest[:G]
    kr_pages = rest[G:2 * G]
    o_ref, cbuf, krbuf, m_ref, l_ref, acc_ref = rest[2 * G:]
    g = pl.program_id(1)
    page = cbuf.shape[0] // G
    q = q_ref[0]
    q_lat = q[:, :MLA_KV_RANK]
    q_rope = q[:, MLA_KV_RANK:MLA_KV_RANK + MLA_ROPE]

    @pl.when(g == 0)
    def _():
        c_new = cnew_ref[...]
        kr_new = krnew_ref[...]
        tq = lax.broadcasted_iota(jnp.int32, (q.shape[0], 1), 0) % t_len
        cols = []
        for j in range(t_len):
            sj = (jnp.sum(q_lat * c_new[j:j + 1, :], axis=-1, keepdims=True)
                  + jnp.sum(q_rope * kr_new[j:j + 1, :], axis=-1, keepdims=True)) * MLA_SCALE
            cols.append(jnp.where(tq >= j, sj, -jnp.inf))
        m0 = cols[0]
        for j in range(1, t_len):
            m0 = jnp.maximum(m0, cols[j])
        l0 = jnp.zeros_like(m0)
        acc0 = jnp.zeros(acc_ref.shape, F32)
        for j in range(t_len):
            pj = jnp.exp(cols[j] - m0)
            l0 = l0 + pj
            acc0 = acc0 + pj * c_new[j:j + 1, :]
        m_ref[...] = m0
        l_ref[...] = l0
        acc_ref[...] = acc0

    for i in range(G):
        cbuf[i * page:(i + 1) * page, :] = c_pages[i][...].astype(BF16)
        krbuf[i * page:(i + 1) * page, :] = kr_pages[i][...].astype(BF16)
    kc = cbuf[...]
    s = (lax.dot_general(q_lat.astype(BF16), kc, NT_DIMS, preferred_element_type=F32)
         + lax.dot_general(q_rope.astype(BF16), krbuf[...], NT_DIMS, preferred_element_type=F32)) * MLA_SCALE
    m_prev = m_ref[...]
    m_new = jnp.maximum(m_prev, jnp.max(s, axis=-1, keepdims=True))
    alpha = jnp.exp(m_prev - m_new)
    p = jnp.exp(s - m_new)
    l_ref[...] = alpha * l_ref[...] + jnp.sum(p, axis=-1, keepdims=True)
    acc_ref[...] = alpha * acc_ref[...] + jnp.dot(p.astype(BF16), kc, preferred_element_type=F32)
    m_ref[...] = m_new

    @pl.when(g == n_steps - 1)
    def _():
        o_ref[0] = acc_ref[...] / l_ref[...]


def mla_paged_attention(q_abs, c_all, kr_all, row0, cache_latent, cache_krope, layer, page_table, *, G=16):
    n_seq, heads, t_len, width = q_abs.shape
    n_pages = page_table.shape[1]
    page = cache_latent.shape[2]
    G = _tile(n_pages, G)
    n_steps = n_pages // G
    rows = heads * t_len
    q3 = q_abs.reshape(n_seq, rows, width)
    assert row0 % t_len == 0

    def pmap(i):
        return lambda b, g, pt: (layer, pt[b, g * G + i], 0, 0)

    in_specs = [pl.BlockSpec((1, rows, width), lambda b, g, pt: (b, 0, 0)),
                pl.BlockSpec((t_len, MLA_KV_RANK), lambda b, g, pt: (row0 // t_len + b, 0)),
                pl.BlockSpec((t_len, MLA_ROPE), lambda b, g, pt: (row0 // t_len + b, 0))]
    in_specs += [pl.BlockSpec((None, None, page, MLA_KV_RANK), pmap(i)) for i in range(G)]
    in_specs += [pl.BlockSpec((None, None, page, MLA_ROPE), pmap(i)) for i in range(G)]
    grid_spec = pltpu.PrefetchScalarGridSpec(
        num_scalar_prefetch=1,
        grid=(n_seq, n_steps),
        in_specs=in_specs,
        out_specs=pl.BlockSpec((1, rows, MLA_KV_RANK), lambda b, g, pt: (b, 0, 0)),
        scratch_shapes=[pltpu.VMEM((G * page, MLA_KV_RANK), BF16), pltpu.VMEM((G * page, MLA_ROPE), BF16),
                        pltpu.VMEM((rows, 1), F32), pltpu.VMEM((rows, 1), F32),
                        pltpu.VMEM((rows, MLA_KV_RANK), F32)],
    )
    return pl.pallas_call(
        functools.partial(_paged_attn_kernel, G=G, t_len=t_len, n_steps=n_steps),
        grid_spec=grid_spec,
        out_shape=jax.ShapeDtypeStruct((n_seq, rows, MLA_KV_RANK), F32),
        compiler_params=_params(("parallel", "arbitrary")),
        name="mla_paged_attention",
    )(page_table, q3, c_all, kr_all, *([cache_latent] * G), *([cache_krope] * G))


def _uv_kernel(o_ref, w_ref, out_ref):
    o = o_ref[...]
    o2 = o.reshape(o.shape[0] * o.shape[1], o.shape[2]).astype(BF16)
    out_ref[...] = jnp.dot(o2, w_ref[...], preferred_element_type=F32).astype(out_ref.dtype)


def mla_uv(o_lat, w_kv_b, n_seq, t_len):
    o4 = o_lat.reshape(n_seq, MLA_HEADS, t_len, MLA_KV_RANK)
    rows = n_seq * t_len
    return pl.pallas_call(
        _uv_kernel,
        grid=(MLA_HEADS,),
        in_specs=[pl.BlockSpec((n_seq, None, t_len, MLA_KV_RANK), lambda h: (0, h, 0, 0)),
                  pl.BlockSpec((MLA_KV_RANK, MLA_V), lambda h: (0, 2 * h + 1))],
        out_specs=pl.BlockSpec((rows, MLA_V), lambda h: (0, h)),
        out_shape=jax.ShapeDtypeStruct((rows, MLA_HEADS * MLA_V), BF16),
        compiler_params=_params(("parallel",)),
        name="mla_uv",
    )(o4, w_kv_b)


def _gmlp_kernel(gu_ref, gv_ref, g_ref, w_ref, bias_ref, o_ref, vn_ref):
    vn = _rms(_gelu(gv_ref[...]), g_ref[...])
    vn_ref[...] = vn
    u = _gelu(gu_ref[...])
    for grp in range(CM_GROUPS):
        sl = slice(grp * LANES, (grp + 1) * LANES)
        z = jnp.dot(w_ref[grp], vn[:, sl], precision=HIGHEST, preferred_element_type=F32) + bias_ref[:, sl]
        o_ref[:, sl] = (u[:, sl] * z).astype(o_ref.dtype)


def gmlp(proj, row0, n_rows, v_norm, w_mix, bias_full):
    width = CM_GROUPS * LANES
    assert row0 % CM_CHUNK == 0 and n_rows % CM_CHUNK == 0
    rb0 = row0 // CM_CHUNK
    return pl.pallas_call(
        _gmlp_kernel,
        grid=(n_rows // CM_CHUNK,),
        in_specs=[pl.BlockSpec((CM_CHUNK, width), lambda i: (rb0 + i, COL_GU // width)),
                  pl.BlockSpec((CM_CHUNK, width), lambda i: (rb0 + i, COL_GV // width)),
                  pl.BlockSpec((1, width), lambda i: (0, 0)),
                  pl.BlockSpec((CM_GROUPS, CM_CHUNK, CM_CHUNK), lambda i: (0, 0, 0)),
                  pl.BlockSpec((CM_CHUNK, width), lambda i: (0, 0))],
        out_specs=[pl.BlockSpec((CM_CHUNK, width), lambda i: (i, 0)),
                   pl.BlockSpec((CM_CHUNK, width), lambda i: (i, 0))],
        out_shape=[jax.ShapeDtypeStruct((n_rows, width), BF16),
                   jax.ShapeDtypeStruct((n_rows, width), F32)],
        compiler_params=_params(("parallel",)),
        name="gmlp",
    )(proj, proj, v_norm.reshape(1, -1), w_mix, bias_full)


def _merge_kernel(oa_ref, ob_ref, oc_ref, gate_ref, w_ref, o_ref, *, tn):
    acc = None
    for br, b_ref in enumerate((oa_ref, ob_ref, oc_ref)):
        part = gate_ref[:, br * tn:(br + 1) * tn] * jnp.dot(b_ref[...], w_ref[br], preferred_element_type=F32)
        acc = part if acc is None else acc + part
    o_ref[...] = acc.astype(o_ref.dtype)


def branch_merge(o_a, o_b, o_c, gates3, w_branch, *, tm=1024, tn=512):
    m, bw = o_a.shape
    d = w_branch.shape[2]
    tm, tn = _tile(m, tm), _tile(d, tn)
    bspec = pl.BlockSpec((tm, bw), lambda i, j: (i, 0))
    return pl.pallas_call(
        functools.partial(_merge_kernel, tn=tn),
        grid=(m // tm, d // tn),
        in_specs=[bspec, bspec, bspec,
                  pl.BlockSpec((tm, 3 * tn), lambda i, j: (i, j)),
                  pl.BlockSpec((N_BRANCH, bw, tn), lambda i, j: (0, 0, j))],
        out_specs=pl.BlockSpec((tm, tn), lambda i, j: (i, j)),
        out_shape=jax.ShapeDtypeStruct((m, d), BF16),
        compiler_params=_params(("parallel", "arbitrary")),
        name="branch_merge",
    )(o_a, o_b, o_c, gates3, w_branch)


def _out_proj_kernel(mix_ref, x_ref, w_ref, g_ref, o_ref):
    y = jnp.dot(mix_ref[...], w_ref[...], preferred_element_type=F32)
    o_ref[...] = x_ref[...] + _rms(y, g_ref[...])


def out_proj(mix, x, w_out, g, *, tm=512):
    m, d = x.shape
    tm = _tile(m, tm)
    return pl.pallas_call(
        _out_proj_kernel,
        grid=(m // tm,),
        in_specs=[pl.BlockSpec((tm, d), lambda i: (i, 0)),
                  pl.BlockSpec((tm, d), lambda i: (i, 0)),
                  pl.BlockSpec((d, d), lambda i: (0, 0)),
                  pl.BlockSpec((1, d), lambda i: (0, 0))],
        out_specs=pl.BlockSpec((tm, d), lambda i: (i, 0)),
        out_shape=jax.ShapeDtypeStruct((m, d), F32),
        compiler_params=_params(("parallel",)),
        name="out_proj",
    )(mix, x, w_out, g.reshape(1, -1))


def _ffn_down_kernel(ug_ref, uv_ref, pg_ref, pv_ref, cw_ref, cb_ref, w_ref, x_ref, g_ref,
                     o_ref, acc_ref, *, t_len, n_k, per_row_prev):
    kk = pl.program_id(1)
    tm = ug_ref.shape[0]
    tpos = lax.broadcasted_iota(jnp.int32, (tm, 1), 0) % t_len

    def conv(u_ref, p_ref, which):
        u = u_ref[...]
        if per_row_prev:
            back1, back2 = p_ref[0], p_ref[1]
        else:
            back1 = p_ref[1:2, :]
            back2 = jnp.where(tpos == 0, p_ref[0:1, :], p_ref[1:2, :])
        m1 = jnp.where(tpos == 0, back1, pltpu.roll(u, 1, 0))
        m2 = jnp.where(tpos <= 1, back2, pltpu.roll(u, 2, 0))
        return (cb_ref[which:which + 1, :] + cw_ref[which, 0:1, :] * m2 + cw_ref[which, 1:2, :] * m1
                + cw_ref[which, 2:3, :] * u)

    gate = conv(ug_ref, pg_ref, 0)
    val = conv(uv_ref, pv_ref, 1)
    act = (_gelu(gate) * val).astype(BF16)
    part = jnp.dot(act, w_ref[...], preferred_element_type=F32)

    @pl.when(kk == 0)
    def _():
        acc_ref[...] = part

    @pl.when(kk > 0)
    def _():
        acc_ref[...] += part

    @pl.when(kk == n_k - 1)
    def _():
        o_ref[...] = x_ref[...] + _rms(acc_ref[...], g_ref[...])


def ffn_down(up, prev, conv_w2, conv_b2, w_down, x, g, row0, n_rows, t_len, *, tm=512, tk=512):
    d_ff, d = w_down.shape
    per_row_prev = t_len < tm
    tm = _tile(n_rows if per_row_prev else t_len, tm)
    tk = _tile(d_ff, tk)
    n_k = d_ff // tk
    assert row0 % tm == 0 and (tm % t_len == 0 or t_len % tm == 0)
    rb0 = row0 // tm
    g_spec = pl.BlockSpec((tm, tk), lambda i, k: (rb0 + i, k))
    v_spec = pl.BlockSpec((tm, tk), lambda i, k: (rb0 + i, n_k + k))
    if per_row_prev:
        pg_spec = pl.BlockSpec((2, tm, tk), lambda i, k: (0, i, k))
        pv_spec = pl.BlockSpec((2, tm, tk), lambda i, k: (0, i, n_k + k))
    else:
        pg_spec = pl.BlockSpec((None, 2, tk), lambda i, k: (i, 0, k))
        pv_spec = pl.BlockSpec((None, 2, tk), lambda i, k: (i, 0, n_k + k))
    return pl.pallas_call(
        functools.partial(_ffn_down_kernel, t_len=t_len, n_k=n_k, per_row_prev=per_row_prev),
        grid=(n_rows // tm, n_k),
        in_specs=[g_spec, v_spec, pg_spec, pv_spec,
                  pl.BlockSpec((2, CONV_W, tk), lambda i, k: (0, 0, k)),
                  pl.BlockSpec((2, tk), lambda i, k: (0, k)),
                  pl.BlockSpec((tk, d), lambda i, k: (k, 0)),
                  pl.BlockSpec((tm, d), lambda i, k: (rb0 + i, 0)),
                  pl.BlockSpec((1, d), lambda i, k: (0, 0))],
        out_specs=pl.BlockSpec((tm, d), lambda i, k: (i, 0)),
        out_shape=jax.ShapeDtypeStruct((n_rows, d), F32),
        scratch_shapes=[pltpu.VMEM((tm, d), F32)],
        compiler_params=_params(("parallel", "arbitrary")),
        name="ffn_down",
    )(up, up, prev, prev, conv_w2, conv_b2, w_down, x, g.reshape(1, -1))


def _ple_kernel(x_ref, pe_ref, g_ref, wg_ref, wp_ref, o_ref):
    x = x_ref[...]
    gate = _sigmoid(jnp.dot(_rms(x, g_ref[...]).astype(BF16), wg_ref[...], preferred_element_type=F32))
    emb = jnp.dot(pe_ref[...].astype(BF16), wp_ref[...], preferred_element_type=F32)
    o_ref[...] = x + emb * gate


def ple(x, pe, g, w_gate, w_proj, *, tm=512):
    m, d = x.shape
    pd = pe.shape[1]
    tm = _tile(m, tm)
    return pl.pallas_call(
        _ple_kernel,
        grid=(m // tm,),
        in_specs=[pl.BlockSpec((tm, d), lambda i: (i, 0)),
                  pl.BlockSpec((tm, pd), lambda i: (i, 0)),
                  pl.BlockSpec((1, d), lambda i: (0, 0)),
                  pl.BlockSpec((d, d), lambda i: (0, 0)),
                  pl.BlockSpec((pd, d), lambda i: (0, 0))],
        out_specs=pl.BlockSpec((tm, d), lambda i: (i, 0)),
        out_shape=jax.ShapeDtypeStruct((m, d), F32),
        compiler_params=_params(("parallel",)),
        name="ple",
    )(x, pe, g.reshape(1, -1), w_gate, w_proj)


def _rope_table(pos):
    inv = 1.0 / (ROPE_THETA ** (jnp.arange(0, MLA_ROPE, 2, dtype=F32) / MLA_ROPE))
    ang = pos.astype(F32)[:, None] * inv[None, :]
    cos, sin = jnp.cos(ang), jnp.sin(ang)
    return jnp.concatenate([cos, cos, -sin, sin], axis=-1)


def _swap_halves(w, width):
    half = width // 2
    return jnp.concatenate([w[..., half:], w[..., :half]], axis=-1)


def kernel(x_prompt, x_sample, cache_latent, cache_krope, state_hgrn, state_ffn_conv, page_table, p_prompt, p_sample, ln_mix_pre, ln_mix_post, ln_ffn_pre, ln_ffn_post, ln_ple, w_in, hgrn_lower_bounds, hgrn_norm, mla_q_norm, mla_kv_norm, w_q_b, w_kv_b, cm_v_norm, cm_w_s, cm_b, w_merge_gate, w_branch, w_out, w_ffn_up, ffn_conv_w, ffn_conv_b, w_ffn_down, w_ple_proj, w_ple_gate):
    n_b, seq, d = x_prompt.shape
    n_db, t_len, _ = x_sample.shape
    depth = w_in.shape[0]
    d_ff = w_ffn_down.shape[1]
    page = cache_latent.shape[2]
    past_len = page_table.shape[1] * page
    rows_p, rows_s = n_b * seq, n_db * t_len
    hk = HG_HEADS * HG_KEY

    lb_soft = jax.nn.softmax(hgrn_lower_bounds.astype(F32), axis=0)
    lbs = jnp.concatenate([jnp.zeros_like(lb_soft[:1]), jnp.cumsum(lb_soft[1:], axis=0)], axis=0)

    tab = jnp.concatenate([jnp.tile(_rope_table(jnp.arange(seq)), (n_b, 1)),
                           jnp.tile(_rope_table(past_len + jnp.arange(t_len)), (n_db, 1))], axis=0)

    x = jnp.concatenate([x_prompt.reshape(rows_p, d), x_sample.reshape(rows_s, d)], axis=0)
    outs = {k: [] for k in ("lat_p", "kr_p", "lat_s", "kr_s", "hg_p", "hg_s", "cv_p", "cv_s", "v_s")}
    tri = jnp.tril(jnp.ones((CM_CHUNK, CM_CHUNK), F32))
    seq_per_chunk = CM_CHUNK // t_len

    for i in range(depth):
        wi = w_in[i]
        kr_cols = wi[:, 4 * hk + MLA_Q_RANK + MLA_KV_RANK: 4 * hk + MLA_Q_RANK + MLA_KV_RANK + MLA_ROPE]
        gu_gv = wi[:, 4 * hk + MLA_Q_RANK + MLA_KV_RANK + MLA_ROPE:]
        w_in_p = jnp.concatenate(
            [wi[:, :4 * hk], gu_gv, wi[:, 4 * hk: 4 * hk + MLA_Q_RANK], kr_cols, _swap_halves(kr_cols, MLA_ROPE),
             jnp.zeros((d, LANES), F32), wi[:, 4 * hk + MLA_Q_RANK: 4 * hk + MLA_Q_RANK + MLA_KV_RANK]],
            axis=1).astype(BF16)
        wq = w_q_b[i].reshape(MLA_Q_RANK, MLA_HEADS, MLA_NOPE + MLA_ROPE)
        wq_rope = wq[..., MLA_NOPE:]
        w_q_ext = jnp.concatenate(
            [wq[..., :MLA_NOPE].reshape(MLA_Q_RANK, -1),
             jnp.concatenate([wq_rope, _swap_halves(wq_rope, MLA_ROPE)], axis=-1).reshape(MLA_Q_RANK, -1)],
            axis=1).astype(BF16)
        w_kvb = w_kv_b[i].astype(BF16)
        lb = lbs[i]
        lb3 = jnp.stack([lb, jnp.log(lb), jnp.log1p(-lb)], axis=0)
        w_mix_p = cm_w_s[i] * tri
        w8 = w_mix_p[:, :t_len, :t_len]
        w_mix_s = jnp.einsum("ab,gts->gatbs", jnp.eye(seq_per_chunk, dtype=F32), w8).reshape(
            CM_GROUPS, CM_CHUNK, CM_CHUNK)
        bias_p = jnp.repeat(cm_b[i].T, LANES, axis=1)
        bias_s = jnp.tile(bias_p[:t_len], (seq_per_chunk, 1))
        tn_m = _tile(d, 512)
        conv_w2 = ffn_conv_w[i].reshape(CONV_W, 2, d_ff).transpose(1, 0, 2)
        conv_b2 = ffn_conv_b[i].reshape(2, d_ff)

        proj = norm_matmul(x, ln_mix_pre[i], w_in_p)
        w_mg = w_merge_gate[i].reshape(d, N_BRANCH, d // tn_m, tn_m).transpose(0, 2, 1, 3).reshape(d, -1)
        gates3 = norm_matmul(x, ln_mix_pre[i], w_mg.astype(BF16), act="sigmoid")

        oa_p, hgp = hgrn2(proj, 0, n_b, seq, lb3, hgrn_norm[i], None, C=_tile(seq, 128), nb=1)
        oa_s, hgs = hgrn2(proj, rows_p, n_db, t_len, lb3, hgrn_norm[i], state_hgrn[i], C=t_len,
                          nb=_tile(n_db, LANES // t_len))

        q_full = mla_q(proj, mla_q_norm[i], w_q_ext, tab)
        c_all, kr_all, k_full, v_all = mla_kv(proj, mla_kv_norm[i], w_kvb, tab)
        ob_p = mla_prompt_attention(q_full, k_full, v_all, n_b, seq)
        q_abs = mla_absorb(q_full, w_kvb, rows_p, n_db, t_len)
        o_lat = mla_paged_attention(q_abs, c_all, kr_all, rows_p, cache_latent, cache_krope, i, page_table)
        ob_s = mla_uv(o_lat, w_kvb, n_db, t_len)

        oc_p, _ = gmlp(proj, 0, rows_p, cm_v_norm[i], w_mix_p, bias_p)
        oc_s, vn_s = gmlp(proj, rows_p, rows_s, cm_v_norm[i], w_mix_s, bias_s)

        o_a = jnp.concatenate([oa_p, oa_s], axis=0)
        o_b = jnp.concatenate([ob_p, ob_s], axis=0)
        o_c = jnp.concatenate([oc_p, oc_s], axis=0)
        mix = branch_merge(o_a, o_b, o_c, gates3, w_branch[i].astype(BF16), tn=tn_m)
        x = out_proj(mix, x, w_out[i].astype(BF16), ln_mix_post[i])

        up = norm_matmul(x, ln_ffn_pre[i], w_ffn_up[i].astype(BF16))
        up_p = up[:rows_p].reshape(n_b, seq, 2 * d_ff)
        up_s = up[rows_p:].reshape(n_db, t_len, 2 * d_ff)
        tm_p = _tile(seq, 512)
        n_tiles = seq // tm_p
        tails = up_p.reshape(n_b, n_tiles, tm_p, 2 * d_ff)[:, :, -2:, :]
        tails = jnp.concatenate([jnp.zeros_like(tails[:, :1]), tails[:, :-1]], axis=1)
        prev_p = tails.reshape(n_b * n_tiles, 2, 2 * d_ff)
        prev = state_ffn_conv[i]
        prev_s = jnp.stack([jnp.repeat(prev[:, 1], t_len, axis=0),
                            jnp.tile(prev, (1, t_len // 2, 1)).reshape(rows_s, 2 * d_ff)], axis=0)
        wd = w_ffn_down[i].astype(BF16)
        x_p = ffn_down(up, prev_p, conv_w2, conv_b2, wd, x, ln_ffn_post[i], 0, rows_p, seq, tm=tm_p)
        x_s = ffn_down(up, prev_s, conv_w2, conv_b2, wd, x, ln_ffn_post[i], rows_p, rows_s, t_len)
        x = jnp.concatenate([x_p, x_s], axis=0)

        pe = jnp.concatenate([p_prompt[i].reshape(rows_p, -1), p_sample[i].reshape(rows_s, -1)], axis=0)
        x = ple(x, pe, ln_ple[i], w_ple_gate[i].astype(BF16), w_ple_proj[i].astype(BF16))

        outs["lat_p"].append(c_all[:rows_p].reshape(n_b, seq, -1))
        outs["kr_p"].append(kr_all[:rows_p].reshape(n_b, seq, -1))
        outs["lat_s"].append(c_all[rows_p:].reshape(n_db, t_len, -1))
        outs["kr_s"].append(kr_all[rows_p:].reshape(n_db, t_len, -1))
        outs["hg_p"].append(hgp)
        outs["hg_s"].append(hgs)
        outs["cv_p"].append(up_p[:, -(CONV_W - 1):])
        outs["cv_s"].append(up_s[:, -(CONV_W - 1):])
        outs["v_s"].append(vn_s.reshape(n_db, t_len, -1))

    return (x[:rows_p].reshape(n_b, seq, d), x[rows_p:].reshape(n_db, t_len, d),
            jnp.stack(outs["lat_p"]), jnp.stack(outs["kr_p"]), jnp.stack(outs["lat_s"]), jnp.stack(outs["kr_s"]),
            jnp.stack(outs["hg_p"]), jnp.stack(outs["hg_s"]), jnp.stack(outs["cv_p"]), jnp.stack(outs["cv_s"]),
            jnp.stack(outs["v_s"]))
```

```python
import functools
import math

import jax
import jax.numpy as jnp
import numpy as np
from jax import lax
from jax.experimental import pallas as pl
from jax.experimental.pallas import tpu as pltpu

F32 = jnp.float32
BF16 = jnp.bfloat16
HIGHEST = lax.Precision.HIGHEST

HG_HEADS = 8
HG_KEY = 128
HG_VAL = 128
MLA_HEADS = 8
MLA_NOPE = 128
MLA_ROPE = 64
MLA_V = 128
MLA_Q_RANK = 768
MLA_KV_RANK = 512
MLA_SCALE = (MLA_NOPE + MLA_ROPE) ** -0.5
ROPE_THETA = 10000.0
CM_GROUPS = 8
CM_CHUNK = 128
N_BRANCH = 3
CONV_W = 3
EPS = 1e-6
LOG2E = math.log2(math.e)

LANES = 128
SUBLANES = 8
VMEM_LIMIT = 56 * 1024 * 1024
HEAD_SLOT = 2 * LANES

COL_GU, COL_GV = 4096, 5120
COL_QA = 6144
COL_KR = 6912
COL_KVA = 7168
IN_COLS_PERM = 7680

NT_DIMS = (((1,), (1,)), ((), ()))
TN_DIMS = (((0,), (0,)), ((), ()))


def _params(sem):
    return pltpu.CompilerParams(dimension_semantics=sem, vmem_limit_bytes=VMEM_LIMIT)


def _rms(x, g):
    ms = jnp.mean(x * x, axis=-1, keepdims=True)
    return x * lax.rsqrt(ms + EPS) * g


def _gelu(x):
    return 0.5 * x * (1.0 + jnp.tanh(math.sqrt(2.0 / math.pi) * (x + 0.044715 * (x * x * x))))


def _sigmoid(x):
    return 1.0 / (1.0 + jnp.exp(-x))


def _silu(x):
    return x * _sigmoid(x)


def _tile(n, pref, unit=1):
    t = (min(pref, n) // unit) * unit
    while t > unit and n % t:
        t -= unit
    assert t > 0 and n % t == 0, (n, pref, unit)
    return t


def _dot(a, b):
    return jnp.dot(a.astype(BF16), b.astype(BF16), preferred_element_type=F32)


def _dot_nt(a, b):
    return lax.dot_general(a.astype(BF16), b.astype(BF16), NT_DIMS, preferred_element_type=F32)


_ANY_SPEC = pl.BlockSpec(memory_space=pl.ANY)


def _norm_matmul_kernel(x_ref, g_ref, w_ref, o_ref, h_ref, *, act):
    @pl.when(pl.program_id(1) == 0)
    def _():
        h_ref[...] = _rms(x_ref[...], g_ref[...]).astype(BF16)

    acc = jnp.dot(h_ref[...], w_ref[...].astype(BF16), preferred_element_type=F32)
    if act == "sigmoid":
        acc = _sigmoid(acc)
    o_ref[...] = acc.astype(o_ref.dtype)


def norm_matmul(x, g, w, layer=None, *, act=None, out_dtype=F32, tm=1024, tn=512):
    m, k = x.shape
    n = w.shape[-1]
    tm, tn = _tile(m, tm, SUBLANES), _tile(n, tn, LANES)
    if layer is None:
        w_spec = pl.BlockSpec((k, tn), lambda i, j: (0, j))
    else:
        w_spec = pl.BlockSpec((None, k, tn), lambda i, j: (layer, 0, j))
    return pl.pallas_call(
        functools.partial(_norm_matmul_kernel, act=act),
        grid=(m // tm, n // tn),
        in_specs=[
            pl.BlockSpec((tm, k), lambda i, j: (i, 0)),
            pl.BlockSpec((1, k), lambda i, j: (0, 0)),
            w_spec,
        ],
        out_specs=pl.BlockSpec((tm, tn), lambda i, j: (i, j)),
        out_shape=jax.ShapeDtypeStruct((m, n), out_dtype),
        scratch_shapes=[pltpu.VMEM((tm, k), BF16)],
        compiler_params=_params(("parallel", "arbitrary")),
        name="norm_matmul",
    )(x, g.reshape(1, k), w)


def _hgrn_head(hq, hf, v, hg, lb3, ng, st_ref, hd, *, C, nb):
    R = nb * C
    lb = lb3[0:1, :]
    log_lb = lb3[1:2, :]
    log_1m_lb = lb3[2:3, :]
    q = _silu(hq)
    log_sig = jnp.minimum(hf, 0.0) - jnp.log1p(jnp.exp(-jnp.abs(hf)))
    c2 = log_1m_lb + log_sig
    logf = jnp.maximum(log_lb, c2) + jnp.log1p(jnp.exp(-jnp.abs(log_lb - c2)))
    k = (1.0 - lb) * (1.0 / (1.0 + jnp.exp(hf)))

    row = lax.broadcasted_iota(jnp.int32, (R, R), 0)
    col = lax.broadcasted_iota(jnp.int32, (R, R), 1)
    tri = jnp.where((col <= row) & ((row // C) == (col // C)), 1.0, 0.0).astype(F32)
    b = jnp.dot(tri, logf, precision=HIGHEST, preferred_element_type=F32)

    G = R // SUBLANES
    q3 = q.reshape(G, SUBLANES, HG_KEY)
    k3 = k.reshape(G, SUBLANES, HG_KEY)
    b3 = b.reshape(G, SUBLANES, HG_KEY)
    v3 = v.reshape(G, SUBLANES, HG_VAL)
    tpos = lax.broadcasted_iota(jnp.int32, (G, SUBLANES, 1), 1)
    o3 = jnp.zeros((G, SUBLANES, HG_VAL), F32)
    for s in range(SUBLANES):
        dec = jnp.exp(jnp.minimum(b3 - b3[:, s:s + 1, :], 0.0))
        a_s = jnp.sum(q3 * k3[:, s:s + 1, :] * dec, axis=-1, keepdims=True)
        o3 = o3 + jnp.where(tpos >= s, a_s, 0.0) * v3[:, s:s + 1, :]
    o = o3.reshape(R, HG_VAL)

    half = C // 2
    a_off = None
    rrow = lax.broadcasted_iota(jnp.int32, (R, 1), 0)
    while half >= SUBLANES:
        blk = 2 * half
        nblk = R // blk
        b4 = b.reshape(nblk, blk, HG_KEY)
        ref_b = jnp.broadcast_to(b4[:, half - 1:half, :], (nblk, blk, HG_KEY)).reshape(R, HG_KEY)
        second = (rrow % blk) >= half
        qe = jnp.where(second, q * jnp.exp(jnp.minimum(b - ref_b, 0.0)), 0.0)
        ke = jnp.where(second, 0.0, k * jnp.exp(jnp.minimum(ref_b - b, 0.0)))
        a_l = _dot_nt(qe, ke)
        if blk != R:
            a_l = jnp.where((row // blk) == (col // blk), a_l, 0.0)
        a_off = a_l if a_off is None else a_off + a_l
        half //= 2
    if a_off is not None:
        o = o + _dot(a_off, v)

    qb = q * jnp.exp(b)
    o_parts = []
    for n in range(nb):
        lo, hi = n * C, (n + 1) * C
        st = st_ref[n, hd]
        o_parts.append(_dot_nt(qb[lo:hi], st))
        b_last = b[hi - 1:hi, :]
        kd = k[lo:hi] * jnp.exp(b_last - b[lo:hi])
        st_ref[n, hd] = st * jnp.exp(b_last) + lax.dot_general(
            v[lo:hi].astype(BF16), kd.astype(BF16), TN_DIMS, preferred_element_type=F32)
    o = o + (o_parts[0] if nb == 1 else jnp.concatenate(o_parts, axis=0))
    return _rms(o, ng) * _silu(hg)


def _hgrn_kernel(*refs, C, nb, hh, n_chunks, has_s0):
    hq_ref, hf_ref, hi_ref, hg_ref, lb_ref, ng_ref = refs[:6]
    s0_ref = refs[6] if has_s0 else None
    o_ref, snew_ref, st_ref = refs[-3:]
    ci = pl.program_id(2)

    @pl.when(ci == 0)
    def _():
        for n in range(nb):
            for hd in range(hh):
                st_ref[n, hd] = s0_ref[n, hd].T if has_s0 else jnp.zeros((HG_VAL, HG_KEY), F32)

    for hd in range(hh):
        sl = slice(hd * HG_KEY, (hd + 1) * HG_KEY)
        o = _hgrn_head(hq_ref[:, sl], hf_ref[:, sl], hi_ref[:, sl], hg_ref[:, sl], lb_ref[:, sl], ng_ref[:, sl],
                       st_ref, hd, C=C, nb=nb)
        o_ref[:, sl] = o.astype(o_ref.dtype)

    @pl.when(ci == n_chunks - 1)
    def _():
        for n in range(nb):
            for hd in range(hh):
                snew_ref[n, hd] = st_ref[n, hd].T


def hgrn2(proj, row0, n_seq, seq_len, lb3, norm_g, s0_all, layer, o_prev, snew_prev, *, C, nb, hh):
    R = nb * C
    n_chunks = seq_len // C
    assert seq_len % C == 0 and n_seq % nb == 0 and row0 % R == 0 and HG_HEADS % hh == 0
    assert nb == 1 or n_chunks == 1
    rb0 = row0 // R
    blocks_per_group = nb * seq_len // R
    hb = HG_HEADS // hh
    w = hh * HG_KEY

    def rmap(sec):
        return lambda s, h, c: (rb0 + s * blocks_per_group + c, sec * hb + h)

    in_specs = [pl.BlockSpec((R, w), rmap(sec)) for sec in range(4)]
    in_specs += [pl.BlockSpec((3, w), lambda s, h, c: (0, h)),
                 pl.BlockSpec((1, w), lambda s, h, c: (0, h))]
    args = [proj, proj, proj, proj, lb3, norm_g.reshape(1, -1)]
    state_spec = pl.BlockSpec((None, nb, hh, HG_KEY, HG_VAL), lambda s, h, c: (layer, s, h, 0, 0))
    if s0_all is not None:
        in_specs.append(state_spec)
        args.append(s0_all)
    aliases = {}
    for out_idx, prev in enumerate((o_prev, snew_prev)):
        if prev is not None:
            aliases[len(args)] = out_idx
            in_specs.append(_ANY_SPEC)
            args.append(prev)
    o_shape = o_prev.shape if o_prev is not None else (proj.shape[0], HG_HEADS * HG_VAL)
    return pl.pallas_call(
        functools.partial(_hgrn_kernel, C=C, nb=nb, hh=hh, n_chunks=n_chunks, has_s0=s0_all is not None),
        grid=(n_seq // nb, hb, n_chunks),
        in_specs=in_specs,
        out_specs=[pl.BlockSpec((R, w), lambda s, h, c: (rb0 + s * blocks_per_group + c, h)), state_spec],
        out_shape=[jax.ShapeDtypeStruct(o_shape, BF16),
                   jax.ShapeDtypeStruct(snew_prev.shape, F32)],
        scratch_shapes=[pltpu.VMEM((nb, hh, HG_VAL, HG_KEY), F32)],
        input_output_aliases=aliases,
        compiler_params=_params(("parallel", "parallel", "arbitrary")),
        name="hgrn2",
    )(*args)


def _rope_pair(slot, tab):
    prod = slot * tab
    return prod + pltpu.roll(prod, MLA_ROPE, 1)


def _mla_q_kernel(x_ref, g_ref, w_ref, tab_ref, o_ref):
    qa = x_ref[...][:, :MLA_Q_RANK]
    acc = _dot(_rms(qa, g_ref[...]), w_ref[...]) * (MLA_SCALE * LOG2E)
    tab = tab_ref[...]
    lane = lax.broadcasted_iota(jnp.int32, tab.shape, 1)
    nope_w = MLA_HEADS * MLA_NOPE
    for h in range(MLA_HEADS):
        slot = acc[:, nope_w + h * LANES: nope_w + (h + 1) * LANES]
        qr = jnp.where(lane < MLA_ROPE, _rope_pair(slot, tab), 0.0)
        o_ref[:, h * HEAD_SLOT: h * HEAD_SLOT + MLA_NOPE] = acc[:, h * MLA_NOPE:(h + 1) * MLA_NOPE].astype(BF16)
        o_ref[:, h * HEAD_SLOT + MLA_NOPE:(h + 1) * HEAD_SLOT] = qr.astype(BF16)


def mla_q(proj, q_norm, w_q_ext, tab, *, tm=512):
    m = proj.shape[0]
    tm = _tile(m, tm, SUBLANES)
    n = w_q_ext.shape[1]
    return pl.pallas_call(
        _mla_q_kernel,
        grid=(m // tm,),
        in_specs=[pl.BlockSpec((tm, 1024), lambda i: (i, COL_QA // 1024)),
                  pl.BlockSpec((1, MLA_Q_RANK), lambda i: (0, 0)),
                  pl.BlockSpec((MLA_Q_RANK, n), lambda i: (0, 0)),
                  pl.BlockSpec((tm, LANES), lambda i: (i, 0))],
        out_specs=pl.BlockSpec((tm, MLA_HEADS * HEAD_SLOT), lambda i: (i, 0)),
        out_shape=jax.ShapeDtypeStruct((m, MLA_HEADS * HEAD_SLOT), BF16),
        compiler_params=_params(("parallel",)),
        name="mla_q",
    )(proj, q_norm.reshape(1, -1), w_q_ext, tab)


def _mla_kv_kernel(*refs, with_kv):
    kva_ref, kr_ref, g_ref, w_ref, tab_ref = refs[:5]
    outs = refs[-4:] if with_kv else refs[-2:]
    c_ref, krot_ref = outs[:2]
    c = _rms(kva_ref[...], g_ref[...])
    c_ref[...] = c
    krkr = _rope_pair(kr_ref[...], tab_ref[...])
    krot_ref[...] = krkr[:, :MLA_ROPE]
    if with_kv:
        kfull_ref, v_ref = outs[2:]
        acc = _dot(c, w_ref[...])
        krkr_b = krkr.astype(BF16)
        for h in range(MLA_HEADS):
            kfull_ref[:, h * HEAD_SLOT: h * HEAD_SLOT + MLA_NOPE] = (
                acc[:, h * HEAD_SLOT: h * HEAD_SLOT + MLA_NOPE].astype(BF16))
            kfull_ref[:, h * HEAD_SLOT + MLA_NOPE:(h + 1) * HEAD_SLOT] = krkr_b
            v_ref[:, h * MLA_V:(h + 1) * MLA_V] = acc[:, h * HEAD_SLOT + MLA_NOPE:(h + 1) * HEAD_SLOT].astype(BF16)


def mla_kv(proj, row0, n_rows, kv_norm, w_kv_b, tab, layer, lat_prev, kr_prev, *, with_kv, tm=512):
    tm = _tile(math.gcd(n_rows, row0) if row0 else n_rows, tm, SUBLANES)
    rb0 = row0 // tm
    out_specs = [pl.BlockSpec((None, tm, MLA_KV_RANK), lambda i: (layer, i, 0)),
                 pl.BlockSpec((None, tm, MLA_ROPE), lambda i: (layer, i, 0))]
    out_shape = [jax.ShapeDtypeStruct(lat_prev.shape, F32), jax.ShapeDtypeStruct(kr_prev.shape, F32)]
    if with_kv:
        out_specs += [pl.BlockSpec((tm, MLA_HEADS * HEAD_SLOT), lambda i: (i, 0)),
                      pl.BlockSpec((tm, MLA_HEADS * MLA_V), lambda i: (i, 0))]
        out_shape += [jax.ShapeDtypeStruct((n_rows, MLA_HEADS * HEAD_SLOT), BF16),
                      jax.ShapeDtypeStruct((n_rows, MLA_HEADS * MLA_V), BF16)]
    return pl.pallas_call(
        functools.partial(_mla_kv_kernel, with_kv=with_kv),
        grid=(n_rows // tm,),
        in_specs=[pl.BlockSpec((tm, MLA_KV_RANK), lambda i: (rb0 + i, COL_KVA // MLA_KV_RANK)),
                  pl.BlockSpec((tm, LANES), lambda i: (rb0 + i, COL_KR // LANES)),
                  pl.BlockSpec((1, MLA_KV_RANK), lambda i: (0, 0)),
                  pl.BlockSpec((MLA_KV_RANK, MLA_HEADS * HEAD_SLOT), lambda i: (0, 0)),
                  pl.BlockSpec((tm, LANES), lambda i: (rb0 + i, 0)),
                  _ANY_SPEC, _ANY_SPEC],
        out_specs=out_specs,
        out_shape=out_shape,
        input_output_aliases={5: 0, 6: 1},
        compiler_params=_params(("parallel",)),
        name="mla_kv",
    )(proj, proj, kv_norm.reshape(1, -1), w_kv_b, tab, lat_prev, kr_prev)


def _flash_kernel(qi_ref, ki_ref, q_ref, k_ref, v_ref, o_ref, m_ref, l_ref, acc_ref, *, tq, tk, hh):
    t = pl.program_id(2)
    qi = qi_ref[t]
    ki = ki_ref[t]
    last_k = ((qi + 1) * tq - 1) // tk

    @pl.when(ki == 0)
    def _():
        m_ref[...] = jnp.full(m_ref.shape, -jnp.inf, F32)
        l_ref[...] = jnp.zeros(l_ref.shape, F32)
        acc_ref[...] = jnp.zeros(acc_ref.shape, F32)

    def update(masked):
        for hd in range(hh):
            s = lax.dot_general(q_ref[:, hd * HEAD_SLOT:(hd + 1) * HEAD_SLOT],
                                k_ref[:, hd * HEAD_SLOT:(hd + 1) * HEAD_SLOT], NT_DIMS,
                                preferred_element_type=F32)
            if masked:
                qpos = qi * tq + lax.broadcasted_iota(jnp.int32, (tq, tk), 0)
                kpos = ki * tk + lax.broadcasted_iota(jnp.int32, (tq, tk), 1)
                s = jnp.where(kpos <= qpos, s, -jnp.inf)
            m_prev = m_ref[hd]
            m_new = jnp.maximum(m_prev, jnp.max(s, axis=-1, keepdims=True))
            alpha = jnp.exp2(m_prev - m_new)
            p = jnp.exp2(s - m_new)
            l_ref[hd] = alpha * l_ref[hd] + jnp.sum(p, axis=-1, keepdims=True)
            acc_ref[hd] = alpha * acc_ref[hd] + jnp.dot(
                p.astype(BF16), v_ref[:, hd * MLA_V:(hd + 1) * MLA_V], preferred_element_type=F32)
            m_ref[hd] = m_new

    @pl.when((ki + 1) * tk - 1 > qi * tq)
    def _():
        update(True)

    @pl.when((ki + 1) * tk - 1 <= qi * tq)
    def _():
        update(False)

    @pl.when(ki == last_k)
    def _():
        for hd in range(hh):
            o_ref[:, hd * MLA_V:(hd + 1) * MLA_V] = (acc_ref[hd] / l_ref[hd]).astype(o_ref.dtype)


def mla_prompt_attention(q_full, k_full, v_all, n_batch, seq, o_rows, *, tq=512, tk=512, hh=4):
    tq, tk = _tile(seq, tq, LANES), _tile(seq, tk, LANES)
    nq, nk = seq // tq, seq // tk
    pairs = [(a, b) for a in range(nq) for b in range(((a + 1) * tq - 1) // tk + 1)]
    qi_tab = jnp.asarray([p[0] for p in pairs], jnp.int32)
    ki_tab = jnp.asarray([p[1] for p in pairs], jnp.int32)
    grid_spec = pltpu.PrefetchScalarGridSpec(
        num_scalar_prefetch=2,
        grid=(n_batch, MLA_HEADS // hh, len(pairs)),
        in_specs=[pl.BlockSpec((tq, hh * HEAD_SLOT), lambda b, h, t, qt, kt: (b * nq + qt[t], h)),
                  pl.BlockSpec((tk, hh * HEAD_SLOT), lambda b, h, t, qt, kt: (b * nk + kt[t], h)),
                  pl.BlockSpec((tk, hh * MLA_V), lambda b, h, t, qt, kt: (b * nk + kt[t], h))],
        out_specs=pl.BlockSpec((tq, hh * MLA_V), lambda b, h, t, qt, kt: (b * nq + qt[t], h)),
        scratch_shapes=[pltpu.VMEM((hh, tq, 1), F32), pltpu.VMEM((hh, tq, 1), F32),
                        pltpu.VMEM((hh, tq, MLA_V), F32)],
    )
    return pl.pallas_call(
        functools.partial(_flash_kernel, tq=tq, tk=tk, hh=hh),
        grid_spec=grid_spec,
        out_shape=jax.ShapeDtypeStruct((o_rows, MLA_HEADS * MLA_V), BF16),
        compiler_params=_params(("parallel", "parallel", "arbitrary")),
        name="mla_prompt_flash",
    )(qi_tab, ki_tab, q_full, k_full, v_all)


def _absorb_kernel(q_ref, w_ref, o_ref):
    qh = q_ref[...]
    q_lat = _dot_nt(qh[:, :MLA_NOPE], w_ref[...])
    out = jnp.concatenate([q_lat, qh[:, MLA_NOPE:].astype(F32)], axis=-1)
    o_ref[...] = out.reshape(o_ref.shape)


def mla_absorb(q_full, w_kv_b, row0, n_seq, t_len):
    rows = n_seq * t_len
    assert row0 % rows == 0
    width = MLA_KV_RANK + LANES
    return pl.pallas_call(
        _absorb_kernel,
        grid=(MLA_HEADS,),
        in_specs=[pl.BlockSpec((rows, HEAD_SLOT), lambda h: (row0 // rows, h)),
                  pl.BlockSpec((MLA_KV_RANK, MLA_NOPE), lambda h: (0, 2 * h))],
        out_specs=pl.BlockSpec((n_seq, None, t_len, width), lambda h: (0, h, 0, 0)),
        out_shape=jax.ShapeDtypeStruct((n_seq, MLA_HEADS, t_len, width), F32),
        compiler_params=_params(("parallel",)),
        name="mla_absorb",
    )(q_full, w_kv_b)


def _paged_attn_kernel(pt_ref, q_ref, cnew_ref, krnew_ref, *rest, G, t_len, n_steps):
    c_pages = rest[:G]
    kr_pages = rest[G:2 * G]
    o_ref, cbuf, m_ref, l_ref, acc_ref = rest[2 * G:]
    g = pl.program_id(1)
    page = cbuf.shape[0] // G
    q = q_ref[0]
    q_lat = q[:, :MLA_KV_RANK]
    q_rope = q[:, MLA_KV_RANK:MLA_KV_RANK + MLA_ROPE]

    @pl.when(g == 0)
    def _():
        c_new = cnew_ref[...]
        kr_new = krnew_ref[...]
        tq = lax.broadcasted_iota(jnp.int32, (q.shape[0], 1), 0) % t_len
        cols = []
        for j in range(t_len):
            sj = (jnp.sum(q_lat * c_new[j:j + 1, :], axis=-1, keepdims=True)
                  + jnp.sum(q_rope * kr_new[j:j + 1, :], axis=-1, keepdims=True))
            cols.append(jnp.where(tq >= j, sj, -jnp.inf))
        m0 = cols[0]
        for j in range(1, t_len):
            m0 = jnp.maximum(m0, cols[j])
        l0 = jnp.zeros_like(m0)
        acc0 = jnp.zeros(acc_ref.shape, F32)
        for j in range(t_len):
            pj = jnp.exp2(cols[j] - m0)
            l0 = l0 + pj
            acc0 = acc0 + pj * c_new[j:j + 1, :]
        m_ref[...] = m0
        l_ref[...] = l0
        acc_ref[...] = acc0

    q_lat_b = q_lat.astype(BF16)
    q_rope_b = q_rope.astype(BF16)
    s_parts = []
    for i in range(0, G, 2):
        cb = jnp.concatenate([c_pages[i][...].astype(BF16), c_pages[i + 1][...].astype(BF16)], axis=0)
        cbuf[i * page:(i + 2) * page, :] = cb
        krt = jnp.concatenate([kr_pages[i][...].astype(BF16), kr_pages[i + 1][...].astype(BF16)], axis=1)
        s_parts.append(lax.dot_general(q_lat_b, cb, NT_DIMS, preferred_element_type=F32)
                       + jnp.dot(q_rope_b, krt, preferred_element_type=F32))
    s = jnp.concatenate(s_parts, axis=1)
    m_prev = m_ref[...]
    m_new = jnp.maximum(m_prev, jnp.max(s, axis=-1, keepdims=True))
    alpha = jnp.exp2(m_prev - m_new)
    p = jnp.exp2(s - m_new)
    l_ref[...] = alpha * l_ref[...] + jnp.sum(p, axis=-1, keepdims=True)
    acc_ref[...] = alpha * acc_ref[...] + jnp.dot(p.astype(BF16), cbuf[...], preferred_element_type=F32)
    m_ref[...] = m_new

    @pl.when(g == n_steps - 1)
    def _():
        o_ref[0] = acc_ref[...] / l_ref[...]


def mla_paged_attention(q_abs, lat_s, kr_s, cache_latent, cache_krope_t, layer, page_table, *, G=32):
    n_seq, heads, t_len, width = q_abs.shape
    n_pages = page_table.shape[1]
    page = cache_latent.shape[2]
    G = _tile(n_pages, G, 2)
    n_steps = n_pages // G
    rows = heads * t_len
    q3 = q_abs.reshape(n_seq, rows, width)

    def pmap(i):
        return lambda b, g, pt: (layer, pt[b, g * G + i], 0, 0)

    in_specs = [pl.BlockSpec((1, rows, width), lambda b, g, pt: (b, 0, 0)),
                pl.BlockSpec((None, t_len, MLA_KV_RANK), lambda b, g, pt: (layer, b, 0)),
                pl.BlockSpec((None, t_len, MLA_ROPE), lambda b, g, pt: (layer, b, 0))]
    in_specs += [pl.BlockSpec((None, None, page, MLA_KV_RANK), pmap(i)) for i in range(G)]
    in_specs += [pl.BlockSpec((None, None, MLA_ROPE, page), pmap(i)) for i in range(G)]
    grid_spec = pltpu.PrefetchScalarGridSpec(
        num_scalar_prefetch=1,
        grid=(n_seq, n_steps),
        in_specs=in_specs,
        out_specs=pl.BlockSpec((1, rows, MLA_KV_RANK), lambda b, g, pt: (b, 0, 0)),
        scratch_shapes=[pltpu.VMEM((G * page, MLA_KV_RANK), BF16),
                        pltpu.VMEM((rows, 1), F32), pltpu.VMEM((rows, 1), F32),
                        pltpu.VMEM((rows, MLA_KV_RANK), F32)],
    )
    return pl.pallas_call(
        functools.partial(_paged_attn_kernel, G=G, t_len=t_len, n_steps=n_steps),
        grid_spec=grid_spec,
        out_shape=jax.ShapeDtypeStruct((n_seq, rows, MLA_KV_RANK), F32),
        compiler_params=_params(("parallel", "arbitrary")),
        name="mla_paged_attention",
    )(page_table, q3, lat_s, kr_s, *([cache_latent] * G), *([cache_krope_t] * G))


def _uv_kernel(o_ref, w_ref, prev_ref, out_ref):
    o = o_ref[...]
    out_ref[...] = _dot(o.reshape(o.shape[0] * o.shape[1], o.shape[2]), w_ref[...]).astype(out_ref.dtype)


def mla_uv(o_lat, w_kv_b, row0, n_seq, t_len, ob_prev):
    o4 = o_lat.reshape(n_seq, MLA_HEADS, t_len, MLA_KV_RANK)
    rows = n_seq * t_len
    assert row0 % rows == 0
    return pl.pallas_call(
        _uv_kernel,
        grid=(MLA_HEADS,),
        in_specs=[pl.BlockSpec((n_seq, None, t_len, MLA_KV_RANK), lambda h: (0, h, 0, 0)),
                  pl.BlockSpec((MLA_KV_RANK, MLA_V), lambda h: (0, 2 * h + 1)),
                  _ANY_SPEC],
        out_specs=pl.BlockSpec((rows, MLA_V), lambda h: (row0 // rows, h)),
        out_shape=jax.ShapeDtypeStruct(ob_prev.shape, BF16),
        input_output_aliases={2: 0},
        compiler_params=_params(("parallel",)),
        name="mla_uv",
    )(o4, w_kv_b, ob_prev)


def _gmlp_kernel(*refs, emit_vn):
    gu_ref, gv_ref, g_ref, w_ref, bias_ref = refs[:5]
    o_ref = refs[-2] if emit_vn else refs[-1]
    vn = _rms(_gelu(gv_ref[...]), g_ref[...])
    if emit_vn:
        refs[-1][...] = vn
    u = _gelu(gu_ref[...])
    for grp in range(CM_GROUPS):
        sl = slice(grp * LANES, (grp + 1) * LANES)
        z = jnp.dot(w_ref[grp], vn[:, sl], precision=HIGHEST, preferred_element_type=F32) + bias_ref[:, sl]
        o_ref[:, sl] = (u[:, sl] * z).astype(o_ref.dtype)


def gmlp(proj, row0, n_rows, v_norm, w_mix, bias_full, oc_prev, *, emit_vn):
    width = CM_GROUPS * LANES
    assert row0 % CM_CHUNK == 0 and n_rows % CM_CHUNK == 0
    rb0 = row0 // CM_CHUNK
    in_specs = [pl.BlockSpec((CM_CHUNK, width), lambda i: (rb0 + i, COL_GU // width)),
                pl.BlockSpec((CM_CHUNK, width), lambda i: (rb0 + i, COL_GV // width)),
                pl.BlockSpec((1, width), lambda i: (0, 0)),
                pl.BlockSpec((CM_GROUPS, CM_CHUNK, CM_CHUNK), lambda i: (0, 0, 0)),
                pl.BlockSpec((CM_CHUNK, width), lambda i: (0, 0))]
    args = [proj, proj, v_norm.reshape(1, -1), w_mix, bias_full]
    aliases = {}
    if oc_prev is not None:
        aliases[len(args)] = 0
        in_specs.append(_ANY_SPEC)
        args.append(oc_prev)
    out_specs = [pl.BlockSpec((CM_CHUNK, width), lambda i: (rb0 + i, 0))]
    out_shape = [jax.ShapeDtypeStruct((proj.shape[0], width), BF16)]
    if emit_vn:
        out_specs.append(pl.BlockSpec((CM_CHUNK, width), lambda i: (i, 0)))
        out_shape.append(jax.ShapeDtypeStruct((n_rows, width), F32))
    return pl.pallas_call(
        functools.partial(_gmlp_kernel, emit_vn=emit_vn),
        grid=(n_rows // CM_CHUNK,),
        in_specs=in_specs,
        out_specs=out_specs,
        out_shape=out_shape,
        input_output_aliases=aliases,
        compiler_params=_params(("parallel",)),
        name="gmlp",
    )(*args)


def _merge_kernel(oa_ref, ob_ref, oc_ref, g0_ref, g1_ref, g2_ref, w_ref, o_ref):
    acc = None
    for br, (b_ref, g_ref) in enumerate(((oa_ref, g0_ref), (ob_ref, g1_ref), (oc_ref, g2_ref))):
        part = g_ref[...] * jnp.dot(b_ref[...], w_ref[br], preferred_element_type=F32)
        acc = part if acc is None else acc + part
    o_ref[...] = acc.astype(o_ref.dtype)


def branch_merge(o_a, o_b, o_c, gates, w_branch, *, tm=1024, tn=512):
    m, bw = o_a.shape
    d = w_branch.shape[2]
    tm, tn = _tile(m, tm, SUBLANES), _tile(d, tn, LANES)
    nt = d // tn
    bspec = pl.BlockSpec((tm, bw), lambda i, j: (i, 0))

    def gspec(br):
        return pl.BlockSpec((tm, tn), lambda i, j: (i, br * nt + j))

    return pl.pallas_call(
        _merge_kernel,
        grid=(m // tm, nt),
        in_specs=[bspec, bspec, bspec, gspec(0), gspec(1), gspec(2),
                  pl.BlockSpec((N_BRANCH, bw, tn), lambda i, j: (0, 0, j))],
        out_specs=pl.BlockSpec((tm, tn), lambda i, j: (i, j)),
        out_shape=jax.ShapeDtypeStruct((m, d), BF16),
        compiler_params=_params(("parallel", "arbitrary")),
        name="branch_merge",
    )(o_a, o_b, o_c, gates, gates, gates, w_branch)


def _out_proj_kernel(mix_ref, x_ref, w_ref, g_ref, o_ref):
    y = jnp.dot(mix_ref[...], w_ref[...], preferred_element_type=F32)
    o_ref[...] = x_ref[...] + _rms(y, g_ref[...])


def out_proj(mix, x, w_out, g, *, tm=512):
    m, d = x.shape
    tm = _tile(m, tm, SUBLANES)
    return pl.pallas_call(
        _out_proj_kernel,
        grid=(m // tm,),
        in_specs=[pl.BlockSpec((tm, d), lambda i: (i, 0)),
                  pl.BlockSpec((tm, d), lambda i: (i, 0)),
                  pl.BlockSpec((d, d), lambda i: (0, 0)),
                  pl.BlockSpec((1, d), lambda i: (0, 0))],
        out_specs=pl.BlockSpec((tm, d), lambda i: (i, 0)),
        out_shape=jax.ShapeDtypeStruct((m, d), F32),
        compiler_params=_params(("parallel",)),
        name="out_proj",
    )(mix, x, w_out, g.reshape(1, -1))


def _ffn_down_kernel(*refs, t_len, n_k, per_seq_prev, kc):
    ug_ref, uv_ref, pg_ref, pv_ref, cw_ref, cb_ref, w_ref, x_ref, g_ref = refs[:9]
    o_ref, acc_ref = refs[-2:]
    kk = pl.program_id(1)
    tk = w_ref.shape[0]

    def conv(u_ref, p_ref, which, sl):
        if per_seq_prev:
            u = u_ref[:, :, sl]
            p = p_ref[:, :, sl]
            tpos = lax.broadcasted_iota(jnp.int32, (1, t_len, 1), 1)
            back1 = p[:, 1:2, :]
            back2 = jnp.where(tpos == 0, p[:, 0:1, :], p[:, 1:2, :])
            axis = 1
        else:
            u = u_ref[:, sl]
            p = p_ref[:, sl]
            tpos = lax.broadcasted_iota(jnp.int32, (u.shape[0], 1), 0)
            back1 = p[1:2, :]
            back2 = jnp.where(tpos == 0, p[0:1, :], p[1:2, :])
            axis = 0
        m1 = jnp.where(tpos == 0, back1, pltpu.roll(u, 1, axis))
        m2 = jnp.where(tpos <= 1, back2, pltpu.roll(u, 2, axis))
        y = (cb_ref[which:which + 1, sl] + cw_ref[which, 0:1, sl] * m2 + cw_ref[which, 1:2, sl] * m1
             + cw_ref[which, 2:3, sl] * u)
        return y.reshape(-1, y.shape[-1])

    part = None
    for c0 in range(0, tk, kc):
        sl = slice(c0, c0 + kc)
        act = (_gelu(conv(ug_ref, pg_ref, 0, sl)) * conv(uv_ref, pv_ref, 1, sl)).astype(BF16)
        d = jnp.dot(act, w_ref[sl, :], preferred_element_type=F32)
        part = d if part is None else part + d

    @pl.when(kk == 0)
    def _():
        acc_ref[...] = part

    @pl.when(kk > 0)
    def _():
        acc_ref[...] += part

    @pl.when(kk == n_k - 1)
    def _():
        o_ref[...] = x_ref[...] + _rms(acc_ref[...], g_ref[...])


def ffn_down(up, prev, layer, conv_w2, conv_b2, w_down, x, g, row0, n_rows, t_len, x_prev, *, tm=512, tk=512,
             kc=256):
    d_ff, d = w_down.shape
    per_seq_prev = t_len < tm
    tm = _tile(n_rows if per_seq_prev else t_len, tm, SUBLANES)
    tk = _tile(d_ff, tk, LANES)
    kc = _tile(tk, kc, LANES)
    n_k = d_ff // tk
    assert row0 % tm == 0 and (tm % t_len == 0 or t_len % tm == 0)
    rb0 = row0 // tm
    if per_seq_prev:
        assert t_len == SUBLANES
        ns = tm // t_len
        up_in = up.reshape(up.shape[0] // t_len, t_len, 2 * d_ff)
        g_spec = pl.BlockSpec((ns, t_len, tk), lambda i, k: (rb0 + i, 0, k))
        v_spec = pl.BlockSpec((ns, t_len, tk), lambda i, k: (rb0 + i, 0, n_k + k))
        pg_spec = pl.BlockSpec((None, ns, CONV_W - 1, tk), lambda i, k: (layer, i, 0, k))
        pv_spec = pl.BlockSpec((None, ns, CONV_W - 1, tk), lambda i, k: (layer, i, 0, n_k + k))
    else:
        up_in = up
        g_spec = pl.BlockSpec((tm, tk), lambda i, k: (rb0 + i, k))
        v_spec = pl.BlockSpec((tm, tk), lambda i, k: (rb0 + i, n_k + k))
        pg_spec = pl.BlockSpec((None, CONV_W - 1, tk), lambda i, k: (i, 0, k))
        pv_spec = pl.BlockSpec((None, CONV_W - 1, tk), lambda i, k: (i, 0, n_k + k))
    in_specs = [g_spec, v_spec, pg_spec, pv_spec,
                pl.BlockSpec((2, CONV_W, tk), lambda i, k: (0, 0, k)),
                pl.BlockSpec((2, tk), lambda i, k: (0, k)),
                pl.BlockSpec((tk, d), lambda i, k: (k, 0)),
                pl.BlockSpec((tm, d), lambda i, k: (rb0 + i, 0)),
                pl.BlockSpec((1, d), lambda i, k: (0, 0))]
    args = [up_in, up_in, prev, prev, conv_w2, conv_b2, w_down, x, g.reshape(1, -1)]
    aliases = {}
    if x_prev is not None:
        aliases[len(args)] = 0
        in_specs.append(_ANY_SPEC)
        args.append(x_prev)
    return pl.pallas_call(
        functools.partial(_ffn_down_kernel, t_len=t_len, n_k=n_k, per_seq_prev=per_seq_prev, kc=kc),
        grid=(n_rows // tm, n_k),
        in_specs=in_specs,
        out_specs=pl.BlockSpec((tm, d), lambda i, k: (rb0 + i, 0)),
        out_shape=jax.ShapeDtypeStruct(x.shape, F32),
        scratch_shapes=[pltpu.VMEM((tm, d), F32)],
        input_output_aliases=aliases,
        compiler_params=_params(("parallel", "arbitrary")),
        name="ffn_down",
    )(*args)


def _ple_kernel(x_ref, pe_ref, g_ref, wg_ref, wp_ref, o_ref):
    x = x_ref[...]
    gate = _sigmoid(_dot(_rms(x, g_ref[...]), wg_ref[...]))
    o_ref[...] = x + _dot(pe_ref[...], wp_ref[...]) * gate


def ple(x, pe, g, w_gate, w_proj, *, tm=512):
    m, d = x.shape
    pd = pe.shape[1]
    tm = _tile(m, tm, SUBLANES)
    return pl.pallas_call(
        _ple_kernel,
        grid=(m // tm,),
        in_specs=[pl.BlockSpec((tm, d), lambda i: (i, 0)),
                  pl.BlockSpec((tm, pd), lambda i: (i, 0)),
                  pl.BlockSpec((1, d), lambda i: (0, 0)),
                  pl.BlockSpec((d, d), lambda i: (0, 0)),
                  pl.BlockSpec((pd, d), lambda i: (0, 0))],
        out_specs=pl.BlockSpec((tm, d), lambda i: (i, 0)),
        out_shape=jax.ShapeDtypeStruct((m, d), F32),
        compiler_params=_params(("parallel",)),
        name="ple",
    )(x, pe, g.reshape(1, -1), w_gate, w_proj)


def _rope_table(pos):
    inv = 1.0 / (ROPE_THETA ** (jnp.arange(0, MLA_ROPE, 2, dtype=F32) / MLA_ROPE))
    ang = pos.astype(F32)[:, None] * inv[None, :]
    cos, sin = jnp.cos(ang), jnp.sin(ang)
    return jnp.concatenate([cos, cos, -sin, sin], axis=-1)


def _swap_halves(w, width):
    half = width // 2
    return jnp.concatenate([w[..., half:], w[..., :half]], axis=-1)


def kernel(x_prompt, x_sample, cache_latent, cache_krope, state_hgrn, state_ffn_conv, page_table, p_prompt, p_sample, ln_mix_pre, ln_mix_post, ln_ffn_pre, ln_ffn_post, ln_ple, w_in, hgrn_lower_bounds, hgrn_norm, mla_q_norm, mla_kv_norm, w_q_b, w_kv_b, cm_v_norm, cm_w_s, cm_b, w_merge_gate, w_branch, w_out, w_ffn_up, ffn_conv_w, ffn_conv_b, w_ffn_down, w_ple_proj, w_ple_gate):
    n_b, seq, d = x_prompt.shape
    n_db, t_len, _ = x_sample.shape
    depth = w_in.shape[0]
    d_ff = w_ffn_down.shape[1]
    page = cache_latent.shape[2]
    past_len = page_table.shape[1] * page
    rows_p, rows_s = n_b * seq, n_db * t_len
    rows = rows_p + rows_s
    hk = HG_HEADS * HG_KEY

    lb_soft = jax.nn.softmax(hgrn_lower_bounds.astype(F32), axis=0)
    lbs = jnp.concatenate([jnp.zeros_like(lb_soft[:1]), jnp.cumsum(lb_soft[1:], axis=0)], axis=0)

    tab = jnp.concatenate([jnp.tile(_rope_table(jnp.arange(seq)), (n_b, 1)),
                           jnp.tile(_rope_table(past_len + jnp.arange(t_len)), (n_db, 1))], axis=0)
    cache_krope_t = jnp.swapaxes(cache_krope, 2, 3)

    x = jnp.concatenate([x_prompt.reshape(rows_p, d), x_sample.reshape(rows_s, d)], axis=0)
    tri = jnp.tril(jnp.ones((CM_CHUNK, CM_CHUNK), F32))
    seq_per_chunk = CM_CHUNK // t_len

    lat_p = jnp.zeros((depth, rows_p, MLA_KV_RANK), F32)
    kr_p = jnp.zeros((depth, rows_p, MLA_ROPE), F32)
    lat_s = jnp.zeros((depth, rows_s, MLA_KV_RANK), F32)
    kr_s = jnp.zeros((depth, rows_s, MLA_ROPE), F32)
    hg_p = jnp.zeros((depth, n_b, HG_HEADS, HG_KEY, HG_VAL), F32)
    hg_s = jnp.zeros((depth, n_db, HG_HEADS, HG_KEY, HG_VAL), F32)
    cv_p, cv_s, v_s = [], [], []

    tm_p = _tile(seq, 512, SUBLANES)
    n_tiles = seq // tm_p
    tile_starts = (np.arange(n_b)[:, None] * seq + np.arange(1, n_tiles)[None, :] * tm_p).reshape(-1)
    halo_idx = jnp.asarray(np.stack([tile_starts - 2, tile_starts - 1], axis=1).reshape(-1), jnp.int32)
    last_p_idx = jnp.asarray((np.arange(n_b)[:, None] * seq + seq - 2 + np.arange(2)[None, :]).reshape(-1),
                             jnp.int32)
    last_s_idx = jnp.asarray((rows_p + np.arange(n_db)[:, None] * t_len + t_len - 2
                              + np.arange(2)[None, :]).reshape(-1), jnp.int32)

    for i in range(depth):
        wi = w_in[i]
        o_kr = 4 * hk + MLA_Q_RANK + MLA_KV_RANK
        kr_cols = wi[:, o_kr: o_kr + MLA_ROPE]
        w_in_p = jnp.concatenate(
            [wi[:, :4 * hk], wi[:, o_kr + MLA_ROPE:], wi[:, 4 * hk: 4 * hk + MLA_Q_RANK], kr_cols,
             _swap_halves(kr_cols, MLA_ROPE), jnp.zeros((d, LANES), F32),
             wi[:, 4 * hk + MLA_Q_RANK: o_kr]], axis=1).astype(BF16)
        wq = w_q_b[i].reshape(MLA_Q_RANK, MLA_HEADS, MLA_NOPE + MLA_ROPE)
        wq_rope = wq[..., MLA_NOPE:]
        w_q_ext = jnp.concatenate(
            [wq[..., :MLA_NOPE].reshape(MLA_Q_RANK, -1),
             jnp.concatenate([wq_rope, _swap_halves(wq_rope, MLA_ROPE)], axis=-1).reshape(MLA_Q_RANK, -1)],
            axis=1).astype(BF16)
        w_kvb = w_kv_b[i].astype(BF16)
        lb = lbs[i]
        lb3 = jnp.stack([lb, jnp.log(lb), jnp.log1p(-lb)], axis=0)
        w_mix_p = cm_w_s[i] * tri
        w_mix_s = jnp.einsum("ab,gts->gatbs", jnp.eye(seq_per_chunk, dtype=F32),
                             w_mix_p[:, :t_len, :t_len]).reshape(CM_GROUPS, CM_CHUNK, CM_CHUNK)
        bias_p = jnp.repeat(cm_b[i].T, LANES, axis=1)
        bias_s = jnp.tile(bias_p[:t_len], (seq_per_chunk, 1))
        conv_w2 = ffn_conv_w[i].reshape(CONV_W, 2, d_ff).transpose(1, 0, 2)
        conv_b2 = ffn_conv_b[i].reshape(2, d_ff)

        proj = norm_matmul(x, ln_mix_pre[i], w_in_p)
        gates = norm_matmul(x, ln_mix_pre[i], w_merge_gate, i, act="sigmoid")

        o_a, hg_p = hgrn2(proj, 0, n_b, seq, lb3, hgrn_norm[i], None, i, None, hg_p,
                          C=_tile(seq, 128, SUBLANES), nb=1, hh=4)
        o_a, hg_s = hgrn2(proj, rows_p, n_db, t_len, lb3, hgrn_norm[i], state_hgrn, i, o_a, hg_s,
                          C=t_len, nb=_tile(n_db, LANES // t_len), hh=2)

        q_full = mla_q(proj, mla_q_norm[i], w_q_ext, tab)
        lat_p, kr_p, k_full, v_all = mla_kv(proj, 0, rows_p, mla_kv_norm[i], w_kvb, tab, i, lat_p, kr_p,
                                            with_kv=True)
        lat_s, kr_s = mla_kv(proj, rows_p, rows_s, mla_kv_norm[i], w_kvb, tab, i, lat_s, kr_s, with_kv=False)
        o_b = mla_prompt_attention(q_full, k_full, v_all, n_b, seq, rows)
        q_abs = mla_absorb(q_full, w_kvb, rows_p, n_db, t_len)
        o_lat = mla_paged_attention(q_abs, lat_s, kr_s, cache_latent, cache_krope_t, i, page_table)
        o_b = mla_uv(o_lat, w_kvb, rows_p, n_db, t_len, o_b)

        (o_c,) = gmlp(proj, 0, rows_p, cm_v_norm[i], w_mix_p, bias_p, None, emit_vn=False)
        o_c, vn_s = gmlp(proj, rows_p, rows_s, cm_v_norm[i], w_mix_s, bias_s, o_c, emit_vn=True)

        mix = branch_merge(o_a, o_b, o_c, gates, w_branch[i].astype(BF16))
        x = out_proj(mix, x, w_out[i].astype(BF16), ln_mix_post[i])

        up = norm_matmul(x, ln_ffn_pre[i], w_ffn_up, i)
        halo = jnp.take(up, halo_idx, axis=0).reshape(n_b, n_tiles - 1, 2, 2 * d_ff)
        prev_p = jnp.concatenate([jnp.zeros((n_b, 1, 2, 2 * d_ff), F32), halo], axis=1).reshape(
            n_b * n_tiles, 2, 2 * d_ff)
        wd = w_ffn_down[i].astype(BF16)
        x_new = ffn_down(up, prev_p, i, conv_w2, conv_b2, wd, x, ln_ffn_post[i], 0, rows_p, seq, None, tm=tm_p)
        x = ffn_down(up, state_ffn_conv, i, conv_w2, conv_b2, wd, x, ln_ffn_post[i], rows_p, rows_s, t_len, x_new)

        pe = jnp.concatenate([p_prompt[i].reshape(rows_p, -1), p_sample[i].reshape(rows_s, -1)], axis=0)
        x = ple(x, pe, ln_ple[i], w_ple_gate[i].astype(BF16), w_ple_proj[i].astype(BF16))

        cv_p.append(jnp.take(up, last_p_idx, axis=0).reshape(n_b, CONV_W - 1, 2 * d_ff))
        cv_s.append(jnp.take(up, last_s_idx, axis=0).reshape(n_db, CONV_W - 1, 2 * d_ff))
        v_s.append(vn_s.reshape(n_db, t_len, -1))

    return (x[:rows_p].reshape(n_b, seq, d), x[rows_p:].reshape(n_db, t_len, d),
            lat_p.reshape(depth, n_b, seq, -1), kr_p.reshape(depth, n_b, seq, -1),
            lat_s.reshape(depth, n_db, t_len, -1), kr_s.reshape(depth, n_db, t_len, -1),
            hg_p, hg_s, jnp.stack(cv_p), jnp.stack(cv_s), jnp.stack(v_s))
```

```python
import functools
import math

import jax
import jax.numpy as jnp
import numpy as np
from jax import lax
from jax.experimental import pallas as pl
from jax.experimental.pallas import tpu as pltpu

F32 = jnp.float32
BF16 = jnp.bfloat16
HIGHEST = lax.Precision.HIGHEST

HG_HEADS = 8
HG_KEY = 128
HG_VAL = 128
MLA_HEADS = 8
MLA_NOPE = 128
MLA_ROPE = 64
MLA_V = 128
MLA_Q_RANK = 768
MLA_KV_RANK = 512
MLA_SCALE = (MLA_NOPE + MLA_ROPE) ** -0.5
ROPE_THETA = 10000.0
CM_GROUPS = 8
CM_CHUNK = 128
N_BRANCH = 3
CONV_W = 3
EPS = 1e-6
LOG2E = math.log2(math.e)

LANES = 128
SUBLANES = 8
VMEM_LIMIT = 56 * 1024 * 1024
HEAD_SLOT = 2 * LANES

COL_GU, COL_GV = 4096, 5120
COL_QA = 6144
COL_KR = 6912
COL_KVA = 7168
IN_COLS_PERM = 7680

NT_DIMS = (((1,), (1,)), ((), ()))
TN_DIMS = (((0,), (0,)), ((), ()))


def _params(sem):
    return pltpu.CompilerParams(dimension_semantics=sem, vmem_limit_bytes=VMEM_LIMIT)


def _rms(x, g):
    ms = jnp.mean(x * x, axis=-1, keepdims=True)
    return x * lax.rsqrt(ms + EPS) * g


def _gelu(x):
    return 0.5 * x * (1.0 + jnp.tanh(math.sqrt(2.0 / math.pi) * (x + 0.044715 * (x * x * x))))


def _sigmoid(x):
    return 1.0 / (1.0 + jnp.exp(-x))


def _silu(x):
    return x * _sigmoid(x)


def _tile(n, pref, unit=1):
    t = (min(pref, n) // unit) * unit
    while t > unit and n % t:
        t -= unit
    assert t > 0 and n % t == 0, (n, pref, unit)
    return t


def _dot(a, b):
    return jnp.dot(a.astype(BF16), b.astype(BF16), preferred_element_type=F32)


def _dot_nt(a, b):
    return lax.dot_general(a.astype(BF16), b.astype(BF16), NT_DIMS, preferred_element_type=F32)


_ANY_SPEC = pl.BlockSpec(memory_space=pl.ANY)


def _norm_matmul_kernel(x_ref, g_ref, w_ref, o_ref, h_ref, *, act):
    @pl.when(pl.program_id(1) == 0)
    def _():
        h_ref[...] = _rms(x_ref[...], g_ref[...]).astype(BF16)

    acc = jnp.dot(h_ref[...], w_ref[...].astype(BF16), preferred_element_type=F32)
    if act == "sigmoid":
        acc = _sigmoid(acc)
    o_ref[...] = acc.astype(o_ref.dtype)


def norm_matmul(x, g, w, layer=None, *, act=None, out_dtype=F32, tm=1024, tn=512):
    m, k = x.shape
    n = w.shape[-1]
    tm, tn = _tile(m, tm, SUBLANES), _tile(n, tn, LANES)
    if layer is None:
        w_spec = pl.BlockSpec((k, tn), lambda i, j: (0, j))
    else:
        w_spec = pl.BlockSpec((None, k, tn), lambda i, j: (layer, 0, j))
    return pl.pallas_call(
        functools.partial(_norm_matmul_kernel, act=act),
        grid=(m // tm, n // tn),
        in_specs=[
            pl.BlockSpec((tm, k), lambda i, j: (i, 0)),
            pl.BlockSpec((1, k), lambda i, j: (0, 0)),
            w_spec,
        ],
        out_specs=pl.BlockSpec((tm, tn), lambda i, j: (i, j)),
        out_shape=jax.ShapeDtypeStruct((m, n), out_dtype),
        scratch_shapes=[pltpu.VMEM((tm, k), BF16)],
        compiler_params=_params(("parallel", "arbitrary")),
        name="norm_matmul",
    )(x, g.reshape(1, k), w)


def _hgrn_head(hq, hf, v, hg, lb3, ng, st_ref, hd, *, C, nb):
    R = nb * C
    lb = lb3[0:1, :]
    log_lb = lb3[1:2, :]
    log_1m_lb = lb3[2:3, :]
    q = _silu(hq)
    log_sig = jnp.minimum(hf, 0.0) - jnp.log1p(jnp.exp(-jnp.abs(hf)))
    c2 = log_1m_lb + log_sig
    logf = jnp.maximum(log_lb, c2) + jnp.log1p(jnp.exp(-jnp.abs(log_lb - c2)))
    k = (1.0 - lb) * (1.0 / (1.0 + jnp.exp(hf)))

    row = lax.broadcasted_iota(jnp.int32, (R, R), 0)
    col = lax.broadcasted_iota(jnp.int32, (R, R), 1)
    tri = jnp.where((col <= row) & ((row // C) == (col // C)), 1.0, 0.0).astype(F32)
    b = jnp.dot(tri, logf, precision=HIGHEST, preferred_element_type=F32)

    G = R // SUBLANES
    q3 = q.reshape(G, SUBLANES, HG_KEY)
    k3 = k.reshape(G, SUBLANES, HG_KEY)
    b3 = b.reshape(G, SUBLANES, HG_KEY)
    v3 = v.reshape(G, SUBLANES, HG_VAL)
    tpos = lax.broadcasted_iota(jnp.int32, (G, SUBLANES, 1), 1)
    o3 = jnp.zeros((G, SUBLANES, HG_VAL), F32)
    for s in range(SUBLANES):
        dec = jnp.exp(jnp.minimum(b3 - b3[:, s:s + 1, :], 0.0))
        a_s = jnp.sum(q3 * k3[:, s:s + 1, :] * dec, axis=-1, keepdims=True)
        o3 = o3 + jnp.where(tpos >= s, a_s, 0.0) * v3[:, s:s + 1, :]
    o = o3.reshape(R, HG_VAL)

    half = C // 2
    a_off = None
    rrow = lax.broadcasted_iota(jnp.int32, (R, 1), 0)
    while half >= SUBLANES:
        blk = 2 * half
        nblk = R // blk
        b4 = b.reshape(nblk, blk, HG_KEY)
        ref_b = jnp.broadcast_to(b4[:, half - 1:half, :], (nblk, blk, HG_KEY)).reshape(R, HG_KEY)
        second = (rrow % blk) >= half
        qe = jnp.where(second, q * jnp.exp(jnp.minimum(b - ref_b, 0.0)), 0.0)
        ke = jnp.where(second, 0.0, k * jnp.exp(jnp.minimum(ref_b - b, 0.0)))
        a_l = _dot_nt(qe, ke)
        if blk != R:
            a_l = jnp.where((row // blk) == (col // blk), a_l, 0.0)
        a_off = a_l if a_off is None else a_off + a_l
        half //= 2
    if a_off is not None:
        o = o + _dot(a_off, v)

    qb = q * jnp.exp(b)
    o_parts = []
    for n in range(nb):
        lo, hi = n * C, (n + 1) * C
        st = st_ref[n, hd]
        o_parts.append(_dot_nt(qb[lo:hi], st))
        b_last = b[hi - 1:hi, :]
        kd = k[lo:hi] * jnp.exp(b_last - b[lo:hi])
        st_ref[n, hd] = st * jnp.exp(b_last) + lax.dot_general(
            v[lo:hi].astype(BF16), kd.astype(BF16), TN_DIMS, preferred_element_type=F32)
    o = o + (o_parts[0] if nb == 1 else jnp.concatenate(o_parts, axis=0))
    return _rms(o, ng) * _silu(hg)


def _hgrn_kernel(*refs, C, nb, hh, n_chunks, has_s0):
    hq_ref, hf_ref, hi_ref, hg_ref, lb_ref, ng_ref = refs[:6]
    s0_ref = refs[6] if has_s0 else None
    o_ref, snew_ref, st_ref = refs[-3:]
    ci = pl.program_id(2)

    @pl.when(ci == 0)
    def _():
        for n in range(nb):
            for hd in range(hh):
                st_ref[n, hd] = s0_ref[n, hd].T if has_s0 else jnp.zeros((HG_VAL, HG_KEY), F32)

    for hd in range(hh):
        sl = slice(hd * HG_KEY, (hd + 1) * HG_KEY)
        o = _hgrn_head(hq_ref[:, sl], hf_ref[:, sl], hi_ref[:, sl], hg_ref[:, sl], lb_ref[:, sl], ng_ref[:, sl],
                       st_ref, hd, C=C, nb=nb)
        o_ref[:, sl] = o.astype(o_ref.dtype)

    @pl.when(ci == n_chunks - 1)
    def _():
        for n in range(nb):
            for hd in range(hh):
                snew_ref[n, hd] = st_ref[n, hd].T


def hgrn2(proj, row0, n_seq, seq_len, lb3, norm_g, s0_all, layer, o_prev, snew_prev, *, C, nb, hh):
    R = nb * C
    n_chunks = seq_len // C
    assert seq_len % C == 0 and n_seq % nb == 0 and row0 % R == 0 and HG_HEADS % hh == 0
    assert nb == 1 or n_chunks == 1
    rb0 = row0 // R
    blocks_per_group = nb * seq_len // R
    hb = HG_HEADS // hh
    w = hh * HG_KEY

    def rmap(sec):
        return lambda s, h, c: (rb0 + s * blocks_per_group + c, sec * hb + h)

    in_specs = [pl.BlockSpec((R, w), rmap(sec)) for sec in range(4)]
    in_specs += [pl.BlockSpec((3, w), lambda s, h, c: (0, h)),
                 pl.BlockSpec((1, w), lambda s, h, c: (0, h))]
    args = [proj, proj, proj, proj, lb3, norm_g.reshape(1, -1)]
    state_spec = pl.BlockSpec((None, nb, hh, HG_KEY, HG_VAL), lambda s, h, c: (layer, s, h, 0, 0))
    if s0_all is not None:
        in_specs.append(state_spec)
        args.append(s0_all)
    aliases = {}
    for out_idx, prev in enumerate((o_prev, snew_prev)):
        if prev is not None:
            aliases[len(args)] = out_idx
            in_specs.append(_ANY_SPEC)
            args.append(prev)
    o_shape = o_prev.shape if o_prev is not None else (proj.shape[0], HG_HEADS * HG_VAL)
    return pl.pallas_call(
        functools.partial(_hgrn_kernel, C=C, nb=nb, hh=hh, n_chunks=n_chunks, has_s0=s0_all is not None),
        grid=(n_seq // nb, hb, n_chunks),
        in_specs=in_specs,
        out_specs=[pl.BlockSpec((R, w), lambda s, h, c: (rb0 + s * blocks_per_group + c, h)), state_spec],
        out_shape=[jax.ShapeDtypeStruct(o_shape, BF16),
                   jax.ShapeDtypeStruct(snew_prev.shape, F32)],
        scratch_shapes=[pltpu.VMEM((nb, hh, HG_VAL, HG_KEY), F32)],
        input_output_aliases=aliases,
        compiler_params=_params(("parallel", "parallel", "arbitrary")),
        name="hgrn2",
    )(*args)


def _rope_pair(slot, tab):
    prod = slot * tab
    return prod + pltpu.roll(prod, MLA_ROPE, 1)


def _mla_q_kernel(x_ref, g_ref, w_ref, tab_ref, o_ref):
    qa = x_ref[...][:, :MLA_Q_RANK]
    acc = _dot(_rms(qa, g_ref[...]), w_ref[...]) * (MLA_SCALE * LOG2E)
    tab = tab_ref[...]
    lane = lax.broadcasted_iota(jnp.int32, tab.shape, 1)
    nope_w = MLA_HEADS * MLA_NOPE
    for h in range(MLA_HEADS):
        slot = acc[:, nope_w + h * LANES: nope_w + (h + 1) * LANES]
        qr = jnp.where(lane < MLA_ROPE, _rope_pair(slot, tab), 0.0)
        o_ref[:, h * HEAD_SLOT: h * HEAD_SLOT + MLA_NOPE] = acc[:, h * MLA_NOPE:(h + 1) * MLA_NOPE].astype(BF16)
        o_ref[:, h * HEAD_SLOT + MLA_NOPE:(h + 1) * HEAD_SLOT] = qr.astype(BF16)


def mla_q(proj, q_norm, w_q_ext, tab, *, tm=512):
    m = proj.shape[0]
    tm = _tile(m, tm, SUBLANES)
    n = w_q_ext.shape[1]
    return pl.pallas_call(
        _mla_q_kernel,
        grid=(m // tm,),
        in_specs=[pl.BlockSpec((tm, 1024), lambda i: (i, COL_QA // 1024)),
                  pl.BlockSpec((1, MLA_Q_RANK), lambda i: (0, 0)),
                  pl.BlockSpec((MLA_Q_RANK, n), lambda i: (0, 0)),
                  pl.BlockSpec((tm, LANES), lambda i: (i, 0))],
        out_specs=pl.BlockSpec((tm, MLA_HEADS * HEAD_SLOT), lambda i: (i, 0)),
        out_shape=jax.ShapeDtypeStruct((m, MLA_HEADS * HEAD_SLOT), BF16),
        compiler_params=_params(("parallel",)),
        name="mla_q",
    )(proj, q_norm.reshape(1, -1), w_q_ext, tab)


def _mla_kv_kernel(*refs, with_kv):
    kva_ref, kr_ref, g_ref, w_ref, tab_ref = refs[:5]
    outs = refs[-4:] if with_kv else refs[-2:]
    c_ref, krot_ref = outs[:2]
    c = _rms(kva_ref[...], g_ref[...])
    c_ref[...] = c
    krkr = _rope_pair(kr_ref[...], tab_ref[...])
    krot_ref[...] = krkr[:, :MLA_ROPE]
    if with_kv:
        kfull_ref, v_ref = outs[2:]
        acc = _dot(c, w_ref[...])
        krkr_b = krkr.astype(BF16)
        for h in range(MLA_HEADS):
            kfull_ref[:, h * HEAD_SLOT: h * HEAD_SLOT + MLA_NOPE] = (
                acc[:, h * HEAD_SLOT: h * HEAD_SLOT + MLA_NOPE].astype(BF16))
            kfull_ref[:, h * HEAD_SLOT + MLA_NOPE:(h + 1) * HEAD_SLOT] = krkr_b
            v_ref[h * MLA_V:(h + 1) * MLA_V, :] = acc[:, h * HEAD_SLOT + MLA_NOPE:(h + 1) * HEAD_SLOT].T.astype(BF16)


def mla_kv(proj, row0, n_rows, kv_norm, w_kv_b, tab, layer, lat_prev, kr_prev, *, with_kv, tm=512):
    tm = _tile(math.gcd(n_rows, row0) if row0 else n_rows, tm, SUBLANES)
    rb0 = row0 // tm
    out_specs = [pl.BlockSpec((None, tm, MLA_KV_RANK), lambda i: (layer, i, 0)),
                 pl.BlockSpec((None, tm, MLA_ROPE), lambda i: (layer, i, 0))]
    out_shape = [jax.ShapeDtypeStruct(lat_prev.shape, F32), jax.ShapeDtypeStruct(kr_prev.shape, F32)]
    if with_kv:
        out_specs += [pl.BlockSpec((tm, MLA_HEADS * HEAD_SLOT), lambda i: (i, 0)),
                      pl.BlockSpec((MLA_HEADS * MLA_V, tm), lambda i: (0, i))]
        out_shape += [jax.ShapeDtypeStruct((n_rows, MLA_HEADS * HEAD_SLOT), BF16),
                      jax.ShapeDtypeStruct((MLA_HEADS * MLA_V, n_rows), BF16)]
    return pl.pallas_call(
        functools.partial(_mla_kv_kernel, with_kv=with_kv),
        grid=(n_rows // tm,),
        in_specs=[pl.BlockSpec((tm, MLA_KV_RANK), lambda i: (rb0 + i, COL_KVA // MLA_KV_RANK)),
                  pl.BlockSpec((tm, LANES), lambda i: (rb0 + i, COL_KR // LANES)),
                  pl.BlockSpec((1, MLA_KV_RANK), lambda i: (0, 0)),
                  pl.BlockSpec((MLA_KV_RANK, MLA_HEADS * HEAD_SLOT), lambda i: (0, 0)),
                  pl.BlockSpec((tm, LANES), lambda i: (rb0 + i, 0)),
                  _ANY_SPEC, _ANY_SPEC],
        out_specs=out_specs,
        out_shape=out_shape,
        input_output_aliases={5: 0, 6: 1},
        compiler_params=_params(("parallel",)),
        name="mla_kv",
    )(proj, proj, kv_norm.reshape(1, -1), w_kv_b, tab, lat_prev, kr_prev)


def _flash_kernel(qi_ref, ki_ref, q_ref, k_ref, vt_ref, prev_ref, o_ref, m_ref, l_ref, acc_ref, *, tq, tk, hh):
    t = pl.program_id(2)
    qi = qi_ref[t]
    ki = ki_ref[t]
    last_k = ((qi + 1) * tq - 1) // tk

    @pl.when(ki == 0)
    def _():
        m_ref[...] = jnp.full(m_ref.shape, -jnp.inf, F32)
        l_ref[...] = jnp.zeros(l_ref.shape, F32)
        acc_ref[...] = jnp.zeros(acc_ref.shape, F32)

    def update(masked):
        for hd in range(hh):
            st = lax.dot_general(k_ref[:, hd * HEAD_SLOT:(hd + 1) * HEAD_SLOT],
                                 q_ref[:, hd * HEAD_SLOT:(hd + 1) * HEAD_SLOT], NT_DIMS,
                                 preferred_element_type=F32)
            if masked:
                kpos = ki * tk + lax.broadcasted_iota(jnp.int32, (tk, tq), 0)
                qpos = qi * tq + lax.broadcasted_iota(jnp.int32, (tk, tq), 1)
                st = jnp.where(kpos <= qpos, st, -jnp.inf)
            m_prev = m_ref[hd]
            m_new = jnp.maximum(m_prev, jnp.max(st, axis=0, keepdims=True))
            alpha = jnp.exp2(m_prev - m_new)
            p = jnp.exp2(st - m_new)
            l_ref[hd] = alpha * l_ref[hd] + jnp.sum(p, axis=0, keepdims=True)
            acc_ref[hd] = alpha * acc_ref[hd] + jnp.dot(
                vt_ref[hd * MLA_V:(hd + 1) * MLA_V, :], p.astype(BF16), preferred_element_type=F32)
            m_ref[hd] = m_new

    @pl.when((ki + 1) * tk - 1 > qi * tq)
    def _():
        update(True)

    @pl.when((ki + 1) * tk - 1 <= qi * tq)
    def _():
        update(False)

    @pl.when(ki == last_k)
    def _():
        for hd in range(hh):
            o_ref[:, hd * MLA_V:(hd + 1) * MLA_V] = (acc_ref[hd] / l_ref[hd]).T.astype(o_ref.dtype)


def mla_prompt_attention(q_full, k_full, v_t, n_batch, seq, o_prev, *, tq=512, tk=512, hh=4):
    tq, tk = _tile(seq, tq, LANES), _tile(seq, tk, LANES)
    nq, nk = seq // tq, seq // tk
    pairs = [(a, b) for a in range(nq) for b in range(((a + 1) * tq - 1) // tk + 1)]
    qi_tab = jnp.asarray([p[0] for p in pairs], jnp.int32)
    ki_tab = jnp.asarray([p[1] for p in pairs], jnp.int32)
    grid_spec = pltpu.PrefetchScalarGridSpec(
        num_scalar_prefetch=2,
        grid=(n_batch, MLA_HEADS // hh, len(pairs)),
        in_specs=[pl.BlockSpec((tq, hh * HEAD_SLOT), lambda b, h, t, qt, kt: (b * nq + qt[t], h)),
                  pl.BlockSpec((tk, hh * HEAD_SLOT), lambda b, h, t, qt, kt: (b * nk + kt[t], h)),
                  pl.BlockSpec((hh * MLA_V, tk), lambda b, h, t, qt, kt: (h, b * nk + kt[t])),
                  _ANY_SPEC],
        out_specs=pl.BlockSpec((tq, hh * MLA_V), lambda b, h, t, qt, kt: (b * nq + qt[t], h)),
        scratch_shapes=[pltpu.VMEM((hh, 1, tq), F32), pltpu.VMEM((hh, 1, tq), F32),
                        pltpu.VMEM((hh, MLA_V, tq), F32)],
    )
    return pl.pallas_call(
        functools.partial(_flash_kernel, tq=tq, tk=tk, hh=hh),
        grid_spec=grid_spec,
        out_shape=jax.ShapeDtypeStruct(o_prev.shape, BF16),
        input_output_aliases={5: 0},
        compiler_params=_params(("parallel", "parallel", "arbitrary")),
        name="mla_prompt_flash",
    )(qi_tab, ki_tab, q_full, k_full, v_t, o_prev)


def _absorb_kernel(q_ref, w_ref, o_ref):
    qh = q_ref[...]
    q_lat = _dot_nt(qh[:, :MLA_NOPE], w_ref[...])
    out = jnp.concatenate([q_lat, qh[:, MLA_NOPE:].astype(F32)], axis=-1)
    o_ref[...] = out.reshape(o_ref.shape)


def mla_absorb(q_full, w_kv_b, row0, n_seq, t_len):
    rows = n_seq * t_len
    assert row0 % rows == 0
    width = MLA_KV_RANK + LANES
    return pl.pallas_call(
        _absorb_kernel,
        grid=(MLA_HEADS,),
        in_specs=[pl.BlockSpec((rows, HEAD_SLOT), lambda h: (row0 // rows, h)),
                  pl.BlockSpec((MLA_KV_RANK, MLA_NOPE), lambda h: (0, 2 * h))],
        out_specs=pl.BlockSpec((n_seq, None, t_len, width), lambda h: (0, h, 0, 0)),
        out_shape=jax.ShapeDtypeStruct((n_seq, MLA_HEADS, t_len, width), F32),
        compiler_params=_params(("parallel",)),
        name="mla_absorb",
    )(q_full, w_kv_b)


def _paged_attn_kernel(pt_ref, q_ref, cnew_ref, krnew_ref, clat_hbm, ckr_hbm, o_ref,
                       cf32, krf32, cbuf, m_ref, l_ref, acc_ref, csem, krsem, *, G, layer, t_len, n_steps, n_seq):
    b = pl.program_id(0)
    g = pl.program_id(1)
    step = b * n_steps + g
    slot = step % 2
    page = cbuf.shape[0] // G

    def page_copies(bb, gg, sl):
        copies = []
        for i in range(G):
            pidx = pt_ref[bb, gg * G + i]
            copies.append(pltpu.make_async_copy(
                clat_hbm.at[layer, pidx], cf32.at[sl, pl.ds(i * page, page), :], csem.at[sl]))
            copies.append(pltpu.make_async_copy(
                ckr_hbm.at[layer, pidx], krf32.at[sl, :, pl.ds(i * page, page)], krsem.at[sl]))
        return copies

    @pl.when(step == 0)
    def _():
        for cp in page_copies(b, g, slot):
            cp.start()

    @pl.when(step + 1 < n_seq * n_steps)
    def _():
        wrap = g + 1 == n_steps
        for cp in page_copies(jnp.where(wrap, b + 1, b), jnp.where(wrap, 0, g + 1), 1 - slot):
            cp.start()

    q = q_ref[0]
    q_lat = q[:, :MLA_KV_RANK]
    q_rope = q[:, MLA_KV_RANK:MLA_KV_RANK + MLA_ROPE]

    @pl.when(g == 0)
    def _():
        c_new = cnew_ref[...]
        kr_new = krnew_ref[...]
        tq = lax.broadcasted_iota(jnp.int32, (q.shape[0], 1), 0) % t_len
        cols = []
        for j in range(t_len):
            sj = (jnp.sum(q_lat * c_new[j:j + 1, :], axis=-1, keepdims=True)
                  + jnp.sum(q_rope * kr_new[j:j + 1, :], axis=-1, keepdims=True))
            cols.append(jnp.where(tq >= j, sj, -jnp.inf))
        m0 = cols[0]
        for j in range(1, t_len):
            m0 = jnp.maximum(m0, cols[j])
        l0 = jnp.zeros_like(m0)
        acc0 = jnp.zeros(acc_ref.shape, F32)
        for j in range(t_len):
            pj = jnp.exp2(cols[j] - m0)
            l0 = l0 + pj
            acc0 = acc0 + pj * c_new[j:j + 1, :]
        m_ref[...] = m0
        l_ref[...] = l0
        acc_ref[...] = acc0

    for cp in page_copies(b, g, slot):
        cp.wait()

    q_lat_b = q_lat.astype(BF16)
    q_rope_b = q_rope.astype(BF16)
    s_parts = []
    for i in range(0, G, 2):
        cb = cf32[slot, pl.ds(i * page, 2 * page), :].astype(BF16)
        cbuf[i * page:(i + 2) * page, :] = cb
        krt = krf32[slot, :, pl.ds(i * page, 2 * page)].astype(BF16)
        s_parts.append(lax.dot_general(q_lat_b, cb, NT_DIMS, preferred_element_type=F32)
                       + jnp.dot(q_rope_b, krt, preferred_element_type=F32))
    s = jnp.concatenate(s_parts, axis=1)
    m_prev = m_ref[...]
    m_new = jnp.maximum(m_prev, jnp.max(s, axis=-1, keepdims=True))
    alpha = jnp.exp2(m_prev - m_new)
    p = jnp.exp2(s - m_new)
    l_ref[...] = alpha * l_ref[...] + jnp.sum(p, axis=-1, keepdims=True)
    acc_ref[...] = alpha * acc_ref[...] + jnp.dot(p.astype(BF16), cbuf[...], preferred_element_type=F32)
    m_ref[...] = m_new

    @pl.when(g == n_steps - 1)
    def _():
        o_ref[0] = acc_ref[...] / l_ref[...]


def mla_paged_attention(q_abs, lat_s, kr_s, cache_latent, cache_krope_t, layer, page_table, *, G=32):
    n_seq, heads, t_len, width = q_abs.shape
    n_pages = page_table.shape[1]
    page = cache_latent.shape[2]
    G = _tile(n_pages, G, 2)
    n_steps = n_pages // G
    rows = heads * t_len
    q3 = q_abs.reshape(n_seq, rows, width)
    grid_spec = pltpu.PrefetchScalarGridSpec(
        num_scalar_prefetch=1,
        grid=(n_seq, n_steps),
        in_specs=[pl.BlockSpec((1, rows, width), lambda b, g, pt: (b, 0, 0)),
                  pl.BlockSpec((None, t_len, MLA_KV_RANK), lambda b, g, pt: (layer, b, 0)),
                  pl.BlockSpec((None, t_len, MLA_ROPE), lambda b, g, pt: (layer, b, 0)),
                  _ANY_SPEC, _ANY_SPEC],
        out_specs=pl.BlockSpec((1, rows, MLA_KV_RANK), lambda b, g, pt: (b, 0, 0)),
        scratch_shapes=[pltpu.VMEM((2, G * page, MLA_KV_RANK), F32),
                        pltpu.VMEM((2, MLA_ROPE, G * page), F32),
                        pltpu.VMEM((G * page, MLA_KV_RANK), BF16),
                        pltpu.VMEM((rows, 1), F32), pltpu.VMEM((rows, 1), F32),
                        pltpu.VMEM((rows, MLA_KV_RANK), F32),
                        pltpu.SemaphoreType.DMA((2,)), pltpu.SemaphoreType.DMA((2,))],
    )
    return pl.pallas_call(
        functools.partial(_paged_attn_kernel, G=G, layer=layer, t_len=t_len, n_steps=n_steps, n_seq=n_seq),
        grid_spec=grid_spec,
        out_shape=jax.ShapeDtypeStruct((n_seq, rows, MLA_KV_RANK), F32),
        compiler_params=_params(("arbitrary", "arbitrary")),
        name="mla_paged_attention",
    )(page_table, q3, lat_s, kr_s, cache_latent, cache_krope_t)


def _uv_kernel(o_ref, w_ref, prev_ref, out_ref):
    o = o_ref[...]
    out_ref[...] = _dot(o.reshape(o.shape[0] * o.shape[1], o.shape[2]), w_ref[...]).astype(out_ref.dtype)


def mla_uv(o_lat, w_kv_b, row0, n_seq, t_len, ob_prev):
    o4 = o_lat.reshape(n_seq, MLA_HEADS, t_len, MLA_KV_RANK)
    rows = n_seq * t_len
    assert row0 % rows == 0
    return pl.pallas_call(
        _uv_kernel,
        grid=(MLA_HEADS,),
        in_specs=[pl.BlockSpec((n_seq, None, t_len, MLA_KV_RANK), lambda h: (0, h, 0, 0)),
                  pl.BlockSpec((MLA_KV_RANK, MLA_V), lambda h: (0, 2 * h + 1)),
                  _ANY_SPEC],
        out_specs=pl.BlockSpec((rows, MLA_V), lambda h: (row0 // rows, h)),
        out_shape=jax.ShapeDtypeStruct(ob_prev.shape, BF16),
        input_output_aliases={2: 0},
        compiler_params=_params(("parallel",)),
        name="mla_uv",
    )(o4, w_kv_b, ob_prev)


def _gmlp_kernel(*refs, emit_vn):
    gu_ref, gv_ref, g_ref, w_ref, bias_ref = refs[:5]
    o_ref = refs[-2] if emit_vn else refs[-1]
    vn = _rms(_gelu(gv_ref[...]), g_ref[...])
    if emit_vn:
        refs[-1][...] = vn
    u = _gelu(gu_ref[...])
    for grp in range(CM_GROUPS):
        sl = slice(grp * LANES, (grp + 1) * LANES)
        z = jnp.dot(w_ref[grp], vn[:, sl], precision=HIGHEST, preferred_element_type=F32) + bias_ref[:, sl]
        o_ref[:, sl] = (u[:, sl] * z).astype(o_ref.dtype)


def gmlp(proj, row0, n_rows, v_norm, w_mix, bias_full, oc_prev, *, emit_vn):
    width = CM_GROUPS * LANES
    assert row0 % CM_CHUNK == 0 and n_rows % CM_CHUNK == 0
    rb0 = row0 // CM_CHUNK
    in_specs = [pl.BlockSpec((CM_CHUNK, width), lambda i: (rb0 + i, COL_GU // width)),
                pl.BlockSpec((CM_CHUNK, width), lambda i: (rb0 + i, COL_GV // width)),
                pl.BlockSpec((1, width), lambda i: (0, 0)),
                pl.BlockSpec((CM_GROUPS, CM_CHUNK, CM_CHUNK), lambda i: (0, 0, 0)),
                pl.BlockSpec((CM_CHUNK, width), lambda i: (0, 0))]
    args = [proj, proj, v_norm.reshape(1, -1), w_mix, bias_full]
    aliases = {}
    if oc_prev is not None:
        aliases[len(args)] = 0
        in_specs.append(_ANY_SPEC)
        args.append(oc_prev)
    out_specs = [pl.BlockSpec((CM_CHUNK, width), lambda i: (rb0 + i, 0))]
    out_shape = [jax.ShapeDtypeStruct((proj.shape[0], width), BF16)]
    if emit_vn:
        out_specs.append(pl.BlockSpec((CM_CHUNK, width), lambda i: (i, 0)))
        out_shape.append(jax.ShapeDtypeStruct((n_rows, width), F32))
    return pl.pallas_call(
        functools.partial(_gmlp_kernel, emit_vn=emit_vn),
        grid=(n_rows // CM_CHUNK,),
        in_specs=in_specs,
        out_specs=out_specs,
        out_shape=out_shape,
        input_output_aliases=aliases,
        compiler_params=_params(("parallel",)),
        name="gmlp",
    )(*args)


def _merge_kernel(oa_ref, ob_ref, oc_ref, g0_ref, g1_ref, g2_ref, w_ref, o_ref):
    acc = None
    for br, (b_ref, g_ref) in enumerate(((oa_ref, g0_ref), (ob_ref, g1_ref), (oc_ref, g2_ref))):
        part = g_ref[...] * jnp.dot(b_ref[...], w_ref[br], preferred_element_type=F32)
        acc = part if acc is None else acc + part
    o_ref[...] = acc.astype(o_ref.dtype)


def branch_merge(o_a, o_b, o_c, gates, w_branch, *, tm=1024, tn=512):
    m, bw = o_a.shape
    d = w_branch.shape[2]
    tm, tn = _tile(m, tm, SUBLANES), _tile(d, tn, LANES)
    nt = d // tn
    bspec = pl.BlockSpec((tm, bw), lambda i, j: (i, 0))

    def gspec(br):
        return pl.BlockSpec((tm, tn), lambda i, j: (i, br * nt + j))

    return pl.pallas_call(
        _merge_kernel,
        grid=(m // tm, nt),
        in_specs=[bspec, bspec, bspec, gspec(0), gspec(1), gspec(2),
                  pl.BlockSpec((N_BRANCH, bw, tn), lambda i, j: (0, 0, j))],
        out_specs=pl.BlockSpec((tm, tn), lambda i, j: (i, j)),
        out_shape=jax.ShapeDtypeStruct((m, d), BF16),
        compiler_params=_params(("parallel", "arbitrary")),
        name="branch_merge",
    )(o_a, o_b, o_c, gates, gates, gates, w_branch)


def _out_proj_kernel(mix_ref, x_ref, w_ref, g_ref, o_ref):
    y = jnp.dot(mix_ref[...], w_ref[...], preferred_element_type=F32)
    o_ref[...] = x_ref[...] + _rms(y, g_ref[...])


def out_proj(mix, x, w_out, g, *, tm=512):
    m, d = x.shape
    tm = _tile(m, tm, SUBLANES)
    return pl.pallas_call(
        _out_proj_kernel,
        grid=(m // tm,),
        in_specs=[pl.BlockSpec((tm, d), lambda i: (i, 0)),
                  pl.BlockSpec((tm, d), lambda i: (i, 0)),
                  pl.BlockSpec((d, d), lambda i: (0, 0)),
                  pl.BlockSpec((1, d), lambda i: (0, 0))],
        out_specs=pl.BlockSpec((tm, d), lambda i: (i, 0)),
        out_shape=jax.ShapeDtypeStruct((m, d), F32),
        compiler_params=_params(("parallel",)),
        name="out_proj",
    )(mix, x, w_out, g.reshape(1, -1))


def _ffn_up_kernel(*refs, t_len, per_seq_prev, tiles_per_seq, kc):
    x_ref, g_ref, wg_ref, wv_ref, cw_ref, cb_ref = refs[:6]
    if per_seq_prev:
        pg_ref, pv_ref = refs[6:8]
        act_ref, tailg_ref, tailv_ref, h_ref = refs[-4:]
    else:
        act_ref, tailg_ref, tailv_ref, h_ref, carry_ref = refs[-5:]
    i = pl.program_id(0)
    j = pl.program_id(1)
    tm, tn = act_ref.shape

    @pl.when(j == 0)
    def _():
        h_ref[...] = _rms(x_ref[...], g_ref[...]).astype(BF16)

    if not per_seq_prev:
        @pl.when(i == 0)
        def _():
            carry_ref[j] = jnp.zeros(carry_ref.shape[1:], F32)

        seq_start = (i % tiles_per_seq) == 0

    def conv(u, which, sl, p_ref):
        if per_seq_prev:
            u = u.reshape(tm // t_len, t_len, u.shape[-1])
            p = p_ref[:, :, sl]
            tpos = lax.broadcasted_iota(jnp.int32, (1, t_len, 1), 1)
            back1 = p[:, 1:2, :]
            back2 = jnp.where(tpos == 0, p[:, 0:1, :], p[:, 1:2, :])
            axis = 1
        else:
            p = jnp.where(seq_start, 0.0, carry_ref[j, which][:, sl])
            tpos = lax.broadcasted_iota(jnp.int32, (tm, 1), 0)
            back1 = p[SUBLANES - 1:SUBLANES, :]
            back2 = jnp.where(tpos == 0, p[SUBLANES - 2:SUBLANES - 1, :], back1)
            axis = 0
        m1 = jnp.where(tpos == 0, back1, pltpu.roll(u, 1, axis))
        m2 = jnp.where(tpos <= 1, back2, pltpu.roll(u, 2, axis))
        y = (cb_ref[which:which + 1, sl] + cw_ref[which, 0:1, sl] * m2 + cw_ref[which, 1:2, sl] * m1
             + cw_ref[which, 2:3, sl] * u)
        return y.reshape(tm, y.shape[-1])

    h = h_ref[...]
    tails = []
    for c0 in range(0, tn, kc):
        sl = slice(c0, c0 + kc)
        ug = jnp.dot(h, wg_ref[:, sl].astype(BF16), preferred_element_type=F32)
        uv = jnp.dot(h, wv_ref[:, sl].astype(BF16), preferred_element_type=F32)
        act = _gelu(conv(ug, 0, sl, pg_ref if per_seq_prev else None)) * conv(
            uv, 1, sl, pv_ref if per_seq_prev else None)
        act_ref[:, sl] = act.astype(act_ref.dtype)
        if per_seq_prev:
            tailg_ref[:, :, sl] = ug.reshape(tm // t_len, t_len, kc)[:, t_len - (CONV_W - 1):, :]
            tailv_ref[:, :, sl] = uv.reshape(tm // t_len, t_len, kc)[:, t_len - (CONV_W - 1):, :]
        else:
            tailg_ref[:, sl] = ug[tm - SUBLANES:, :]
            tailv_ref[:, sl] = uv[tm - SUBLANES:, :]
            tails.append((sl, ug[tm - SUBLANES:, :], uv[tm - SUBLANES:, :]))
    for sl, tg, tv in tails:
        carry_ref[j, 0, :, sl] = tg
        carry_ref[j, 1, :, sl] = tv


def ffn_up(x, g, w_up_all, layer, conv_w2, conv_b2, prev_state, row0, n_rows, t_len, act_prev, *, tm=1024,
           tn=512, kc=256):
    d = x.shape[1]
    d_ff = w_up_all.shape[2] // 2
    per_seq_prev = t_len == SUBLANES
    tm = _tile(n_rows if per_seq_prev else t_len, tm, SUBLANES)
    tn = _tile(d_ff, tn, LANES)
    kc = _tile(tn, kc, LANES)
    n_j = d_ff // tn
    n_i = n_rows // tm
    assert row0 % tm == 0 and (tm % t_len == 0 or t_len % tm == 0)
    rb0 = row0 // tm
    in_specs = [pl.BlockSpec((tm, d), lambda i, j: (rb0 + i, 0)),
                pl.BlockSpec((1, d), lambda i, j: (0, 0)),
                pl.BlockSpec((None, d, tn), lambda i, j: (layer, 0, j)),
                pl.BlockSpec((None, d, tn), lambda i, j: (layer, 0, n_j + j)),
                pl.BlockSpec((2, CONV_W, tn), lambda i, j: (0, 0, j)),
                pl.BlockSpec((2, tn), lambda i, j: (0, j))]
    args = [x, g.reshape(1, -1), w_up_all, w_up_all, conv_w2, conv_b2]
    scratch = [pltpu.VMEM((tm, d), BF16)]
    if per_seq_prev:
        assert t_len == SUBLANES
        ns = tm // t_len
        in_specs += [pl.BlockSpec((None, ns, CONV_W - 1, tn), lambda i, j: (layer, i, 0, j)),
                     pl.BlockSpec((None, ns, CONV_W - 1, tn), lambda i, j: (layer, i, 0, n_j + j))]
        args += [prev_state, prev_state]
        tail_spec = pl.BlockSpec((ns, CONV_W - 1, tn), lambda i, j: (i, 0, j))
        tail_shape = jax.ShapeDtypeStruct((n_rows // t_len, CONV_W - 1, d_ff), F32)
    else:
        scratch.append(pltpu.VMEM((n_j, 2, SUBLANES, tn), F32))
        tail_spec = pl.BlockSpec((None, SUBLANES, tn), lambda i, j: (i, 0, j))
        tail_shape = jax.ShapeDtypeStruct((n_i, SUBLANES, d_ff), F32)
    aliases = {}
    if act_prev is not None:
        aliases[len(args)] = 0
        in_specs.append(_ANY_SPEC)
        args.append(act_prev)
    return pl.pallas_call(
        functools.partial(_ffn_up_kernel, t_len=t_len, per_seq_prev=per_seq_prev,
                          tiles_per_seq=max(t_len // tm, 1), kc=kc),
        grid=(n_i, n_j),
        in_specs=in_specs,
        out_specs=[pl.BlockSpec((tm, tn), lambda i, j: (rb0 + i, j)), tail_spec, tail_spec],
        out_shape=[jax.ShapeDtypeStruct((x.shape[0], d_ff), BF16), tail_shape, tail_shape],
        scratch_shapes=scratch,
        input_output_aliases=aliases,
        compiler_params=_params(("arbitrary", "arbitrary")),
        name="ffn_up",
    )(*args)


def _ffn_down_kernel(a_ref, w_ref, x_ref, g_ref, o_ref, acc_ref, *, n_k):
    kk = pl.program_id(1)
    part = jnp.dot(a_ref[...], w_ref[...], preferred_element_type=F32)

    @pl.when(kk == 0)
    def _():
        acc_ref[...] = part

    @pl.when(kk > 0)
    def _():
        acc_ref[...] += part

    @pl.when(kk == n_k - 1)
    def _():
        o_ref[...] = x_ref[...] + _rms(acc_ref[...], g_ref[...])


def ffn_down(act, w_down, x, g, *, tm=512, tk=1408):
    d_ff, d = w_down.shape
    m = x.shape[0]
    tm = _tile(m, tm, SUBLANES)
    tk = _tile(d_ff, tk, LANES)
    n_k = d_ff // tk
    return pl.pallas_call(
        functools.partial(_ffn_down_kernel, n_k=n_k),
        grid=(m // tm, n_k),
        in_specs=[pl.BlockSpec((tm, tk), lambda i, k: (i, k)),
                  pl.BlockSpec((tk, d), lambda i, k: (k, 0)),
                  pl.BlockSpec((tm, d), lambda i, k: (i, 0)),
                  pl.BlockSpec((1, d), lambda i, k: (0, 0))],
        out_specs=pl.BlockSpec((tm, d), lambda i, k: (i, 0)),
        out_shape=jax.ShapeDtypeStruct(x.shape, F32),
        scratch_shapes=[pltpu.VMEM((tm, d), F32)],
        compiler_params=_params(("parallel", "arbitrary")),
        name="ffn_down",
    )(act, w_down, x, g.reshape(1, -1))


def _ple_kernel(x_ref, pe_ref, g_ref, wg_ref, wp_ref, o_ref):
    x = x_ref[...]
    gate = _sigmoid(_dot(_rms(x, g_ref[...]), wg_ref[...]))
    o_ref[...] = x + _dot(pe_ref[...], wp_ref[...]) * gate


def ple(x, pe, g, w_gate, w_proj, *, tm=512):
    m, d = x.shape
    pd = pe.shape[1]
    tm = _tile(m, tm, SUBLANES)
    return pl.pallas_call(
        _ple_kernel,
        grid=(m // tm,),
        in_specs=[pl.BlockSpec((tm, d), lambda i: (i, 0)),
                  pl.BlockSpec((tm, pd), lambda i: (i, 0)),
                  pl.BlockSpec((1, d), lambda i: (0, 0)),
                  pl.BlockSpec((d, d), lambda i: (0, 0)),
                  pl.BlockSpec((pd, d), lambda i: (0, 0))],
        out_specs=pl.BlockSpec((tm, d), lambda i: (i, 0)),
        out_shape=jax.ShapeDtypeStruct((m, d), F32),
        compiler_params=_params(("parallel",)),
        name="ple",
    )(x, pe, g.reshape(1, -1), w_gate, w_proj)


def _rope_table(pos):
    inv = 1.0 / (ROPE_THETA ** (jnp.arange(0, MLA_ROPE, 2, dtype=F32) / MLA_ROPE))
    ang = pos.astype(F32)[:, None] * inv[None, :]
    cos, sin = jnp.cos(ang), jnp.sin(ang)
    return jnp.concatenate([cos, cos, -sin, sin], axis=-1)


def _swap_halves(w, width):
    half = width // 2
    return jnp.concatenate([w[..., half:], w[..., :half]], axis=-1)


def kernel(x_prompt, x_sample, cache_latent, cache_krope, state_hgrn, state_ffn_conv, page_table, p_prompt, p_sample, ln_mix_pre, ln_mix_post, ln_ffn_pre, ln_ffn_post, ln_ple, w_in, hgrn_lower_bounds, hgrn_norm, mla_q_norm, mla_kv_norm, w_q_b, w_kv_b, cm_v_norm, cm_w_s, cm_b, w_merge_gate, w_branch, w_out, w_ffn_up, ffn_conv_w, ffn_conv_b, w_ffn_down, w_ple_proj, w_ple_gate):
    n_b, seq, d = x_prompt.shape
    n_db, t_len, _ = x_sample.shape
    depth = w_in.shape[0]
    d_ff = w_ffn_down.shape[1]
    page = cache_latent.shape[2]
    past_len = page_table.shape[1] * page
    rows_p, rows_s = n_b * seq, n_db * t_len
    rows = rows_p + rows_s
    hk = HG_HEADS * HG_KEY

    lb_soft = jax.nn.softmax(hgrn_lower_bounds.astype(F32), axis=0)
    lbs = jnp.concatenate([jnp.zeros_like(lb_soft[:1]), jnp.cumsum(lb_soft[1:], axis=0)], axis=0)

    tab = jnp.concatenate([jnp.tile(_rope_table(jnp.arange(seq)), (n_b, 1)),
                           jnp.tile(_rope_table(past_len + jnp.arange(t_len)), (n_db, 1))], axis=0)
    cache_krope_t = jnp.swapaxes(cache_krope, 2, 3)

    x = jnp.concatenate([x_prompt.reshape(rows_p, d), x_sample.reshape(rows_s, d)], axis=0)
    tri = jnp.tril(jnp.ones((CM_CHUNK, CM_CHUNK), F32))
    seq_per_chunk = CM_CHUNK // t_len

    lat_p = jnp.zeros((depth, rows_p, MLA_KV_RANK), F32)
    kr_p = jnp.zeros((depth, rows_p, MLA_ROPE), F32)
    lat_s = jnp.zeros((depth, rows_s, MLA_KV_RANK), F32)
    kr_s = jnp.zeros((depth, rows_s, MLA_ROPE), F32)
    hg_p = jnp.zeros((depth, n_b, HG_HEADS, HG_KEY, HG_VAL), F32)
    hg_s = jnp.zeros((depth, n_db, HG_HEADS, HG_KEY, HG_VAL), F32)
    cv_p, cv_s, v_s = [], [], []

    for i in range(depth):
        wi = w_in[i]
        o_kr = 4 * hk + MLA_Q_RANK + MLA_KV_RANK
        kr_cols = wi[:, o_kr: o_kr + MLA_ROPE]
        w_in_p = jnp.concatenate(
            [wi[:, :4 * hk], wi[:, o_kr + MLA_ROPE:], wi[:, 4 * hk: 4 * hk + MLA_Q_RANK], kr_cols,
             _swap_halves(kr_cols, MLA_ROPE), jnp.zeros((d, LANES), F32),
             wi[:, 4 * hk + MLA_Q_RANK: o_kr]], axis=1).astype(BF16)
        wq = w_q_b[i].reshape(MLA_Q_RANK, MLA_HEADS, MLA_NOPE + MLA_ROPE)
        wq_rope = wq[..., MLA_NOPE:]
        w_q_ext = jnp.concatenate(
            [wq[..., :MLA_NOPE].reshape(MLA_Q_RANK, -1),
             jnp.concatenate([wq_rope, _swap_halves(wq_rope, MLA_ROPE)], axis=-1).reshape(MLA_Q_RANK, -1)],
            axis=1).astype(BF16)
        w_kvb = w_kv_b[i].astype(BF16)
        lb = lbs[i]
        lb3 = jnp.stack([lb, jnp.log(lb), jnp.log1p(-lb)], axis=0)
        w_mix_p = cm_w_s[i] * tri
        w_mix_s = jnp.einsum("ab,gts->gatbs", jnp.eye(seq_per_chunk, dtype=F32),
                             w_mix_p[:, :t_len, :t_len]).reshape(CM_GROUPS, CM_CHUNK, CM_CHUNK)
        bias_p = jnp.repeat(cm_b[i].T, LANES, axis=1)
        bias_s = jnp.tile(bias_p[:t_len], (seq_per_chunk, 1))
        conv_w2 = ffn_conv_w[i].reshape(CONV_W, 2, d_ff).transpose(1, 0, 2)
        conv_b2 = ffn_conv_b[i].reshape(2, d_ff)

        zeros_branch = jnp.zeros((rows, HG_HEADS * HG_VAL), BF16)
        proj = norm_matmul(x, ln_mix_pre[i], w_in_p, tn=1536)
        gates = norm_matmul(x, ln_mix_pre[i], w_merge_gate, i, act="sigmoid", tn=1024)

        o_a, hg_p = hgrn2(proj, 0, n_b, seq, lb3, hgrn_norm[i], None, i, zeros_branch, hg_p,
                          C=_tile(seq, 128, SUBLANES), nb=1, hh=4)
        o_a, hg_s = hgrn2(proj, rows_p, n_db, t_len, lb3, hgrn_norm[i], state_hgrn, i, o_a, hg_s,
                          C=t_len, nb=_tile(n_db, LANES // t_len), hh=2)

        q_full = mla_q(proj, mla_q_norm[i], w_q_ext, tab)
        lat_p, kr_p, k_full, v_all = mla_kv(proj, 0, rows_p, mla_kv_norm[i], w_kvb, tab, i, lat_p, kr_p,
                                            with_kv=True)
        lat_s, kr_s = mla_kv(proj, rows_p, rows_s, mla_kv_norm[i], w_kvb, tab, i, lat_s, kr_s, with_kv=False)
        o_b = mla_prompt_attention(q_full, k_full, v_all, n_b, seq, zeros_branch)
        q_abs = mla_absorb(q_full, w_kvb, rows_p, n_db, t_len)
        o_lat = mla_paged_attention(q_abs, lat_s, kr_s, cache_latent, cache_krope_t, i, page_table)
        o_b = mla_uv(o_lat, w_kvb, rows_p, n_db, t_len, o_b)

        (o_c,) = gmlp(proj, 0, rows_p, cm_v_norm[i], w_mix_p, bias_p, zeros_branch, emit_vn=False)
        o_c, vn_s = gmlp(proj, rows_p, rows_s, cm_v_norm[i], w_mix_s, bias_s, o_c, emit_vn=True)

        mix = branch_merge(o_a, o_b, o_c, gates, w_branch[i].astype(BF16))
        x = out_proj(mix, x, w_out[i].astype(BF16), ln_mix_post[i])

        act, tg_p, tv_p = ffn_up(x, ln_ffn_pre[i], w_ffn_up, i, conv_w2, conv_b2, None, 0, rows_p, seq,
                                 jnp.zeros((rows, d_ff), BF16))
        act, tg_s, tv_s = ffn_up(x, ln_ffn_pre[i], w_ffn_up, i, conv_w2, conv_b2, state_ffn_conv, rows_p, rows_s,
                                 t_len, act)
        x = ffn_down(act, w_ffn_down[i].astype(BF16), x, ln_ffn_post[i])

        pe = jnp.concatenate([p_prompt[i].reshape(rows_p, -1), p_sample[i].reshape(rows_s, -1)], axis=0)
        x = ple(x, pe, ln_ple[i], w_ple_gate[i].astype(BF16), w_ple_proj[i].astype(BF16))

        tails_p = jnp.concatenate([tg_p, tv_p], axis=-1).reshape(n_b, -1, SUBLANES, 2 * d_ff)
        cv_p.append(tails_p[:, -1, SUBLANES - (CONV_W - 1):, :])
        cv_s.append(jnp.concatenate([tg_s, tv_s], axis=-1))
        v_s.append(vn_s.reshape(n_db, t_len, -1))

    return (x[:rows_p].reshape(n_b, seq, d), x[rows_p:].reshape(n_db, t_len, d),
            lat_p.reshape(depth, n_b, seq, -1), kr_p.reshape(depth, n_b, seq, -1),
            lat_s.reshape(depth, n_db, t_len, -1), kr_s.reshape(depth, n_db, t_len, -1),
            hg_p, hg_s, jnp.stack(cv_p), jnp.stack(cv_s), jnp.stack(v_s))
```

```python
import functools
import math

import jax
import jax.numpy as jnp
import numpy as np
from jax import lax
from jax.experimental import pallas as pl
from jax.experimental.pallas import tpu as pltpu

F32 = jnp.float32
BF16 = jnp.bfloat16
HIGHEST = lax.Precision.HIGHEST

HG_HEADS = 8
HG_KEY = 128
HG_VAL = 128
MLA_HEADS = 8
MLA_NOPE = 128
MLA_ROPE = 64
MLA_V = 128
MLA_Q_RANK = 768
MLA_KV_RANK = 512
MLA_SCALE = (MLA_NOPE + MLA_ROPE) ** -0.5
ROPE_THETA = 10000.0
CM_GROUPS = 8
CM_CHUNK = 128
N_BRANCH = 3
CONV_W = 3
EPS = 1e-6
LOG2E = math.log2(math.e)

LANES = 128
SUBLANES = 8
VMEM_LIMIT = 56 * 1024 * 1024
HEAD_SLOT = 2 * LANES

COL_GU, COL_GV = 4096, 5120
COL_QA = 6144
COL_KR = 6912
COL_KVA = 7168
IN_COLS_PERM = 7680

NT_DIMS = (((1,), (1,)), ((), ()))
TN_DIMS = (((0,), (0,)), ((), ()))


def _params(sem):
    return pltpu.CompilerParams(dimension_semantics=sem, vmem_limit_bytes=VMEM_LIMIT)


def _rms(x, g):
    ms = jnp.mean(x * x, axis=-1, keepdims=True)
    return x * lax.rsqrt(ms + EPS) * g


def _gelu(x):
    return 0.5 * x * (1.0 + jnp.tanh(math.sqrt(2.0 / math.pi) * (x + 0.044715 * (x * x * x))))


def _sigmoid(x):
    return 0.5 * jnp.tanh(0.5 * x) + 0.5


def _silu(x):
    return x * _sigmoid(x)


def _tile(n, pref, unit=1):
    t = (min(pref, n) // unit) * unit
    while t > unit and n % t:
        t -= unit
    assert t > 0 and n % t == 0, (n, pref, unit)
    return t


def _dot(a, b):
    return jnp.dot(a.astype(BF16), b.astype(BF16), preferred_element_type=F32)


def _dot_nt(a, b):
    return lax.dot_general(a.astype(BF16), b.astype(BF16), NT_DIMS, preferred_element_type=F32)


_ANY_SPEC = pl.BlockSpec(memory_space=pl.ANY)


def _norm_matmul_kernel(x_ref, g_ref, w_ref, o_ref, h_ref, *, act):
    @pl.when(pl.program_id(1) == 0)
    def _():
        h_ref[...] = _rms(x_ref[...], g_ref[...]).astype(BF16)

    acc = jnp.dot(h_ref[...], w_ref[...].astype(BF16), preferred_element_type=F32)
    if act == "sigmoid":
        acc = _sigmoid(acc)
    o_ref[...] = acc.astype(o_ref.dtype)


def norm_matmul(x, g, w, layer=None, *, act=None, out_dtype=F32, tm=1024, tn=512):
    m, k = x.shape
    n = w.shape[-1]
    tm, tn = _tile(m, tm, SUBLANES), _tile(n, tn, LANES)
    if layer is None:
        w_spec = pl.BlockSpec((k, tn), lambda i, j: (0, j))
    else:
        w_spec = pl.BlockSpec((None, k, tn), lambda i, j: (layer, 0, j))
    return pl.pallas_call(
        functools.partial(_norm_matmul_kernel, act=act),
        grid=(m // tm, n // tn),
        in_specs=[
            pl.BlockSpec((tm, k), lambda i, j: (i, 0)),
            pl.BlockSpec((1, k), lambda i, j: (0, 0)),
            w_spec,
        ],
        out_specs=pl.BlockSpec((tm, tn), lambda i, j: (i, j)),
        out_shape=jax.ShapeDtypeStruct((m, n), out_dtype),
        scratch_shapes=[pltpu.VMEM((tm, k), BF16)],
        compiler_params=_params(("parallel", "arbitrary")),
        name="norm_matmul",
    )(x, g.reshape(1, k), w)


def _hgrn_head(hq, hf, v, hg, lb3, ng, st_ref, hd, *, C, nb):
    R = nb * C
    lb = lb3[0:1, :]
    log_lb = lb3[1:2, :]
    log_1m_lb = lb3[2:3, :]
    q = _silu(hq)
    log_sig = jnp.minimum(hf, 0.0) - jnp.log1p(jnp.exp(-jnp.abs(hf)))
    c2 = log_1m_lb + log_sig
    logf = jnp.maximum(log_lb, c2) + jnp.log1p(jnp.exp(-jnp.abs(log_lb - c2)))
    k = (1.0 - lb) * (1.0 / (1.0 + jnp.exp(hf)))

    row = lax.broadcasted_iota(jnp.int32, (R, R), 0)
    col = lax.broadcasted_iota(jnp.int32, (R, R), 1)
    tri = jnp.where((col <= row) & ((row // C) == (col // C)), 1.0, 0.0).astype(F32)
    b = jnp.dot(tri, logf, precision=HIGHEST, preferred_element_type=F32)

    G = R // SUBLANES
    q3 = q.reshape(G, SUBLANES, HG_KEY)
    k3 = k.reshape(G, SUBLANES, HG_KEY)
    b3 = b.reshape(G, SUBLANES, HG_KEY)
    v3 = v.reshape(G, SUBLANES, HG_VAL)
    tpos = lax.broadcasted_iota(jnp.int32, (G, SUBLANES, 1), 1)
    o3 = jnp.zeros((G, SUBLANES, HG_VAL), F32)
    for s in range(SUBLANES):
        dec = jnp.exp(b3 - b3[:, s:s + 1, :])
        a_s = jnp.sum(q3 * k3[:, s:s + 1, :] * dec, axis=-1, keepdims=True)
        o3 = o3 + jnp.where(tpos >= s, a_s, 0.0) * v3[:, s:s + 1, :]
    o = o3.reshape(R, HG_VAL)

    half = C // 2
    a_off = None
    rrow = lax.broadcasted_iota(jnp.int32, (R, 1), 0)
    while half >= SUBLANES:
        blk = 2 * half
        nblk = R // blk
        b4 = b.reshape(nblk, blk, HG_KEY)
        ref_b = jnp.broadcast_to(b4[:, half - 1:half, :], (nblk, blk, HG_KEY)).reshape(R, HG_KEY)
        second = (rrow % blk) >= half
        qe = jnp.where(second, q * jnp.exp(jnp.minimum(b - ref_b, 0.0)), 0.0)
        ke = jnp.where(second, 0.0, k * jnp.exp(jnp.minimum(ref_b - b, 0.0)))
        a_l = _dot_nt(qe, ke)
        if blk != R:
            a_l = jnp.where((row // blk) == (col // blk), a_l, 0.0)
        a_off = a_l if a_off is None else a_off + a_l
        half //= 2
    if a_off is not None:
        o = o + _dot(a_off, v)

    qb = q * jnp.exp(b)
    o_parts = []
    for n in range(nb):
        lo, hi = n * C, (n + 1) * C
        st = st_ref[n, hd]
        o_parts.append(_dot_nt(qb[lo:hi], st))
        b_last = b[hi - 1:hi, :]
        kd = k[lo:hi] * jnp.exp(b_last - b[lo:hi])
        st_ref[n, hd] = st * jnp.exp(b_last) + lax.dot_general(
            v[lo:hi].astype(BF16), kd.astype(BF16), TN_DIMS, preferred_element_type=F32)
    o = o + (o_parts[0] if nb == 1 else jnp.concatenate(o_parts, axis=0))
    return _rms(o, ng) * _silu(hg)


def _hgrn_kernel(*refs, C, nb, hh, n_chunks, has_s0):
    hq_ref, hf_ref, hi_ref, hg_ref, lb_ref, ng_ref = refs[:6]
    s0_ref = refs[6] if has_s0 else None
    o_ref, snew_ref, st_ref = refs[-3:]
    ci = pl.program_id(2)

    @pl.when(ci == 0)
    def _():
        for n in range(nb):
            for hd in range(hh):
                st_ref[n, hd] = s0_ref[n, hd].T if has_s0 else jnp.zeros((HG_VAL, HG_KEY), F32)

    for hd in range(hh):
        sl = slice(hd * HG_KEY, (hd + 1) * HG_KEY)
        o = _hgrn_head(hq_ref[:, sl], hf_ref[:, sl], hi_ref[:, sl], hg_ref[:, sl], lb_ref[:, sl], ng_ref[:, sl],
                       st_ref, hd, C=C, nb=nb)
        o_ref[:, sl] = o.astype(o_ref.dtype)

    @pl.when(ci == n_chunks - 1)
    def _():
        for n in range(nb):
            for hd in range(hh):
                snew_ref[n, hd] = st_ref[n, hd].T


def hgrn2(proj, row0, n_seq, seq_len, lb3, norm_g, s0_all, layer, o_prev, snew_prev, *, C, nb, hh):
    R = nb * C
    n_chunks = seq_len // C
    assert seq_len % C == 0 and n_seq % nb == 0 and row0 % R == 0 and HG_HEADS % hh == 0
    assert nb == 1 or n_chunks == 1
    rb0 = row0 // R
    blocks_per_group = nb * seq_len // R
    hb = HG_HEADS // hh
    w = hh * HG_KEY

    def rmap(sec):
        return lambda s, h, c: (rb0 + s * blocks_per_group + c, sec * hb + h)

    in_specs = [pl.BlockSpec((R, w), rmap(sec)) for sec in range(4)]
    in_specs += [pl.BlockSpec((3, w), lambda s, h, c: (0, h)),
                 pl.BlockSpec((1, w), lambda s, h, c: (0, h))]
    args = [proj, proj, proj, proj, lb3, norm_g.reshape(1, -1)]
    state_spec = pl.BlockSpec((None, nb, hh, HG_KEY, HG_VAL), lambda s, h, c: (layer, s, h, 0, 0))
    if s0_all is not None:
        in_specs.append(state_spec)
        args.append(s0_all)
    aliases = {}
    for out_idx, prev in enumerate((o_prev, snew_prev)):
        if prev is not None:
            aliases[len(args)] = out_idx
            in_specs.append(_ANY_SPEC)
            args.append(prev)
    o_shape = o_prev.shape if o_prev is not None else (proj.shape[0], HG_HEADS * HG_VAL)
    return pl.pallas_call(
        functools.partial(_hgrn_kernel, C=C, nb=nb, hh=hh, n_chunks=n_chunks, has_s0=s0_all is not None),
        grid=(n_seq // nb, hb, n_chunks),
        in_specs=in_specs,
        out_specs=[pl.BlockSpec((R, w), lambda s, h, c: (rb0 + s * blocks_per_group + c, h)), state_spec],
        out_shape=[jax.ShapeDtypeStruct(o_shape, BF16),
                   jax.ShapeDtypeStruct(snew_prev.shape, F32)],
        scratch_shapes=[pltpu.VMEM((nb, hh, HG_VAL, HG_KEY), F32)],
        input_output_aliases=aliases,
        compiler_params=_params(("parallel", "parallel", "arbitrary")),
        name="hgrn2",
    )(*args)


def _rope_pair(slot, tab):
    prod = slot * tab
    return prod + pltpu.roll(prod, MLA_ROPE, 1)


def _mla_q_kernel(x_ref, g_ref, w_ref, tab_ref, o_ref):
    qa = x_ref[...][:, :MLA_Q_RANK]
    acc = _dot(_rms(qa, g_ref[...]), w_ref[...]) * (MLA_SCALE * LOG2E)
    tab = tab_ref[...]
    lane = lax.broadcasted_iota(jnp.int32, tab.shape, 1)
    nope_w = MLA_HEADS * MLA_NOPE
    for h in range(MLA_HEADS):
        slot = acc[:, nope_w + h * LANES: nope_w + (h + 1) * LANES]
        qr = jnp.where(lane < MLA_ROPE, _rope_pair(slot, tab), 0.0)
        o_ref[:, h * HEAD_SLOT: h * HEAD_SLOT + MLA_NOPE] = acc[:, h * MLA_NOPE:(h + 1) * MLA_NOPE].astype(BF16)
        o_ref[:, h * HEAD_SLOT + MLA_NOPE:(h + 1) * HEAD_SLOT] = qr.astype(BF16)


def mla_q(proj, q_norm, w_q_ext, tab, *, tm=512):
    m = proj.shape[0]
    tm = _tile(m, tm, SUBLANES)
    n = w_q_ext.shape[1]
    return pl.pallas_call(
        _mla_q_kernel,
        grid=(m // tm,),
        in_specs=[pl.BlockSpec((tm, 1024), lambda i: (i, COL_QA // 1024)),
                  pl.BlockSpec((1, MLA_Q_RANK), lambda i: (0, 0)),
                  pl.BlockSpec((MLA_Q_RANK, n), lambda i: (0, 0)),
                  pl.BlockSpec((tm, LANES), lambda i: (i, 0))],
        out_specs=pl.BlockSpec((tm, MLA_HEADS * HEAD_SLOT), lambda i: (i, 0)),
        out_shape=jax.ShapeDtypeStruct((m, MLA_HEADS * HEAD_SLOT), BF16),
        compiler_params=_params(("parallel",)),
        name="mla_q",
    )(proj, q_norm.reshape(1, -1), w_q_ext, tab)


def _mla_kv_kernel(*refs, with_kv):
    kva_ref, kr_ref, g_ref, w_ref, tab_ref = refs[:5]
    outs = refs[-4:] if with_kv else refs[-2:]
    c_ref, krot_ref = outs[:2]
    c = _rms(kva_ref[...], g_ref[...])
    c_ref[...] = c
    krkr = _rope_pair(kr_ref[...], tab_ref[...])
    krot_ref[...] = krkr[:, :MLA_ROPE]
    if with_kv:
        kfull_ref, v_ref = outs[2:]
        acc = _dot(c, w_ref[...])
        krkr_b = krkr.astype(BF16)
        for h in range(MLA_HEADS):
            kfull_ref[:, h * HEAD_SLOT: h * HEAD_SLOT + MLA_NOPE] = (
                acc[:, h * HEAD_SLOT: h * HEAD_SLOT + MLA_NOPE].astype(BF16))
            kfull_ref[:, h * HEAD_SLOT + MLA_NOPE:(h + 1) * HEAD_SLOT] = krkr_b
            v_ref[h * MLA_V:(h + 1) * MLA_V, :] = acc[:, h * HEAD_SLOT + MLA_NOPE:(h + 1) * HEAD_SLOT].T.astype(BF16)


def mla_kv(proj, row0, n_rows, kv_norm, w_kv_b, tab, layer, lat_prev, kr_prev, *, with_kv, tm=512):
    tm = _tile(math.gcd(n_rows, row0) if row0 else n_rows, tm, SUBLANES)
    rb0 = row0 // tm
    out_specs = [pl.BlockSpec((None, tm, MLA_KV_RANK), lambda i: (layer, i, 0)),
                 pl.BlockSpec((None, tm, MLA_ROPE), lambda i: (layer, i, 0))]
    out_shape = [jax.ShapeDtypeStruct(lat_prev.shape, F32), jax.ShapeDtypeStruct(kr_prev.shape, F32)]
    if with_kv:
        out_specs += [pl.BlockSpec((tm, MLA_HEADS * HEAD_SLOT), lambda i: (i, 0)),
                      pl.BlockSpec((MLA_HEADS * MLA_V, tm), lambda i: (0, i))]
        out_shape += [jax.ShapeDtypeStruct((n_rows, MLA_HEADS * HEAD_SLOT), BF16),
                      jax.ShapeDtypeStruct((MLA_HEADS * MLA_V, n_rows), BF16)]
    return pl.pallas_call(
        functools.partial(_mla_kv_kernel, with_kv=with_kv),
        grid=(n_rows // tm,),
        in_specs=[pl.BlockSpec((tm, MLA_KV_RANK), lambda i: (rb0 + i, COL_KVA // MLA_KV_RANK)),
                  pl.BlockSpec((tm, LANES), lambda i: (rb0 + i, COL_KR // LANES)),
                  pl.BlockSpec((1, MLA_KV_RANK), lambda i: (0, 0)),
                  pl.BlockSpec((MLA_KV_RANK, MLA_HEADS * HEAD_SLOT), lambda i: (0, 0)),
                  pl.BlockSpec((tm, LANES), lambda i: (rb0 + i, 0)),
                  _ANY_SPEC, _ANY_SPEC],
        out_specs=out_specs,
        out_shape=out_shape,
        input_output_aliases={5: 0, 6: 1},
        compiler_params=_params(("parallel",)),
        name="mla_kv",
    )(proj, proj, kv_norm.reshape(1, -1), w_kv_b, tab, lat_prev, kr_prev)


def _flash_kernel(qi_ref, ki_ref, q_ref, k_ref, vt_ref, prev_ref, o_ref, m_ref, l_ref, acc_ref, s_ref, *, tq, tk,
                  hh):
    t = pl.program_id(2)
    qi = qi_ref[t]
    ki = ki_ref[t]
    last_k = ((qi + 1) * tq - 1) // tk

    @pl.when(ki == 0)
    def _():
        m_ref[...] = jnp.full(m_ref.shape, -jnp.inf, F32)
        l_ref[...] = jnp.zeros(l_ref.shape, F32)
        acc_ref[...] = jnp.zeros(acc_ref.shape, F32)

    def update(masked):
        for hd in range(hh):
            s_ref[hd] = lax.dot_general(k_ref[:, hd * HEAD_SLOT:(hd + 1) * HEAD_SLOT],
                                        q_ref[:, hd * HEAD_SLOT:(hd + 1) * HEAD_SLOT], NT_DIMS,
                                        preferred_element_type=F32)
        for hd in range(hh):
            st = s_ref[hd]
            if masked:
                kpos = ki * tk + lax.broadcasted_iota(jnp.int32, (tk, tq), 0)
                qpos = qi * tq + lax.broadcasted_iota(jnp.int32, (tk, tq), 1)
                st = jnp.where(kpos <= qpos, st, -jnp.inf)
            m_prev = m_ref[hd]
            m_new = jnp.maximum(m_prev, jnp.max(st, axis=0, keepdims=True))
            alpha = jnp.exp2(m_prev - m_new)
            p = jnp.exp2(st - m_new)
            l_ref[hd] = alpha * l_ref[hd] + jnp.sum(p, axis=0, keepdims=True)
            acc_ref[hd] = alpha * acc_ref[hd] + jnp.dot(
                vt_ref[hd * MLA_V:(hd + 1) * MLA_V, :], p.astype(BF16), preferred_element_type=F32)
            m_ref[hd] = m_new

    @pl.when((ki + 1) * tk - 1 > qi * tq)
    def _():
        update(True)

    @pl.when((ki + 1) * tk - 1 <= qi * tq)
    def _():
        update(False)

    @pl.when(ki == last_k)
    def _():
        for hd in range(hh):
            o_ref[:, hd * MLA_V:(hd + 1) * MLA_V] = (acc_ref[hd] / l_ref[hd]).T.astype(o_ref.dtype)


def mla_prompt_attention(q_full, k_full, v_t, n_batch, seq, o_prev, *, tq=512, tk=512, hh=4):
    tq, tk = _tile(seq, tq, LANES), _tile(seq, tk, LANES)
    nq, nk = seq // tq, seq // tk
    pairs = [(a, b) for a in range(nq) for b in range(((a + 1) * tq - 1) // tk + 1)]
    qi_tab = jnp.asarray([p[0] for p in pairs], jnp.int32)
    ki_tab = jnp.asarray([p[1] for p in pairs], jnp.int32)
    grid_spec = pltpu.PrefetchScalarGridSpec(
        num_scalar_prefetch=2,
        grid=(n_batch, MLA_HEADS // hh, len(pairs)),
        in_specs=[pl.BlockSpec((tq, hh * HEAD_SLOT), lambda b, h, t, qt, kt: (b * nq + qt[t], h)),
                  pl.BlockSpec((tk, hh * HEAD_SLOT), lambda b, h, t, qt, kt: (b * nk + kt[t], h)),
                  pl.BlockSpec((hh * MLA_V, tk), lambda b, h, t, qt, kt: (h, b * nk + kt[t])),
                  _ANY_SPEC],
        out_specs=pl.BlockSpec((tq, hh * MLA_V), lambda b, h, t, qt, kt: (b * nq + qt[t], h)),
        scratch_shapes=[pltpu.VMEM((hh, 1, tq), F32), pltpu.VMEM((hh, 1, tq), F32),
                        pltpu.VMEM((hh, MLA_V, tq), F32), pltpu.VMEM((hh, tk, tq), F32)],
    )
    return pl.pallas_call(
        functools.partial(_flash_kernel, tq=tq, tk=tk, hh=hh),
        grid_spec=grid_spec,
        out_shape=jax.ShapeDtypeStruct(o_prev.shape, BF16),
        input_output_aliases={5: 0},
        compiler_params=_params(("parallel", "parallel", "arbitrary")),
        name="mla_prompt_flash",
    )(qi_tab, ki_tab, q_full, k_full, v_t, o_prev)


def _absorb_kernel(q_ref, w_ref, o_ref):
    qh = q_ref[...]
    q_lat = _dot_nt(qh[:, :MLA_NOPE], w_ref[...])
    out = jnp.concatenate([q_lat, qh[:, MLA_NOPE:].astype(F32)], axis=-1)
    o_ref[...] = out.reshape(o_ref.shape)


def mla_absorb(q_full, w_kv_b, row0, n_seq, t_len):
    rows = n_seq * t_len
    assert row0 % rows == 0
    width = MLA_KV_RANK + LANES
    return pl.pallas_call(
        _absorb_kernel,
        grid=(MLA_HEADS,),
        in_specs=[pl.BlockSpec((rows, HEAD_SLOT), lambda h: (row0 // rows, h)),
                  pl.BlockSpec((MLA_KV_RANK, MLA_NOPE), lambda h: (0, 2 * h))],
        out_specs=pl.BlockSpec((n_seq, None, t_len, width), lambda h: (0, h, 0, 0)),
        out_shape=jax.ShapeDtypeStruct((n_seq, MLA_HEADS, t_len, width), F32),
        compiler_params=_params(("parallel",)),
        name="mla_absorb",
    )(q_full, w_kv_b)


def _paged_attn_kernel(pt_ref, q_ref, cnew_ref, krnew_ref, clat_hbm, ckr_hbm, o_ref,
                       cf32, krf32, cbuf, m_ref, l_ref, acc_ref, csem, krsem, *, G, layer, t_len, n_steps, n_seq):
    b = pl.program_id(0)
    g = pl.program_id(1)
    step = b * n_steps + g
    slot = step % 2
    page = cbuf.shape[0] // G

    def page_copies(bb, gg, sl):
        copies = []
        for i in range(G):
            pidx = pt_ref[bb, gg * G + i]
            copies.append(pltpu.make_async_copy(
                clat_hbm.at[layer, pidx], cf32.at[sl, pl.ds(i * page, page), :], csem.at[sl]))
            copies.append(pltpu.make_async_copy(
                ckr_hbm.at[layer, pidx], krf32.at[sl, :, pl.ds(i * page, page)], krsem.at[sl]))
        return copies

    @pl.when(step == 0)
    def _():
        for cp in page_copies(b, g, slot):
            cp.start()

    @pl.when(step + 1 < n_seq * n_steps)
    def _():
        wrap = g + 1 == n_steps
        for cp in page_copies(jnp.where(wrap, b + 1, b), jnp.where(wrap, 0, g + 1), 1 - slot):
            cp.start()

    q = q_ref[0]
    q_lat = q[:, :MLA_KV_RANK]
    q_rope = q[:, MLA_KV_RANK:MLA_KV_RANK + MLA_ROPE]

    @pl.when(g == 0)
    def _():
        c_new = cnew_ref[...]
        kr_new = krnew_ref[...]
        tq = lax.broadcasted_iota(jnp.int32, (q.shape[0], 1), 0) % t_len
        cols = []
        for j in range(t_len):
            sj = (jnp.sum(q_lat * c_new[j:j + 1, :], axis=-1, keepdims=True)
                  + jnp.sum(q_rope * kr_new[j:j + 1, :], axis=-1, keepdims=True))
            cols.append(jnp.where(tq >= j, sj, -jnp.inf))
        m0 = cols[0]
        for j in range(1, t_len):
            m0 = jnp.maximum(m0, cols[j])
        l0 = jnp.zeros_like(m0)
        acc0 = jnp.zeros(acc_ref.shape, F32)
        for j in range(t_len):
            pj = jnp.exp2(cols[j] - m0)
            l0 = l0 + pj
            acc0 = acc0 + pj * c_new[j:j + 1, :]
        m_ref[...] = m0
        l_ref[...] = l0
        acc_ref[...] = acc0

    pltpu.make_async_copy(cf32.at[slot], cf32.at[slot], csem.at[slot]).wait()
    pltpu.make_async_copy(krf32.at[slot], krf32.at[slot], krsem.at[slot]).wait()

    q_lat_b = q_lat.astype(BF16)
    q_rope_b = q_rope.astype(BF16)
    s_parts = []
    for i in range(0, G, 2):
        cb = cf32[slot, pl.ds(i * page, 2 * page), :].astype(BF16)
        cbuf[i * page:(i + 2) * page, :] = cb
        krt = krf32[slot, :, pl.ds(i * page, 2 * page)].astype(BF16)
        s_parts.append(lax.dot_general(q_lat_b, cb, NT_DIMS, preferred_element_type=F32)
                       + jnp.dot(q_rope_b, krt, preferred_element_type=F32))
    s = jnp.concatenate(s_parts, axis=1)
    m_prev = m_ref[...]
    m_new = jnp.maximum(m_prev, jnp.max(s, axis=-1, keepdims=True))
    alpha = jnp.exp2(m_prev - m_new)
    p = jnp.exp2(s - m_new)
    l_ref[...] = alpha * l_ref[...] + jnp.sum(p, axis=-1, keepdims=True)
    acc_ref[...] = alpha * acc_ref[...] + jnp.dot(p.astype(BF16), cbuf[...], preferred_element_type=F32)
    m_ref[...] = m_new

    @pl.when(g == n_steps - 1)
    def _():
        o_ref[0] = acc_ref[...] / l_ref[...]


def mla_paged_attention(q_abs, lat_s, kr_s, cache_latent, cache_krope_t, layer, page_table, *, G=32):
    n_seq, heads, t_len, width = q_abs.shape
    n_pages = page_table.shape[1]
    page = cache_latent.shape[2]
    G = _tile(n_pages, G, 2)
    n_steps = n_pages // G
    rows = heads * t_len
    q3 = q_abs.reshape(n_seq, rows, width)
    grid_spec = pltpu.PrefetchScalarGridSpec(
        num_scalar_prefetch=1,
        grid=(n_seq, n_steps),
        in_specs=[pl.BlockSpec((1, rows, width), lambda b, g, pt: (b, 0, 0)),
                  pl.BlockSpec((None, t_len, MLA_KV_RANK), lambda b, g, pt: (layer, b, 0)),
                  pl.BlockSpec((None, t_len, MLA_ROPE), lambda b, g, pt: (layer, b, 0)),
                  _ANY_SPEC, _ANY_SPEC],
        out_specs=pl.BlockSpec((1, rows, MLA_KV_RANK), lambda b, g, pt: (b, 0, 0)),
        scratch_shapes=[pltpu.VMEM((2, G * page, MLA_KV_RANK), F32),
                        pltpu.VMEM((2, MLA_ROPE, G * page), F32),
                        pltpu.VMEM((G * page, MLA_KV_RANK), BF16),
                        pltpu.VMEM((rows, 1), F32), pltpu.VMEM((rows, 1), F32),
                        pltpu.VMEM((rows, MLA_KV_RANK), F32),
                        pltpu.SemaphoreType.DMA((2,)), pltpu.SemaphoreType.DMA((2,))],
    )
    return pl.pallas_call(
        functools.partial(_paged_attn_kernel, G=G, layer=layer, t_len=t_len, n_steps=n_steps, n_seq=n_seq),
        grid_spec=grid_spec,
        out_shape=jax.ShapeDtypeStruct((n_seq, rows, MLA_KV_RANK), F32),
        compiler_params=_params(("arbitrary", "arbitrary")),
        name="mla_paged_attention",
    )(page_table, q3, lat_s, kr_s, cache_latent, cache_krope_t)


def _uv_kernel(o_ref, w_ref, prev_ref, out_ref):
    o = o_ref[...]
    out_ref[...] = _dot(o.reshape(o.shape[0] * o.shape[1], o.shape[2]), w_ref[...]).astype(out_ref.dtype)


def mla_uv(o_lat, w_kv_b, row0, n_seq, t_len, ob_prev):
    o4 = o_lat.reshape(n_seq, MLA_HEADS, t_len, MLA_KV_RANK)
    rows = n_seq * t_len
    assert row0 % rows == 0
    return pl.pallas_call(
        _uv_kernel,
        grid=(MLA_HEADS,),
        in_specs=[pl.BlockSpec((n_seq, None, t_len, MLA_KV_RANK), lambda h: (0, h, 0, 0)),
                  pl.BlockSpec((MLA_KV_RANK, MLA_V), lambda h: (0, 2 * h + 1)),
                  _ANY_SPEC],
        out_specs=pl.BlockSpec((rows, MLA_V), lambda h: (row0 // rows, h)),
        out_shape=jax.ShapeDtypeStruct(ob_prev.shape, BF16),
        input_output_aliases={2: 0},
        compiler_params=_params(("parallel",)),
        name="mla_uv",
    )(o4, w_kv_b, ob_prev)


def _gmlp_kernel(*refs, emit_vn):
    gu_ref, gv_ref, g_ref, w_ref, bias_ref = refs[:5]
    o_ref = refs[-2] if emit_vn else refs[-1]
    vn = _rms(_gelu(gv_ref[...]), g_ref[...])
    if emit_vn:
        refs[-1][...] = vn
    u = _gelu(gu_ref[...])
    for grp in range(CM_GROUPS):
        sl = slice(grp * LANES, (grp + 1) * LANES)
        z = jnp.dot(w_ref[grp], vn[:, sl], precision=HIGHEST, preferred_element_type=F32) + bias_ref[:, sl]
        o_ref[:, sl] = (u[:, sl] * z).astype(o_ref.dtype)


def gmlp(proj, row0, n_rows, v_norm, w_mix, bias_full, oc_prev, *, emit_vn):
    width = CM_GROUPS * LANES
    assert row0 % CM_CHUNK == 0 and n_rows % CM_CHUNK == 0
    rb0 = row0 // CM_CHUNK
    in_specs = [pl.BlockSpec((CM_CHUNK, width), lambda i: (rb0 + i, COL_GU // width)),
                pl.BlockSpec((CM_CHUNK, width), lambda i: (rb0 + i, COL_GV // width)),
                pl.BlockSpec((1, width), lambda i: (0, 0)),
                pl.BlockSpec((CM_GROUPS, CM_CHUNK, CM_CHUNK), lambda i: (0, 0, 0)),
                pl.BlockSpec((CM_CHUNK, width), lambda i: (0, 0))]
    args = [proj, proj, v_norm.reshape(1, -1), w_mix, bias_full]
    aliases = {}
    if oc_prev is not None:
        aliases[len(args)] = 0
        in_specs.append(_ANY_SPEC)
        args.append(oc_prev)
    out_specs = [pl.BlockSpec((CM_CHUNK, width), lambda i: (rb0 + i, 0))]
    out_shape = [jax.ShapeDtypeStruct((proj.shape[0], width), BF16)]
    if emit_vn:
        out_specs.append(pl.BlockSpec((CM_CHUNK, width), lambda i: (i, 0)))
        out_shape.append(jax.ShapeDtypeStruct((n_rows, width), F32))
    return pl.pallas_call(
        functools.partial(_gmlp_kernel, emit_vn=emit_vn),
        grid=(n_rows // CM_CHUNK,),
        in_specs=in_specs,
        out_specs=out_specs,
        out_shape=out_shape,
        input_output_aliases=aliases,
        compiler_params=_params(("parallel",)),
        name="gmlp",
    )(*args)


def _merge_kernel(oa_ref, ob_ref, oc_ref, g0_ref, g1_ref, g2_ref, w_ref, o_ref):
    acc = None
    for br, (b_ref, g_ref) in enumerate(((oa_ref, g0_ref), (ob_ref, g1_ref), (oc_ref, g2_ref))):
        part = g_ref[...] * jnp.dot(b_ref[...], w_ref[br], preferred_element_type=F32)
        acc = part if acc is None else acc + part
    o_ref[...] = acc.astype(o_ref.dtype)


def branch_merge(o_a, o_b, o_c, gates, w_branch, *, tm=1024, tn=512):
    m, bw = o_a.shape
    d = w_branch.shape[2]
    tm, tn = _tile(m, tm, SUBLANES), _tile(d, tn, LANES)
    nt = d // tn
    bspec = pl.BlockSpec((tm, bw), lambda i, j: (i, 0))

    def gspec(br):
        return pl.BlockSpec((tm, tn), lambda i, j: (i, br * nt + j))

    return pl.pallas_call(
        _merge_kernel,
        grid=(m // tm, nt),
        in_specs=[bspec, bspec, bspec, gspec(0), gspec(1), gspec(2),
                  pl.BlockSpec((N_BRANCH, bw, tn), lambda i, j: (0, 0, j))],
        out_specs=pl.BlockSpec((tm, tn), lambda i, j: (i, j)),
        out_shape=jax.ShapeDtypeStruct((m, d), BF16),
        compiler_params=_params(("parallel", "arbitrary")),
        name="branch_merge",
    )(o_a, o_b, o_c, gates, gates, gates, w_branch)


def _out_proj_kernel(mix_ref, x_ref, w_ref, g_ref, o_ref):
    y = jnp.dot(mix_ref[...], w_ref[...], preferred_element_type=F32)
    o_ref[...] = x_ref[...] + _rms(y, g_ref[...])


def out_proj(mix, x, w_out, g, *, tm=512):
    m, d = x.shape
    tm = _tile(m, tm, SUBLANES)
    return pl.pallas_call(
        _out_proj_kernel,
        grid=(m // tm,),
        in_specs=[pl.BlockSpec((tm, d), lambda i: (i, 0)),
                  pl.BlockSpec((tm, d), lambda i: (i, 0)),
                  pl.BlockSpec((d, d), lambda i: (0, 0)),
                  pl.BlockSpec((1, d), lambda i: (0, 0))],
        out_specs=pl.BlockSpec((tm, d), lambda i: (i, 0)),
        out_shape=jax.ShapeDtypeStruct((m, d), F32),
        compiler_params=_params(("parallel",)),
        name="out_proj",
    )(mix, x, w_out, g.reshape(1, -1))


def _ffn_up_kernel(*refs, t_len, per_seq_prev, tiles_per_seq, kc, rb):
    x_ref, g_ref, wg_ref, wv_ref, cw_ref, cb_ref = refs[:6]
    if per_seq_prev:
        pg_ref, pv_ref = refs[6:8]
        act_ref, tailg_ref, tailv_ref, h_ref = refs[-4:]
    else:
        act_ref, tailg_ref, tailv_ref, h_ref, carry_ref = refs[-5:]
    i = pl.program_id(0)
    j = pl.program_id(1)
    tm, tn = act_ref.shape

    @pl.when(j == 0)
    def _():
        h_ref[...] = _rms(x_ref[...], g_ref[...]).astype(BF16)

    if not per_seq_prev:
        @pl.when(i == 0)
        def _():
            carry_ref[j] = jnp.zeros(carry_ref.shape[1:], F32)

        seq_start = (i % tiles_per_seq) == 0

    def conv(u, which, sl, p):
        if per_seq_prev:
            u = u.reshape(rb // t_len, t_len, u.shape[-1])
            tpos = lax.broadcasted_iota(jnp.int32, (1, t_len, 1), 1)
            back1 = p[:, 1:2, :]
            back2 = jnp.where(tpos == 0, p[:, 0:1, :], p[:, 1:2, :])
            axis = 1
        else:
            tpos = lax.broadcasted_iota(jnp.int32, (rb, 1), 0)
            back1 = p[SUBLANES - 1:SUBLANES, :]
            back2 = jnp.where(tpos == 0, p[SUBLANES - 2:SUBLANES - 1, :], back1)
            axis = 0
        m1 = jnp.where(tpos == 0, back1, pltpu.roll(u, 1, axis))
        m2 = jnp.where(tpos <= 1, back2, pltpu.roll(u, 2, axis))
        y = (cb_ref[which:which + 1, sl] + cw_ref[which, 0:1, sl] * m2 + cw_ref[which, 1:2, sl] * m1
             + cw_ref[which, 2:3, sl] * u)
        return y.reshape(rb, y.shape[-1])

    tails = []
    for c0 in range(0, tn, kc):
        sl = slice(c0, c0 + kc)
        wg = wg_ref[:, sl].astype(BF16)
        wv = wv_ref[:, sl].astype(BF16)
        if not per_seq_prev:
            pg = jnp.where(seq_start, 0.0, carry_ref[j, 0][:, sl])
            pv = jnp.where(seq_start, 0.0, carry_ref[j, 1][:, sl])
        for r0 in range(0, tm, rb):
            hr = h_ref[r0:r0 + rb, :]
            ug = jnp.dot(hr, wg, preferred_element_type=F32)
            uv = jnp.dot(hr, wv, preferred_element_type=F32)
            if per_seq_prev:
                s0, s1 = r0 // t_len, (r0 + rb) // t_len
                pg = pg_ref[s0:s1, :, sl]
                pv = pv_ref[s0:s1, :, sl]
            act = _gelu(conv(ug, 0, sl, pg)) * conv(uv, 1, sl, pv)
            act_ref[r0:r0 + rb, sl] = act.astype(act_ref.dtype)
            if per_seq_prev:
                tailg_ref[s0:s1, :, sl] = ug.reshape(rb // t_len, t_len, kc)[:, t_len - (CONV_W - 1):, :]
                tailv_ref[s0:s1, :, sl] = uv.reshape(rb // t_len, t_len, kc)[:, t_len - (CONV_W - 1):, :]
            else:
                pg, pv = ug[rb - SUBLANES:, :], uv[rb - SUBLANES:, :]
        if not per_seq_prev:
            tailg_ref[:, sl] = pg
            tailv_ref[:, sl] = pv
            tails.append((sl, pg, pv))
    for sl, tg, tv in tails:
        carry_ref[j, 0, :, sl] = tg
        carry_ref[j, 1, :, sl] = tv


def ffn_up(x, g, w_up_all, layer, conv_w2, conv_b2, prev_state, row0, n_rows, t_len, act_prev, *, tm=1024,
           tn=512, kc=256, rb=512):
    d = x.shape[1]
    d_ff = w_up_all.shape[2] // 2
    per_seq_prev = t_len == SUBLANES
    tm = _tile(n_rows if per_seq_prev else t_len, tm, SUBLANES)
    tn = _tile(d_ff, tn, LANES)
    kc = _tile(tn, kc, LANES)
    n_j = d_ff // tn
    n_i = n_rows // tm
    assert row0 % tm == 0 and (tm % t_len == 0 or t_len % tm == 0)
    rb0 = row0 // tm
    in_specs = [pl.BlockSpec((tm, d), lambda i, j: (rb0 + i, 0)),
                pl.BlockSpec((1, d), lambda i, j: (0, 0)),
                pl.BlockSpec((None, d, tn), lambda i, j: (layer, 0, j)),
                pl.BlockSpec((None, d, tn), lambda i, j: (layer, 0, n_j + j)),
                pl.BlockSpec((2, CONV_W, tn), lambda i, j: (0, 0, j)),
                pl.BlockSpec((2, tn), lambda i, j: (0, j))]
    args = [x, g.reshape(1, -1), w_up_all, w_up_all, conv_w2, conv_b2]
    scratch = [pltpu.VMEM((tm, d), BF16)]
    if per_seq_prev:
        assert t_len == SUBLANES
        ns = tm // t_len
        in_specs += [pl.BlockSpec((None, ns, CONV_W - 1, tn), lambda i, j: (layer, i, 0, j)),
                     pl.BlockSpec((None, ns, CONV_W - 1, tn), lambda i, j: (layer, i, 0, n_j + j))]
        args += [prev_state, prev_state]
        tail_spec = pl.BlockSpec((ns, CONV_W - 1, tn), lambda i, j: (i, 0, j))
        tail_shape = jax.ShapeDtypeStruct((n_rows // t_len, CONV_W - 1, d_ff), F32)
    else:
        scratch.append(pltpu.VMEM((n_j, 2, SUBLANES, tn), F32))
        tail_spec = pl.BlockSpec((None, SUBLANES, tn), lambda i, j: (i, 0, j))
        tail_shape = jax.ShapeDtypeStruct((n_i, SUBLANES, d_ff), F32)
    aliases = {}
    if act_prev is not None:
        aliases[len(args)] = 0
        in_specs.append(_ANY_SPEC)
        args.append(act_prev)
    return pl.pallas_call(
        functools.partial(_ffn_up_kernel, t_len=t_len, per_seq_prev=per_seq_prev,
                          tiles_per_seq=max(t_len // tm, 1), kc=kc,
                          rb=_tile(tm, rb, t_len if per_seq_prev else SUBLANES)),
        grid=(n_i, n_j),
        in_specs=in_specs,
        out_specs=[pl.BlockSpec((tm, tn), lambda i, j: (rb0 + i, j)), tail_spec, tail_spec],
        out_shape=[jax.ShapeDtypeStruct((x.shape[0], d_ff), BF16), tail_shape, tail_shape],
        scratch_shapes=scratch,
        input_output_aliases=aliases,
        compiler_params=_params(("arbitrary", "arbitrary")),
        name="ffn_up",
    )(*args)


def _ffn_down_kernel(a_ref, w_ref, x_ref, g_ref, o_ref, acc_ref, *, n_k):
    kk = pl.program_id(1)
    part = jnp.dot(a_ref[...], w_ref[...], preferred_element_type=F32)
    if n_k == 1:
        o_ref[...] = x_ref[...] + _rms(part, g_ref[...])
        return

    @pl.when(kk == 0)
    def _():
        acc_ref[...] = part

    @pl.when((kk > 0) & (kk < n_k - 1))
    def _():
        acc_ref[...] += part

    @pl.when(kk == n_k - 1)
    def _():
        o_ref[...] = x_ref[...] + _rms(acc_ref[...] + part, g_ref[...])


def ffn_down(act, w_down, x, g, *, tm=512, tk=2816):
    d_ff, d = w_down.shape
    m = x.shape[0]
    tm = _tile(m, tm, SUBLANES)
    tk = _tile(d_ff, tk, LANES)
    n_k = d_ff // tk
    return pl.pallas_call(
        functools.partial(_ffn_down_kernel, n_k=n_k),
        grid=(m // tm, n_k),
        in_specs=[pl.BlockSpec((tm, tk), lambda i, k: (i, k)),
                  pl.BlockSpec((tk, d), lambda i, k: (k, 0)),
                  pl.BlockSpec((tm, d), lambda i, k: (i, 0)),
                  pl.BlockSpec((1, d), lambda i, k: (0, 0))],
        out_specs=pl.BlockSpec((tm, d), lambda i, k: (i, 0)),
        out_shape=jax.ShapeDtypeStruct(x.shape, F32),
        scratch_shapes=[pltpu.VMEM((tm, d), F32)],
        compiler_params=_params(("parallel", "arbitrary")),
        name="ffn_down",
    )(act, w_down, x, g.reshape(1, -1))


def _ple_kernel(x_ref, pe_ref, g_ref, wg_ref, wp_ref, o_ref):
    x = x_ref[...]
    gate = _sigmoid(_dot(_rms(x, g_ref[...]), wg_ref[...]))
    o_ref[...] = x + _dot(pe_ref[...], wp_ref[...]) * gate


def ple(x, pe, g, w_gate, w_proj, *, tm=512):
    m, d = x.shape
    pd = pe.shape[1]
    tm = _tile(m, tm, SUBLANES)
    return pl.pallas_call(
        _ple_kernel,
        grid=(m // tm,),
        in_specs=[pl.BlockSpec((tm, d), lambda i: (i, 0)),
                  pl.BlockSpec((tm, pd), lambda i: (i, 0)),
                  pl.BlockSpec((1, d), lambda i: (0, 0)),
                  pl.BlockSpec((d, d), lambda i: (0, 0)),
                  pl.BlockSpec((pd, d), lambda i: (0, 0))],
        out_specs=pl.BlockSpec((tm, d), lambda i: (i, 0)),
        out_shape=jax.ShapeDtypeStruct((m, d), F32),
        compiler_params=_params(("parallel",)),
        name="ple",
    )(x, pe, g.reshape(1, -1), w_gate, w_proj)


def _rope_table(pos):
    inv = 1.0 / (ROPE_THETA ** (jnp.arange(0, MLA_ROPE, 2, dtype=F32) / MLA_ROPE))
    ang = pos.astype(F32)[:, None] * inv[None, :]
    cos, sin = jnp.cos(ang), jnp.sin(ang)
    return jnp.concatenate([cos, cos, -sin, sin], axis=-1)


def _swap_halves(w, width):
    half = width // 2
    return jnp.concatenate([w[..., half:], w[..., :half]], axis=-1)


def kernel(x_prompt, x_sample, cache_latent, cache_krope, state_hgrn, state_ffn_conv, page_table, p_prompt, p_sample, ln_mix_pre, ln_mix_post, ln_ffn_pre, ln_ffn_post, ln_ple, w_in, hgrn_lower_bounds, hgrn_norm, mla_q_norm, mla_kv_norm, w_q_b, w_kv_b, cm_v_norm, cm_w_s, cm_b, w_merge_gate, w_branch, w_out, w_ffn_up, ffn_conv_w, ffn_conv_b, w_ffn_down, w_ple_proj, w_ple_gate):
    n_b, seq, d = x_prompt.shape
    n_db, t_len, _ = x_sample.shape
    depth = w_in.shape[0]
    d_ff = w_ffn_down.shape[1]
    page = cache_latent.shape[2]
    past_len = page_table.shape[1] * page
    rows_p, rows_s = n_b * seq, n_db * t_len
    rows = rows_p + rows_s
    hk = HG_HEADS * HG_KEY

    lb_soft = jax.nn.softmax(hgrn_lower_bounds.astype(F32), axis=0)
    lbs = jnp.concatenate([jnp.zeros_like(lb_soft[:1]), jnp.cumsum(lb_soft[1:], axis=0)], axis=0)

    tab = jnp.concatenate([jnp.tile(_rope_table(jnp.arange(seq)), (n_b, 1)),
                           jnp.tile(_rope_table(past_len + jnp.arange(t_len)), (n_db, 1))], axis=0)
    cache_krope_t = jnp.swapaxes(cache_krope, 2, 3)

    x = jnp.concatenate([x_prompt.reshape(rows_p, d), x_sample.reshape(rows_s, d)], axis=0)
    tri = jnp.tril(jnp.ones((CM_CHUNK, CM_CHUNK), F32))
    seq_per_chunk = CM_CHUNK // t_len

    lat_p = jnp.zeros((depth, rows_p, MLA_KV_RANK), F32)
    kr_p = jnp.zeros((depth, rows_p, MLA_ROPE), F32)
    lat_s = jnp.zeros((depth, rows_s, MLA_KV_RANK), F32)
    kr_s = jnp.zeros((depth, rows_s, MLA_ROPE), F32)
    hg_p = jnp.zeros((depth, n_b, HG_HEADS, HG_KEY, HG_VAL), F32)
    hg_s = jnp.zeros((depth, n_db, HG_HEADS, HG_KEY, HG_VAL), F32)
    cv_p, cv_s, v_s = [], [], []

    for i in range(depth):
        wi = w_in[i]
        o_kr = 4 * hk + MLA_Q_RANK + MLA_KV_RANK
        kr_cols = wi[:, o_kr: o_kr + MLA_ROPE]
        w_in_p = jnp.concatenate(
            [wi[:, :4 * hk], wi[:, o_kr + MLA_ROPE:], wi[:, 4 * hk: 4 * hk + MLA_Q_RANK], kr_cols,
             _swap_halves(kr_cols, MLA_ROPE), jnp.zeros((d, LANES), F32),
             wi[:, 4 * hk + MLA_Q_RANK: o_kr]], axis=1).astype(BF16)
        wq = w_q_b[i].reshape(MLA_Q_RANK, MLA_HEADS, MLA_NOPE + MLA_ROPE)
        wq_rope = wq[..., MLA_NOPE:]
        w_q_ext = jnp.concatenate(
            [wq[..., :MLA_NOPE].reshape(MLA_Q_RANK, -1),
             jnp.concatenate([wq_rope, _swap_halves(wq_rope, MLA_ROPE)], axis=-1).reshape(MLA_Q_RANK, -1)],
            axis=1).astype(BF16)
        w_kvb = w_kv_b[i].astype(BF16)
        lb = lbs[i]
        lb3 = jnp.stack([lb, jnp.log(lb), jnp.log1p(-lb)], axis=0)
        w_mix_p = cm_w_s[i] * tri
        w_mix_s = jnp.einsum("ab,gts->gatbs", jnp.eye(seq_per_chunk, dtype=F32),
                             w_mix_p[:, :t_len, :t_len]).reshape(CM_GROUPS, CM_CHUNK, CM_CHUNK)
        bias_p = jnp.repeat(cm_b[i].T, LANES, axis=1)
        bias_s = jnp.tile(bias_p[:t_len], (seq_per_chunk, 1))
        conv_w2 = ffn_conv_w[i].reshape(CONV_W, 2, d_ff).transpose(1, 0, 2)
        conv_b2 = ffn_conv_b[i].reshape(2, d_ff)

        zeros_branch = jnp.zeros((rows, HG_HEADS * HG_VAL), BF16)
        proj = norm_matmul(x, ln_mix_pre[i], w_in_p, tn=1536)
        gates = norm_matmul(x, ln_mix_pre[i], w_merge_gate, i, act="sigmoid", tn=1024)

        o_a, hg_p = hgrn2(proj, 0, n_b, seq, lb3, hgrn_norm[i], None, i, zeros_branch, hg_p,
                          C=_tile(seq, 128, SUBLANES), nb=1, hh=4)
        o_a, hg_s = hgrn2(proj, rows_p, n_db, t_len, lb3, hgrn_norm[i], state_hgrn, i, o_a, hg_s,
                          C=t_len, nb=_tile(n_db, LANES // t_len), hh=2)

        q_full = mla_q(proj, mla_q_norm[i], w_q_ext, tab)
        lat_p, kr_p, k_full, v_all = mla_kv(proj, 0, rows_p, mla_kv_norm[i], w_kvb, tab, i, lat_p, kr_p,
                                            with_kv=True)
        lat_s, kr_s = mla_kv(proj, rows_p, rows_s, mla_kv_norm[i], w_kvb, tab, i, lat_s, kr_s, with_kv=False)
        o_b = mla_prompt_attention(q_full, k_full, v_all, n_b, seq, zeros_branch)
        q_abs = mla_absorb(q_full, w_kvb, rows_p, n_db, t_len)
        o_lat = mla_paged_attention(q_abs, lat_s, kr_s, cache_latent, cache_krope_t, i, page_table)
        o_b = mla_uv(o_lat, w_kvb, rows_p, n_db, t_len, o_b)

        (o_c,) = gmlp(proj, 0, rows_p, cm_v_norm[i], w_mix_p, bias_p, zeros_branch, emit_vn=False)
        o_c, vn_s = gmlp(proj, rows_p, rows_s, cm_v_norm[i], w_mix_s, bias_s, o_c, emit_vn=True)

        mix = branch_merge(o_a, o_b, o_c, gates, w_branch[i].astype(BF16))
        x = out_proj(mix, x, w_out[i].astype(BF16), ln_mix_post[i])

        act, tg_p, tv_p = ffn_up(x, ln_ffn_pre[i], w_ffn_up, i, conv_w2, conv_b2, None, 0, rows_p, seq,
                                 jnp.zeros((rows, d_ff), BF16))
        act, tg_s, tv_s = ffn_up(x, ln_ffn_pre[i], w_ffn_up, i, conv_w2, conv_b2, state_ffn_conv, rows_p, rows_s,
                                 t_len, act)
        x = ffn_down(act, w_ffn_down[i].astype(BF16), x, ln_ffn_post[i])

        pe = jnp.concatenate([p_prompt[i].reshape(rows_p, -1), p_sample[i].reshape(rows_s, -1)], axis=0)
        x = ple(x, pe, ln_ple[i], w_ple_gate[i].astype(BF16), w_ple_proj[i].astype(BF16))

        tails_p = jnp.concatenate([tg_p, tv_p], axis=-1).reshape(n_b, -1, SUBLANES, 2 * d_ff)
        cv_p.append(tails_p[:, -1, SUBLANES - (CONV_W - 1):, :])
        cv_s.append(jnp.concatenate([tg_s, tv_s], axis=-1))
        v_s.append(vn_s.reshape(n_db, t_len, -1))

    return (x[:rows_p].reshape(n_b, seq, d), x[rows_p:].reshape(n_db, t_len, d),
            lat_p.reshape(depth, n_b, seq, -1), kr_p.reshape(depth, n_b, seq, -1),
            lat_s.reshape(depth, n_db, t_len, -1), kr_s.reshape(depth, n_db, t_len, -1),
            hg_p, hg_s, jnp.stack(cv_p), jnp.stack(cv_s), jnp.stack(v_s))
```

```python
import functools
import math

import jax
import jax.numpy as jnp
import numpy as np
from jax import lax
from jax.experimental import pallas as pl
from jax.experimental.pallas import tpu as pltpu

F32 = jnp.float32
BF16 = jnp.bfloat16
HIGHEST = lax.Precision.HIGHEST

HG_HEADS = 8
HG_KEY = 128
HG_VAL = 128
MLA_HEADS = 8
MLA_NOPE = 128
MLA_ROPE = 64
MLA_V = 128
MLA_Q_RANK = 768
MLA_KV_RANK = 512
MLA_SCALE = (MLA_NOPE + MLA_ROPE) ** -0.5
ROPE_THETA = 10000.0
CM_GROUPS = 8
CM_CHUNK = 128
N_BRANCH = 3
CONV_W = 3
EPS = 1e-6
LOG2E = math.log2(math.e)

LANES = 128
SUBLANES = 8
VMEM_LIMIT = 56 * 1024 * 1024
HEAD_SLOT = 2 * LANES

COL_GU, COL_GV = 4096, 5120
COL_QA = 6144
COL_KR = 6912
COL_KVA = 7168
IN_COLS_PERM = 7680

NT_DIMS = (((1,), (1,)), ((), ()))
TN_DIMS = (((0,), (0,)), ((), ()))


def _params(sem):
    return pltpu.CompilerParams(dimension_semantics=sem, vmem_limit_bytes=VMEM_LIMIT)


def _rms(x, g):
    ms = jnp.mean(x * x, axis=-1, keepdims=True)
    return x * lax.rsqrt(ms + EPS) * g


def _gelu(x):
    return 0.5 * x * (1.0 + jnp.tanh(math.sqrt(2.0 / math.pi) * (x + 0.044715 * (x * x * x))))


def _sigmoid(x):
    return 0.5 * jnp.tanh(0.5 * x) + 0.5


def _silu(x):
    return x * _sigmoid(x)


def _tile(n, pref, unit=1):
    t = (min(pref, n) // unit) * unit
    while t > unit and n % t:
        t -= unit
    assert t > 0 and n % t == 0, (n, pref, unit)
    return t


def _dot(a, b):
    return jnp.dot(a.astype(BF16), b.astype(BF16), preferred_element_type=F32)


def _dot_nt(a, b):
    return lax.dot_general(a.astype(BF16), b.astype(BF16), NT_DIMS, preferred_element_type=F32)


_ANY_SPEC = pl.BlockSpec(memory_space=pl.ANY)


def _norm_matmul_kernel(x_ref, g_ref, w_ref, o_ref, h_ref, *, act):
    @pl.when(pl.program_id(1) == 0)
    def _():
        h_ref[...] = _rms(x_ref[...], g_ref[...]).astype(BF16)

    acc = jnp.dot(h_ref[...], w_ref[...].astype(BF16), preferred_element_type=F32)
    if act == "sigmoid":
        acc = _sigmoid(acc)
    o_ref[...] = acc.astype(o_ref.dtype)


def norm_matmul(x, g, w, layer=None, *, act=None, out_dtype=F32, tm=1024, tn=512):
    m, k = x.shape
    n = w.shape[-1]
    tm, tn = _tile(m, tm, SUBLANES), _tile(n, tn, LANES)
    if layer is None:
        w_spec = pl.BlockSpec((k, tn), lambda i, j: (0, j))
    else:
        w_spec = pl.BlockSpec((None, k, tn), lambda i, j: (layer, 0, j))
    return pl.pallas_call(
        functools.partial(_norm_matmul_kernel, act=act),
        grid=(m // tm, n // tn),
        in_specs=[
            pl.BlockSpec((tm, k), lambda i, j: (i, 0)),
            pl.BlockSpec((1, k), lambda i, j: (0, 0)),
            w_spec,
        ],
        out_specs=pl.BlockSpec((tm, tn), lambda i, j: (i, j)),
        out_shape=jax.ShapeDtypeStruct((m, n), out_dtype),
        scratch_shapes=[pltpu.VMEM((tm, k), BF16)],
        compiler_params=_params(("parallel", "arbitrary")),
        name="norm_matmul",
    )(x, g.reshape(1, k), w)


def _hgrn_head(hq, hf, v, hg, lb3, ng, st_ref, hd, *, C, nb):
    R = nb * C
    lb = lb3[0:1, :]
    log_lb = lb3[1:2, :]
    log_1m_lb = lb3[2:3, :]
    q = _silu(hq)
    log_sig = jnp.minimum(hf, 0.0) - jnp.log1p(jnp.exp(-jnp.abs(hf)))
    c2 = log_1m_lb + log_sig
    logf = jnp.maximum(log_lb, c2) + jnp.log1p(jnp.exp(-jnp.abs(log_lb - c2)))
    k = (1.0 - lb) * (1.0 / (1.0 + jnp.exp(hf)))

    row = lax.broadcasted_iota(jnp.int32, (R, R), 0)
    col = lax.broadcasted_iota(jnp.int32, (R, R), 1)
    tri = jnp.where((col <= row) & ((row // C) == (col // C)), 1.0, 0.0).astype(F32)
    b = jnp.dot(tri, logf, precision=HIGHEST, preferred_element_type=F32)

    G = R // SUBLANES
    q3 = q.reshape(G, SUBLANES, HG_KEY)
    k3 = k.reshape(G, SUBLANES, HG_KEY)
    b3 = b.reshape(G, SUBLANES, HG_KEY)
    v3 = v.reshape(G, SUBLANES, HG_VAL)
    tpos = lax.broadcasted_iota(jnp.int32, (G, SUBLANES, 1), 1)
    o3 = jnp.zeros((G, SUBLANES, HG_VAL), F32)
    for s in range(SUBLANES):
        dec = jnp.exp(b3 - b3[:, s:s + 1, :])
        a_s = jnp.sum(q3 * k3[:, s:s + 1, :] * dec, axis=-1, keepdims=True)
        o3 = o3 + jnp.where(tpos >= s, a_s, 0.0) * v3[:, s:s + 1, :]
    o = o3.reshape(R, HG_VAL)

    half = C // 2
    a_off = None
    rrow = lax.broadcasted_iota(jnp.int32, (R, 1), 0)
    while half >= SUBLANES:
        blk = 2 * half
        nblk = R // blk
        b4 = b.reshape(nblk, blk, HG_KEY)
        ref_b = jnp.broadcast_to(b4[:, half - 1:half, :], (nblk, blk, HG_KEY)).reshape(R, HG_KEY)
        second = (rrow % blk) >= half
        qe = jnp.where(second, q * jnp.exp(jnp.minimum(b - ref_b, 0.0)), 0.0)
        ke = jnp.where(second, 0.0, k * jnp.exp(jnp.minimum(ref_b - b, 0.0)))
        a_l = _dot_nt(qe, ke)
        if blk != R:
            a_l = jnp.where((row // blk) == (col // blk), a_l, 0.0)
        a_off = a_l if a_off is None else a_off + a_l
        half //= 2
    if a_off is not None:
        o = o + _dot(a_off, v)

    qb = q * jnp.exp(b)
    o_parts = []
    for n in range(nb):
        lo, hi = n * C, (n + 1) * C
        st = st_ref[n, hd]
        o_parts.append(_dot_nt(qb[lo:hi], st))
        b_last = b[hi - 1:hi, :]
        kd = k[lo:hi] * jnp.exp(b_last - b[lo:hi])
        st_ref[n, hd] = st * jnp.exp(b_last) + lax.dot_general(
            v[lo:hi].astype(BF16), kd.astype(BF16), TN_DIMS, preferred_element_type=F32)
    o = o + (o_parts[0] if nb == 1 else jnp.concatenate(o_parts, axis=0))
    return _rms(o, ng) * _silu(hg)


def _hgrn_kernel(*refs, C, nb, hh, n_chunks, has_s0):
    hq_ref, hf_ref, hi_ref, hg_ref, lb_ref, ng_ref = refs[:6]
    s0_ref = refs[6] if has_s0 else None
    o_ref, snew_ref, st_ref = refs[-3:]
    ci = pl.program_id(2)

    @pl.when(ci == 0)
    def _():
        for n in range(nb):
            for hd in range(hh):
                st_ref[n, hd] = s0_ref[n, hd].T if has_s0 else jnp.zeros((HG_VAL, HG_KEY), F32)

    for hd in range(hh):
        sl = slice(hd * HG_KEY, (hd + 1) * HG_KEY)
        o = _hgrn_head(hq_ref[:, sl], hf_ref[:, sl], hi_ref[:, sl], hg_ref[:, sl], lb_ref[:, sl], ng_ref[:, sl],
                       st_ref, hd, C=C, nb=nb)
        o_ref[:, sl] = o.astype(o_ref.dtype)

    @pl.when(ci == n_chunks - 1)
    def _():
        for n in range(nb):
            for hd in range(hh):
                snew_ref[n, hd] = st_ref[n, hd].T


def hgrn2(proj, row0, n_seq, seq_len, lb3, norm_g, s0_all, layer, o_prev, snew_prev, *, C, nb, hh):
    R = nb * C
    n_chunks = seq_len // C
    assert seq_len % C == 0 and n_seq % nb == 0 and row0 % R == 0 and HG_HEADS % hh == 0
    assert nb == 1 or n_chunks == 1
    rb0 = row0 // R
    blocks_per_group = nb * seq_len // R
    hb = HG_HEADS // hh
    w = hh * HG_KEY

    def rmap(sec):
        return lambda s, h, c: (rb0 + s * blocks_per_group + c, sec * hb + h)

    in_specs = [pl.BlockSpec((R, w), rmap(sec)) for sec in range(4)]
    in_specs += [pl.BlockSpec((3, w), lambda s, h, c: (0, h)),
                 pl.BlockSpec((1, w), lambda s, h, c: (0, h))]
    args = [proj, proj, proj, proj, lb3, norm_g.reshape(1, -1)]
    state_spec = pl.BlockSpec((None, nb, hh, HG_KEY, HG_VAL), lambda s, h, c: (layer, s, h, 0, 0))
    if s0_all is not None:
        in_specs.append(state_spec)
        args.append(s0_all)
    aliases = {}
    for out_idx, prev in enumerate((o_prev, snew_prev)):
        if prev is not None:
            aliases[len(args)] = out_idx
            in_specs.append(_ANY_SPEC)
            args.append(prev)
    o_shape = o_prev.shape if o_prev is not None else (proj.shape[0], HG_HEADS * HG_VAL)
    return pl.pallas_call(
        functools.partial(_hgrn_kernel, C=C, nb=nb, hh=hh, n_chunks=n_chunks, has_s0=s0_all is not None),
        grid=(n_seq // nb, hb, n_chunks),
        in_specs=in_specs,
        out_specs=[pl.BlockSpec((R, w), lambda s, h, c: (rb0 + s * blocks_per_group + c, h)), state_spec],
        out_shape=[jax.ShapeDtypeStruct(o_shape, BF16),
                   jax.ShapeDtypeStruct(snew_prev.shape, F32)],
        scratch_shapes=[pltpu.VMEM((nb, hh, HG_VAL, HG_KEY), F32)],
        input_output_aliases=aliases,
        compiler_params=_params(("parallel", "parallel", "arbitrary")),
        name="hgrn2",
    )(*args)


def _rope_pair(slot, tab):
    prod = slot * tab
    return prod + pltpu.roll(prod, MLA_ROPE, 1)


def _mla_q_kernel(x_ref, g_ref, w_ref, tab_ref, o_ref):
    qa = x_ref[...][:, :MLA_Q_RANK]
    acc = _dot(_rms(qa, g_ref[...]), w_ref[...]) * (MLA_SCALE * LOG2E)
    tab = tab_ref[...]
    lane = lax.broadcasted_iota(jnp.int32, tab.shape, 1)
    nope_w = MLA_HEADS * MLA_NOPE
    for h in range(MLA_HEADS):
        slot = acc[:, nope_w + h * LANES: nope_w + (h + 1) * LANES]
        qr = jnp.where(lane < MLA_ROPE, _rope_pair(slot, tab), 0.0)
        o_ref[:, h * HEAD_SLOT: h * HEAD_SLOT + MLA_NOPE] = acc[:, h * MLA_NOPE:(h + 1) * MLA_NOPE].astype(BF16)
        o_ref[:, h * HEAD_SLOT + MLA_NOPE:(h + 1) * HEAD_SLOT] = qr.astype(BF16)


def mla_q(proj, q_norm, w_q_ext, tab, *, tm=512):
    m = proj.shape[0]
    tm = _tile(m, tm, SUBLANES)
    n = w_q_ext.shape[1]
    return pl.pallas_call(
        _mla_q_kernel,
        grid=(m // tm,),
        in_specs=[pl.BlockSpec((tm, 1024), lambda i: (i, COL_QA // 1024)),
                  pl.BlockSpec((1, MLA_Q_RANK), lambda i: (0, 0)),
                  pl.BlockSpec((MLA_Q_RANK, n), lambda i: (0, 0)),
                  pl.BlockSpec((tm, LANES), lambda i: (i, 0))],
        out_specs=pl.BlockSpec((tm, MLA_HEADS * HEAD_SLOT), lambda i: (i, 0)),
        out_shape=jax.ShapeDtypeStruct((m, MLA_HEADS * HEAD_SLOT), BF16),
        compiler_params=_params(("parallel",)),
        name="mla_q",
    )(proj, q_norm.reshape(1, -1), w_q_ext, tab)


def _mla_kv_kernel(*refs, with_kv):
    kva_ref, kr_ref, g_ref, w_ref, tab_ref = refs[:5]
    outs = refs[-4:] if with_kv else refs[-2:]
    c_ref, krot_ref = outs[:2]
    c = _rms(kva_ref[...], g_ref[...])
    c_ref[...] = c
    krkr = _rope_pair(kr_ref[...], tab_ref[...])
    krot_ref[...] = krkr[:, :MLA_ROPE]
    if with_kv:
        kfull_ref, v_ref = outs[2:]
        acc = _dot(c, w_ref[...])
        krkr_b = krkr.astype(BF16)
        for h in range(MLA_HEADS):
            kfull_ref[:, h * HEAD_SLOT: h * HEAD_SLOT + MLA_NOPE] = (
                acc[:, h * HEAD_SLOT: h * HEAD_SLOT + MLA_NOPE].astype(BF16))
            kfull_ref[:, h * HEAD_SLOT + MLA_NOPE:(h + 1) * HEAD_SLOT] = krkr_b
            v_ref[h * MLA_V:(h + 1) * MLA_V, :] = acc[:, h * HEAD_SLOT + MLA_NOPE:(h + 1) * HEAD_SLOT].T.astype(BF16)


def mla_kv(proj, row0, n_rows, kv_norm, w_kv_b, tab, layer, lat_prev, kr_prev, *, with_kv, tm=512):
    tm = _tile(math.gcd(n_rows, row0) if row0 else n_rows, tm, SUBLANES)
    rb0 = row0 // tm
    out_specs = [pl.BlockSpec((None, tm, MLA_KV_RANK), lambda i: (layer, i, 0)),
                 pl.BlockSpec((None, tm, MLA_ROPE), lambda i: (layer, i, 0))]
    out_shape = [jax.ShapeDtypeStruct(lat_prev.shape, F32), jax.ShapeDtypeStruct(kr_prev.shape, F32)]
    if with_kv:
        out_specs += [pl.BlockSpec((tm, MLA_HEADS * HEAD_SLOT), lambda i: (i, 0)),
                      pl.BlockSpec((MLA_HEADS * MLA_V, tm), lambda i: (0, i))]
        out_shape += [jax.ShapeDtypeStruct((n_rows, MLA_HEADS * HEAD_SLOT), BF16),
                      jax.ShapeDtypeStruct((MLA_HEADS * MLA_V, n_rows), BF16)]
    return pl.pallas_call(
        functools.partial(_mla_kv_kernel, with_kv=with_kv),
        grid=(n_rows // tm,),
        in_specs=[pl.BlockSpec((tm, MLA_KV_RANK), lambda i: (rb0 + i, COL_KVA // MLA_KV_RANK)),
                  pl.BlockSpec((tm, LANES), lambda i: (rb0 + i, COL_KR // LANES)),
                  pl.BlockSpec((1, MLA_KV_RANK), lambda i: (0, 0)),
                  pl.BlockSpec((MLA_KV_RANK, MLA_HEADS * HEAD_SLOT), lambda i: (0, 0)),
                  pl.BlockSpec((tm, LANES), lambda i: (rb0 + i, 0)),
                  _ANY_SPEC, _ANY_SPEC],
        out_specs=out_specs,
        out_shape=out_shape,
        input_output_aliases={5: 0, 6: 1},
        compiler_params=_params(("parallel",)),
        name="mla_kv",
    )(proj, proj, kv_norm.reshape(1, -1), w_kv_b, tab, lat_prev, kr_prev)


def _flash_kernel(qi_ref, ki_ref, q_ref, k_ref, vt_ref, prev_ref, o_ref, m_ref, l_ref, acc_ref, s_ref, *, tq, tk,
                  hh):
    t = pl.program_id(2)
    qi = qi_ref[t]
    ki = ki_ref[t]
    last_k = ((qi + 1) * tq - 1) // tk

    @pl.when(ki == 0)
    def _():
        m_ref[...] = jnp.full(m_ref.shape, -jnp.inf, F32)
        l_ref[...] = jnp.zeros(l_ref.shape, F32)
        acc_ref[...] = jnp.zeros(acc_ref.shape, F32)

    def update(masked):
        for hd in range(hh):
            s_ref[hd] = lax.dot_general(k_ref[:, hd * HEAD_SLOT:(hd + 1) * HEAD_SLOT],
                                        q_ref[:, hd * HEAD_SLOT:(hd + 1) * HEAD_SLOT], NT_DIMS,
                                        preferred_element_type=F32)
        for hd in range(hh):
            st = s_ref[hd]
            if masked:
                kpos = ki * tk + lax.broadcasted_iota(jnp.int32, (tk, tq), 0)
                qpos = qi * tq + lax.broadcasted_iota(jnp.int32, (tk, tq), 1)
                st = jnp.where(kpos <= qpos, st, -jnp.inf)
            m_prev = m_ref[hd]
            m_new = jnp.maximum(m_prev, jnp.max(st, axis=0, keepdims=True))
            alpha = jnp.exp2(m_prev - m_new)
            p = jnp.exp2(st - m_new)
            l_ref[hd] = alpha * l_ref[hd] + jnp.sum(p, axis=0, keepdims=True)
            acc_ref[hd] = alpha * acc_ref[hd] + jnp.dot(
                vt_ref[hd * MLA_V:(hd + 1) * MLA_V, :], p.astype(BF16), preferred_element_type=F32)
            m_ref[hd] = m_new

    @pl.when((ki + 1) * tk - 1 > qi * tq)
    def _():
        update(True)

    @pl.when((ki + 1) * tk - 1 <= qi * tq)
    def _():
        update(False)

    @pl.when(ki == last_k)
    def _():
        for hd in range(hh):
            o_ref[:, hd * MLA_V:(hd + 1) * MLA_V] = (acc_ref[hd] / l_ref[hd]).T.astype(o_ref.dtype)


def mla_prompt_attention(q_full, k_full, v_t, n_batch, seq, o_prev, *, tq=512, tk=512, hh=4):
    tq, tk = _tile(seq, tq, LANES), _tile(seq, tk, LANES)
    nq, nk = seq // tq, seq // tk
    pairs = [(a, b) for a in range(nq) for b in range(((a + 1) * tq - 1) // tk + 1)]
    qi_tab = jnp.asarray([p[0] for p in pairs], jnp.int32)
    ki_tab = jnp.asarray([p[1] for p in pairs], jnp.int32)
    grid_spec = pltpu.PrefetchScalarGridSpec(
        num_scalar_prefetch=2,
        grid=(n_batch, MLA_HEADS // hh, len(pairs)),
        in_specs=[pl.BlockSpec((tq, hh * HEAD_SLOT), lambda b, h, t, qt, kt: (b * nq + qt[t], h)),
                  pl.BlockSpec((tk, hh * HEAD_SLOT), lambda b, h, t, qt, kt: (b * nk + kt[t], h)),
                  pl.BlockSpec((hh * MLA_V, tk), lambda b, h, t, qt, kt: (h, b * nk + kt[t])),
                  _ANY_SPEC],
        out_specs=pl.BlockSpec((tq, hh * MLA_V), lambda b, h, t, qt, kt: (b * nq + qt[t], h)),
        scratch_shapes=[pltpu.VMEM((hh, 1, tq), F32), pltpu.VMEM((hh, 1, tq), F32),
                        pltpu.VMEM((hh, MLA_V, tq), F32), pltpu.VMEM((hh, tk, tq), F32)],
    )
    return pl.pallas_call(
        functools.partial(_flash_kernel, tq=tq, tk=tk, hh=hh),
        grid_spec=grid_spec,
        out_shape=jax.ShapeDtypeStruct(o_prev.shape, BF16),
        input_output_aliases={5: 0},
        compiler_params=_params(("parallel", "parallel", "arbitrary")),
        name="mla_prompt_flash",
    )(qi_tab, ki_tab, q_full, k_full, v_t, o_prev)


def _absorb_kernel(q_ref, w_ref, o_ref):
    qh = q_ref[...]
    q_lat = _dot_nt(qh[:, :MLA_NOPE], w_ref[...])
    out = jnp.concatenate([q_lat, qh[:, MLA_NOPE:].astype(F32)], axis=-1)
    o_ref[...] = out.reshape(o_ref.shape)


def mla_absorb(q_full, w_kv_b, row0, n_seq, t_len):
    rows = n_seq * t_len
    assert row0 % rows == 0
    width = MLA_KV_RANK + LANES
    return pl.pallas_call(
        _absorb_kernel,
        grid=(MLA_HEADS,),
        in_specs=[pl.BlockSpec((rows, HEAD_SLOT), lambda h: (row0 // rows, h)),
                  pl.BlockSpec((MLA_KV_RANK, MLA_NOPE), lambda h: (0, 2 * h))],
        out_specs=pl.BlockSpec((n_seq, None, t_len, width), lambda h: (0, h, 0, 0)),
        out_shape=jax.ShapeDtypeStruct((n_seq, MLA_HEADS, t_len, width), F32),
        compiler_params=_params(("parallel",)),
        name="mla_absorb",
    )(q_full, w_kv_b)


def _paged_attn_kernel(pt_ref, q_ref, cnew_ref, krnew_ref, clat_hbm, ckr_hbm, o_ref,
                       cf32, krf32, cbuf, m_ref, l_ref, acc_ref, csem, krsem, *, G, layer, t_len, n_steps, n_seq):
    b = pl.program_id(0)
    g = pl.program_id(1)
    step = b * n_steps + g
    slot = step % 2
    page = cbuf.shape[0] // G

    def page_copies(bb, gg, sl):
        copies = []
        for i in range(G):
            pidx = pt_ref[bb, gg * G + i]
            copies.append(pltpu.make_async_copy(
                clat_hbm.at[layer, pidx], cf32.at[sl, pl.ds(i * page, page), :], csem.at[sl]))
            copies.append(pltpu.make_async_copy(
                ckr_hbm.at[layer, pidx], krf32.at[sl, :, pl.ds(i * page, page)], krsem.at[sl]))
        return copies

    @pl.when(step == 0)
    def _():
        for cp in page_copies(b, g, slot):
            cp.start()

    @pl.when(step + 1 < n_seq * n_steps)
    def _():
        wrap = g + 1 == n_steps
        for cp in page_copies(jnp.where(wrap, b + 1, b), jnp.where(wrap, 0, g + 1), 1 - slot):
            cp.start()

    q = q_ref[0]
    q_lat = q[:, :MLA_KV_RANK]
    q_rope = q[:, MLA_KV_RANK:MLA_KV_RANK + MLA_ROPE]

    @pl.when(g == 0)
    def _():
        c_new = cnew_ref[...]
        kr_new = krnew_ref[...]
        tq = lax.broadcasted_iota(jnp.int32, (q.shape[0], 1), 0) % t_len
        cols = []
        for j in range(t_len):
            sj = (jnp.sum(q_lat * c_new[j:j + 1, :], axis=-1, keepdims=True)
                  + jnp.sum(q_rope * kr_new[j:j + 1, :], axis=-1, keepdims=True))
            cols.append(jnp.where(tq >= j, sj, -jnp.inf))
        m0 = cols[0]
        for j in range(1, t_len):
            m0 = jnp.maximum(m0, cols[j])
        l0 = jnp.zeros_like(m0)
        acc0 = jnp.zeros(acc_ref.shape, F32)
        for j in range(t_len):
            pj = jnp.exp2(cols[j] - m0)
            l0 = l0 + pj
            acc0 = acc0 + pj * c_new[j:j + 1, :]
        m_ref[...] = m0
        l_ref[...] = l0
        acc_ref[...] = acc0

    pltpu.make_async_copy(cf32.at[slot], cf32.at[slot], csem.at[slot]).wait()
    pltpu.make_async_copy(krf32.at[slot], krf32.at[slot], krsem.at[slot]).wait()

    q_lat_b = q_lat.astype(BF16)
    q_rope_b = q_rope.astype(BF16)
    s_parts = []
    for i in range(0, G, 2):
        cb = cf32[slot, pl.ds(i * page, 2 * page), :].astype(BF16)
        cbuf[i * page:(i + 2) * page, :] = cb
        krt = krf32[slot, :, pl.ds(i * page, 2 * page)].astype(BF16)
        s_parts.append(lax.dot_general(q_lat_b, cb, NT_DIMS, preferred_element_type=F32)
                       + jnp.dot(q_rope_b, krt, preferred_element_type=F32))
    s = jnp.concatenate(s_parts, axis=1)
    m_prev = m_ref[...]
    m_new = jnp.maximum(m_prev, jnp.max(s, axis=-1, keepdims=True))
    alpha = jnp.exp2(m_prev - m_new)
    p = jnp.exp2(s - m_new)
    l_ref[...] = alpha * l_ref[...] + jnp.sum(p, axis=-1, keepdims=True)
    acc_ref[...] = alpha * acc_ref[...] + jnp.dot(p.astype(BF16), cbuf[...], preferred_element_type=F32)
    m_ref[...] = m_new

    @pl.when(g == n_steps - 1)
    def _():
        o_ref[0] = acc_ref[...] / l_ref[...]


def mla_paged_attention(q_abs, lat_s, kr_s, cache_latent, cache_krope_t, layer, page_table, *, G=32):
    n_seq, heads, t_len, width = q_abs.shape
    n_pages = page_table.shape[1]
    page = cache_latent.shape[2]
    G = _tile(n_pages, G, 2)
    n_steps = n_pages // G
    rows = heads * t_len
    q3 = q_abs.reshape(n_seq, rows, width)
    grid_spec = pltpu.PrefetchScalarGridSpec(
        num_scalar_prefetch=1,
        grid=(n_seq, n_steps),
        in_specs=[pl.BlockSpec((1, rows, width), lambda b, g, pt: (b, 0, 0)),
                  pl.BlockSpec((None, t_len, MLA_KV_RANK), lambda b, g, pt: (layer, b, 0)),
                  pl.BlockSpec((None, t_len, MLA_ROPE), lambda b, g, pt: (layer, b, 0)),
                  _ANY_SPEC, _ANY_SPEC],
        out_specs=pl.BlockSpec((1, rows, MLA_KV_RANK), lambda b, g, pt: (b, 0, 0)),
        scratch_shapes=[pltpu.VMEM((2, G * page, MLA_KV_RANK), F32),
                        pltpu.VMEM((2, MLA_ROPE, G * page), F32),
                        pltpu.VMEM((G * page, MLA_KV_RANK), BF16),
                        pltpu.VMEM((rows, 1), F32), pltpu.VMEM((rows, 1), F32),
                        pltpu.VMEM((rows, MLA_KV_RANK), F32),
                        pltpu.SemaphoreType.DMA((2,)), pltpu.SemaphoreType.DMA((2,))],
    )
    return pl.pallas_call(
        functools.partial(_paged_attn_kernel, G=G, layer=layer, t_len=t_len, n_steps=n_steps, n_seq=n_seq),
        grid_spec=grid_spec,
        out_shape=jax.ShapeDtypeStruct((n_seq, rows, MLA_KV_RANK), F32),
        compiler_params=_params(("arbitrary", "arbitrary")),
        name="mla_paged_attention",
    )(page_table, q3, lat_s, kr_s, cache_latent, cache_krope_t)


def _uv_kernel(o_ref, w_ref, prev_ref, out_ref):
    o = o_ref[...]
    out_ref[...] = _dot(o.reshape(o.shape[0] * o.shape[1], o.shape[2]), w_ref[...]).astype(out_ref.dtype)


def mla_uv(o_lat, w_kv_b, row0, n_seq, t_len, ob_prev):
    o4 = o_lat.reshape(n_seq, MLA_HEADS, t_len, MLA_KV_RANK)
    rows = n_seq * t_len
    assert row0 % rows == 0
    return pl.pallas_call(
        _uv_kernel,
        grid=(MLA_HEADS,),
        in_specs=[pl.BlockSpec((n_seq, None, t_len, MLA_KV_RANK), lambda h: (0, h, 0, 0)),
                  pl.BlockSpec((MLA_KV_RANK, MLA_V), lambda h: (0, 2 * h + 1)),
                  _ANY_SPEC],
        out_specs=pl.BlockSpec((rows, MLA_V), lambda h: (row0 // rows, h)),
        out_shape=jax.ShapeDtypeStruct(ob_prev.shape, BF16),
        input_output_aliases={2: 0},
        compiler_params=_params(("parallel",)),
        name="mla_uv",
    )(o4, w_kv_b, ob_prev)


def _gmlp_kernel(*refs, emit_vn):
    gu_ref, gv_ref, g_ref, w_ref, bias_ref = refs[:5]
    o_ref = refs[-2] if emit_vn else refs[-1]
    vn = _rms(_gelu(gv_ref[...]), g_ref[...])
    if emit_vn:
        refs[-1][...] = vn
    u = _gelu(gu_ref[...])
    for grp in range(CM_GROUPS):
        sl = slice(grp * LANES, (grp + 1) * LANES)
        z = _dot(w_ref[grp], vn[:, sl]) + bias_ref[:, sl]
        o_ref[:, sl] = (u[:, sl] * z).astype(o_ref.dtype)


def gmlp(proj, row0, n_rows, v_norm, w_mix, bias_full, oc_prev, *, emit_vn):
    width = CM_GROUPS * LANES
    assert row0 % CM_CHUNK == 0 and n_rows % CM_CHUNK == 0
    rb0 = row0 // CM_CHUNK
    in_specs = [pl.BlockSpec((CM_CHUNK, width), lambda i: (rb0 + i, COL_GU // width)),
                pl.BlockSpec((CM_CHUNK, width), lambda i: (rb0 + i, COL_GV // width)),
                pl.BlockSpec((1, width), lambda i: (0, 0)),
                pl.BlockSpec((CM_GROUPS, CM_CHUNK, CM_CHUNK), lambda i: (0, 0, 0)),
                pl.BlockSpec((CM_CHUNK, width), lambda i: (0, 0))]
    args = [proj, proj, v_norm.reshape(1, -1), w_mix, bias_full]
    aliases = {}
    if oc_prev is not None:
        aliases[len(args)] = 0
        in_specs.append(_ANY_SPEC)
        args.append(oc_prev)
    out_specs = [pl.BlockSpec((CM_CHUNK, width), lambda i: (rb0 + i, 0))]
    out_shape = [jax.ShapeDtypeStruct((proj.shape[0], width), BF16)]
    if emit_vn:
        out_specs.append(pl.BlockSpec((CM_CHUNK, width), lambda i: (i, 0)))
        out_shape.append(jax.ShapeDtypeStruct((n_rows, width), F32))
    return pl.pallas_call(
        functools.partial(_gmlp_kernel, emit_vn=emit_vn),
        grid=(n_rows // CM_CHUNK,),
        in_specs=in_specs,
        out_specs=out_specs,
        out_shape=out_shape,
        input_output_aliases=aliases,
        compiler_params=_params(("parallel",)),
        name="gmlp",
    )(*args)


def _merge_kernel(oa_ref, ob_ref, oc_ref, g0_ref, g1_ref, g2_ref, w_ref, o_ref):
    acc = None
    for br, (b_ref, g_ref) in enumerate(((oa_ref, g0_ref), (ob_ref, g1_ref), (oc_ref, g2_ref))):
        part = g_ref[...] * jnp.dot(b_ref[...], w_ref[br], preferred_element_type=F32)
        acc = part if acc is None else acc + part
    o_ref[...] = acc.astype(o_ref.dtype)


def branch_merge(o_a, o_b, o_c, gates, w_branch, *, tm=1024, tn=512):
    m, bw = o_a.shape
    d = w_branch.shape[2]
    tm, tn = _tile(m, tm, SUBLANES), _tile(d, tn, LANES)
    nt = d // tn
    bspec = pl.BlockSpec((tm, bw), lambda i, j: (i, 0))

    def gspec(br):
        return pl.BlockSpec((tm, tn), lambda i, j: (i, br * nt + j))

    return pl.pallas_call(
        _merge_kernel,
        grid=(m // tm, nt),
        in_specs=[bspec, bspec, bspec, gspec(0), gspec(1), gspec(2),
                  pl.BlockSpec((N_BRANCH, bw, tn), lambda i, j: (0, 0, j))],
        out_specs=pl.BlockSpec((tm, tn), lambda i, j: (i, j)),
        out_shape=jax.ShapeDtypeStruct((m, d), BF16),
        compiler_params=_params(("parallel", "arbitrary")),
        name="branch_merge",
    )(o_a, o_b, o_c, gates, gates, gates, w_branch)


def _out_proj_kernel(mix_ref, x_ref, w_ref, g_ref, o_ref):
    y = jnp.dot(mix_ref[...], w_ref[...], preferred_element_type=F32)
    o_ref[...] = x_ref[...] + _rms(y, g_ref[...])


def out_proj(mix, x, w_out, g, *, tm=512):
    m, d = x.shape
    tm = _tile(m, tm, SUBLANES)
    return pl.pallas_call(
        _out_proj_kernel,
        grid=(m // tm,),
        in_specs=[pl.BlockSpec((tm, d), lambda i: (i, 0)),
                  pl.BlockSpec((tm, d), lambda i: (i, 0)),
                  pl.BlockSpec((d, d), lambda i: (0, 0)),
                  pl.BlockSpec((1, d), lambda i: (0, 0))],
        out_specs=pl.BlockSpec((tm, d), lambda i: (i, 0)),
        out_shape=jax.ShapeDtypeStruct((m, d), F32),
        compiler_params=_params(("parallel",)),
        name="out_proj",
    )(mix, x, w_out, g.reshape(1, -1))


def _ffn_up_kernel(*refs, t_len, per_seq_prev, tiles_per_seq, kc, rb):
    x_ref, g_ref, wg_ref, wv_ref, cw_ref, cb_ref = refs[:6]
    if per_seq_prev:
        pg_ref, pv_ref = refs[6:8]
        act_ref, tailg_ref, tailv_ref, h_ref = refs[-4:]
    else:
        act_ref, tailg_ref, tailv_ref, h_ref, carry_ref, shift_ref = refs[-6:]
    i = pl.program_id(0)
    j = pl.program_id(1)
    tm, tn = act_ref.shape

    @pl.when(j == 0)
    def _():
        h_ref[...] = _rms(x_ref[...], g_ref[...]).astype(BF16)

    if not per_seq_prev:
        @pl.when(i == 0)
        def _():
            carry_ref[j] = jnp.zeros(carry_ref.shape[1:], F32)

        seq_start = (i % tiles_per_seq) == 0

    def conv(u, which, sl, p):
        if per_seq_prev:
            u = u.reshape(rb // t_len, t_len, u.shape[-1])
            tpos = lax.broadcasted_iota(jnp.int32, (1, t_len, 1), 1)
            back1 = p[:, 1:2, :]
            back2 = jnp.where(tpos == 0, p[:, 0:1, :], p[:, 1:2, :])
            axis = 1
            m1 = jnp.where(tpos == 0, back1, pltpu.roll(u, 1, axis))
            m2 = jnp.where(tpos <= 1, back2, pltpu.roll(u, 2, axis))
        else:
            shift_ref[which, 0:SUBLANES, :] = p
            shift_ref[which, SUBLANES:SUBLANES + rb, :] = u
            m1 = shift_ref[which, SUBLANES - 1:SUBLANES - 1 + rb, :]
            m2 = shift_ref[which, SUBLANES - 2:SUBLANES - 2 + rb, :]
        y = (cb_ref[which:which + 1, sl] + cw_ref[which, 0:1, sl] * m2 + cw_ref[which, 1:2, sl] * m1
             + cw_ref[which, 2:3, sl] * u)
        return y.reshape(rb, y.shape[-1])

    tails = []
    for c0 in range(0, tn, kc):
        sl = slice(c0, c0 + kc)
        wg = wg_ref[:, sl].astype(BF16)
        wv = wv_ref[:, sl].astype(BF16)
        if not per_seq_prev:
            pg = jnp.where(seq_start, 0.0, carry_ref[j, 0][:, sl])
            pv = jnp.where(seq_start, 0.0, carry_ref[j, 1][:, sl])
        for r0 in range(0, tm, rb):
            hr = h_ref[r0:r0 + rb, :]
            ug = jnp.dot(hr, wg, preferred_element_type=F32)
            uv = jnp.dot(hr, wv, preferred_element_type=F32)
            if per_seq_prev:
                s0, s1 = r0 // t_len, (r0 + rb) // t_len
                pg = pg_ref[s0:s1, :, sl]
                pv = pv_ref[s0:s1, :, sl]
            act = _gelu(conv(ug, 0, sl, pg)) * conv(uv, 1, sl, pv)
            act_ref[r0:r0 + rb, sl] = act.astype(act_ref.dtype)
            if per_seq_prev:
                tailg_ref[s0:s1, :, sl] = ug.reshape(rb // t_len, t_len, kc)[:, t_len - (CONV_W - 1):, :]
                tailv_ref[s0:s1, :, sl] = uv.reshape(rb // t_len, t_len, kc)[:, t_len - (CONV_W - 1):, :]
            else:
                pg, pv = ug[rb - SUBLANES:, :], uv[rb - SUBLANES:, :]
        if not per_seq_prev:
            tailg_ref[:, sl] = pg
            tailv_ref[:, sl] = pv
            tails.append((sl, pg, pv))
    for sl, tg, tv in tails:
        carry_ref[j, 0, :, sl] = tg
        carry_ref[j, 1, :, sl] = tv


def ffn_up(x, g, w_up_all, layer, conv_w2, conv_b2, prev_state, row0, n_rows, t_len, act_prev, *, tm=1024,
           tn=512, kc=256, rb=512):
    d = x.shape[1]
    d_ff = w_up_all.shape[2] // 2
    per_seq_prev = t_len == SUBLANES
    tm = _tile(n_rows if per_seq_prev else t_len, tm, SUBLANES)
    tn = _tile(d_ff, tn, LANES)
    kc = _tile(tn, kc, LANES)
    rb = _tile(tm, rb, t_len if per_seq_prev else SUBLANES)
    n_j = d_ff // tn
    n_i = n_rows // tm
    assert row0 % tm == 0 and (tm % t_len == 0 or t_len % tm == 0)
    rb0 = row0 // tm
    in_specs = [pl.BlockSpec((tm, d), lambda i, j: (rb0 + i, 0)),
                pl.BlockSpec((1, d), lambda i, j: (0, 0)),
                pl.BlockSpec((None, d, tn), lambda i, j: (layer, 0, j)),
                pl.BlockSpec((None, d, tn), lambda i, j: (layer, 0, n_j + j)),
                pl.BlockSpec((2, CONV_W, tn), lambda i, j: (0, 0, j)),
                pl.BlockSpec((2, tn), lambda i, j: (0, j))]
    args = [x, g.reshape(1, -1), w_up_all, w_up_all, conv_w2, conv_b2]
    scratch = [pltpu.VMEM((tm, d), BF16)]
    if per_seq_prev:
        assert t_len == SUBLANES
        ns = tm // t_len
        in_specs += [pl.BlockSpec((None, ns, CONV_W - 1, tn), lambda i, j: (layer, i, 0, j)),
                     pl.BlockSpec((None, ns, CONV_W - 1, tn), lambda i, j: (layer, i, 0, n_j + j))]
        args += [prev_state, prev_state]
        tail_spec = pl.BlockSpec((ns, CONV_W - 1, tn), lambda i, j: (i, 0, j))
        tail_shape = jax.ShapeDtypeStruct((n_rows // t_len, CONV_W - 1, d_ff), F32)
    else:
        scratch.append(pltpu.VMEM((n_j, 2, SUBLANES, tn), F32))
        scratch.append(pltpu.VMEM((2, SUBLANES + rb, kc), F32))
        tail_spec = pl.BlockSpec((None, SUBLANES, tn), lambda i, j: (i, 0, j))
        tail_shape = jax.ShapeDtypeStruct((n_i, SUBLANES, d_ff), F32)
    aliases = {}
    if act_prev is not None:
        aliases[len(args)] = 0
        in_specs.append(_ANY_SPEC)
        args.append(act_prev)
    return pl.pallas_call(
        functools.partial(_ffn_up_kernel, t_len=t_len, per_seq_prev=per_seq_prev,
                          tiles_per_seq=max(t_len // tm, 1), kc=kc, rb=rb),
        grid=(n_i, n_j),
        in_specs=in_specs,
        out_specs=[pl.BlockSpec((tm, tn), lambda i, j: (rb0 + i, j)), tail_spec, tail_spec],
        out_shape=[jax.ShapeDtypeStruct((x.shape[0], d_ff), BF16), tail_shape, tail_shape],
        scratch_shapes=scratch,
        input_output_aliases=aliases,
        compiler_params=_params(("arbitrary", "arbitrary")),
        name="ffn_up",
    )(*args)


def _ffn_down_kernel(a_ref, w_ref, x_ref, g_ref, o_ref, acc_ref, *, n_k):
    kk = pl.program_id(1)
    part = jnp.dot(a_ref[...], w_ref[...], preferred_element_type=F32)
    if n_k == 1:
        o_ref[...] = x_ref[...] + _rms(part, g_ref[...])
        return

    @pl.when(kk == 0)
    def _():
        acc_ref[...] = part

    @pl.when((kk > 0) & (kk < n_k - 1))
    def _():
        acc_ref[...] += part

    @pl.when(kk == n_k - 1)
    def _():
        o_ref[...] = x_ref[...] + _rms(acc_ref[...] + part, g_ref[...])


def ffn_down(act, w_down, x, g, *, tm=512, tk=2816):
    d_ff, d = w_down.shape
    m = x.shape[0]
    tm = _tile(m, tm, SUBLANES)
    tk = _tile(d_ff, tk, LANES)
    n_k = d_ff // tk
    return pl.pallas_call(
        functools.partial(_ffn_down_kernel, n_k=n_k),
        grid=(m // tm, n_k),
        in_specs=[pl.BlockSpec((tm, tk), lambda i, k: (i, k)),
                  pl.BlockSpec((tk, d), lambda i, k: (k, 0)),
                  pl.BlockSpec((tm, d), lambda i, k: (i, 0)),
                  pl.BlockSpec((1, d), lambda i, k: (0, 0))],
        out_specs=pl.BlockSpec((tm, d), lambda i, k: (i, 0)),
        out_shape=jax.ShapeDtypeStruct(x.shape, F32),
        scratch_shapes=[pltpu.VMEM((tm, d), F32)],
        compiler_params=_params(("parallel", "arbitrary")),
        name="ffn_down",
    )(act, w_down, x, g.reshape(1, -1))


def _ple_kernel(x_ref, pe_ref, g_ref, wg_ref, wp_ref, o_ref):
    x = x_ref[...]
    gate = _sigmoid(_dot(_rms(x, g_ref[...]), wg_ref[...]))
    o_ref[...] = x + _dot(pe_ref[...], wp_ref[...]) * gate


def ple(x, pe, g, w_gate, w_proj, *, tm=512):
    m, d = x.shape
    pd = pe.shape[1]
    tm = _tile(m, tm, SUBLANES)
    return pl.pallas_call(
        _ple_kernel,
        grid=(m // tm,),
        in_specs=[pl.BlockSpec((tm, d), lambda i: (i, 0)),
                  pl.BlockSpec((tm, pd), lambda i: (i, 0)),
                  pl.BlockSpec((1, d), lambda i: (0, 0)),
                  pl.BlockSpec((d, d), lambda i: (0, 0)),
                  pl.BlockSpec((pd, d), lambda i: (0, 0))],
        out_specs=pl.BlockSpec((tm, d), lambda i: (i, 0)),
        out_shape=jax.ShapeDtypeStruct((m, d), F32),
        compiler_params=_params(("parallel",)),
        name="ple",
    )(x, pe, g.reshape(1, -1), w_gate, w_proj)


def _rope_table(pos):
    inv = 1.0 / (ROPE_THETA ** (jnp.arange(0, MLA_ROPE, 2, dtype=F32) / MLA_ROPE))
    ang = pos.astype(F32)[:, None] * inv[None, :]
    cos, sin = jnp.cos(ang), jnp.sin(ang)
    return jnp.concatenate([cos, cos, -sin, sin], axis=-1)


def _swap_halves(w, width):
    half = width // 2
    return jnp.concatenate([w[..., half:], w[..., :half]], axis=-1)


def kernel(x_prompt, x_sample, cache_latent, cache_krope, state_hgrn, state_ffn_conv, page_table, p_prompt, p_sample, ln_mix_pre, ln_mix_post, ln_ffn_pre, ln_ffn_post, ln_ple, w_in, hgrn_lower_bounds, hgrn_norm, mla_q_norm, mla_kv_norm, w_q_b, w_kv_b, cm_v_norm, cm_w_s, cm_b, w_merge_gate, w_branch, w_out, w_ffn_up, ffn_conv_w, ffn_conv_b, w_ffn_down, w_ple_proj, w_ple_gate):
    n_b, seq, d = x_prompt.shape
    n_db, t_len, _ = x_sample.shape
    depth = w_in.shape[0]
    d_ff = w_ffn_down.shape[1]
    page = cache_latent.shape[2]
    past_len = page_table.shape[1] * page
    rows_p, rows_s = n_b * seq, n_db * t_len
    rows = rows_p + rows_s
    hk = HG_HEADS * HG_KEY

    lb_soft = jax.nn.softmax(hgrn_lower_bounds.astype(F32), axis=0)
    lbs = jnp.concatenate([jnp.zeros_like(lb_soft[:1]), jnp.cumsum(lb_soft[1:], axis=0)], axis=0)

    tab = jnp.concatenate([jnp.tile(_rope_table(jnp.arange(seq)), (n_b, 1)),
                           jnp.tile(_rope_table(past_len + jnp.arange(t_len)), (n_db, 1))], axis=0)
    cache_krope_t = jnp.swapaxes(cache_krope, 2, 3)

    x = jnp.concatenate([x_prompt.reshape(rows_p, d), x_sample.reshape(rows_s, d)], axis=0)
    tri = jnp.tril(jnp.ones((CM_CHUNK, CM_CHUNK), F32))
    seq_per_chunk = CM_CHUNK // t_len

    lat_p = jnp.zeros((depth, rows_p, MLA_KV_RANK), F32)
    kr_p = jnp.zeros((depth, rows_p, MLA_ROPE), F32)
    lat_s = jnp.zeros((depth, rows_s, MLA_KV_RANK), F32)
    kr_s = jnp.zeros((depth, rows_s, MLA_ROPE), F32)
    hg_p = jnp.zeros((depth, n_b, HG_HEADS, HG_KEY, HG_VAL), F32)
    hg_s = jnp.zeros((depth, n_db, HG_HEADS, HG_KEY, HG_VAL), F32)
    cv_p, cv_s, v_s = [], [], []

    for i in range(depth):
        wi = w_in[i]
        o_kr = 4 * hk + MLA_Q_RANK + MLA_KV_RANK
        kr_cols = wi[:, o_kr: o_kr + MLA_ROPE]
        w_in_p = jnp.concatenate(
            [wi[:, :4 * hk], wi[:, o_kr + MLA_ROPE:], wi[:, 4 * hk: 4 * hk + MLA_Q_RANK], kr_cols,
             _swap_halves(kr_cols, MLA_ROPE), jnp.zeros((d, LANES), F32),
             wi[:, 4 * hk + MLA_Q_RANK: o_kr]], axis=1).astype(BF16)
        wq = w_q_b[i].reshape(MLA_Q_RANK, MLA_HEADS, MLA_NOPE + MLA_ROPE)
        wq_rope = wq[..., MLA_NOPE:]
        w_q_ext = jnp.concatenate(
            [wq[..., :MLA_NOPE].reshape(MLA_Q_RANK, -1),
             jnp.concatenate([wq_rope, _swap_halves(wq_rope, MLA_ROPE)], axis=-1).reshape(MLA_Q_RANK, -1)],
            axis=1).astype(BF16)
        w_kvb = w_kv_b[i].astype(BF16)
        lb = lbs[i]
        lb3 = jnp.stack([lb, jnp.log(lb), jnp.log1p(-lb)], axis=0)
        w_mix_p = cm_w_s[i] * tri
        w_mix_s = jnp.einsum("ab,gts->gatbs", jnp.eye(seq_per_chunk, dtype=F32),
                             w_mix_p[:, :t_len, :t_len]).reshape(CM_GROUPS, CM_CHUNK, CM_CHUNK)
        bias_p = jnp.repeat(cm_b[i].T, LANES, axis=1)
        bias_s = jnp.tile(bias_p[:t_len], (seq_per_chunk, 1))
        conv_w2 = ffn_conv_w[i].reshape(CONV_W, 2, d_ff).transpose(1, 0, 2)
        conv_b2 = ffn_conv_b[i].reshape(2, d_ff)

        zeros_branch = jnp.zeros((rows, HG_HEADS * HG_VAL), BF16)
        proj = norm_matmul(x, ln_mix_pre[i], w_in_p, tn=1536)
        gates = norm_matmul(x, ln_mix_pre[i], w_merge_gate, i, act="sigmoid", tn=1024)

        o_a, hg_p = hgrn2(proj, 0, n_b, seq, lb3, hgrn_norm[i], None, i, zeros_branch, hg_p,
                          C=_tile(seq, 128, SUBLANES), nb=1, hh=4)
        o_a, hg_s = hgrn2(proj, rows_p, n_db, t_len, lb3, hgrn_norm[i], state_hgrn, i, o_a, hg_s,
                          C=t_len, nb=_tile(n_db, LANES // t_len), hh=2)

        q_full = mla_q(proj, mla_q_norm[i], w_q_ext, tab)
        lat_p, kr_p, k_full, v_all = mla_kv(proj, 0, rows_p, mla_kv_norm[i], w_kvb, tab, i, lat_p, kr_p,
                                            with_kv=True)
        lat_s, kr_s = mla_kv(proj, rows_p, rows_s, mla_kv_norm[i], w_kvb, tab, i, lat_s, kr_s, with_kv=False)
        o_b = mla_prompt_attention(q_full, k_full, v_all, n_b, seq, zeros_branch)
        q_abs = mla_absorb(q_full, w_kvb, rows_p, n_db, t_len)
        o_lat = mla_paged_attention(q_abs, lat_s, kr_s, cache_latent, cache_krope_t, i, page_table)
        o_b = mla_uv(o_lat, w_kvb, rows_p, n_db, t_len, o_b)

        (o_c,) = gmlp(proj, 0, rows_p, cm_v_norm[i], w_mix_p, bias_p, zeros_branch, emit_vn=False)
        o_c, vn_s = gmlp(proj, rows_p, rows_s, cm_v_norm[i], w_mix_s, bias_s, o_c, emit_vn=True)

        mix = branch_merge(o_a, o_b, o_c, gates, w_branch[i].astype(BF16))
        x = out_proj(mix, x, w_out[i].astype(BF16), ln_mix_post[i])

        act, tg_p, tv_p = ffn_up(x, ln_ffn_pre[i], w_ffn_up, i, conv_w2, conv_b2, None, 0, rows_p, seq,
                                 jnp.zeros((rows, d_ff), BF16))
        act, tg_s, tv_s = ffn_up(x, ln_ffn_pre[i], w_ffn_up, i, conv_w2, conv_b2, state_ffn_conv, rows_p, rows_s,
                                 t_len, act)
        x = ffn_down(act, w_ffn_down[i].astype(BF16), x, ln_ffn_post[i])

        pe = jnp.concatenate([p_prompt[i].reshape(rows_p, -1), p_sample[i].reshape(rows_s, -1)], axis=0)
        x = ple(x, pe, ln_ple[i], w_ple_gate[i].astype(BF16), w_ple_proj[i].astype(BF16))

        tails_p = jnp.concatenate([tg_p, tv_p], axis=-1).reshape(n_b, -1, SUBLANES, 2 * d_ff)
        cv_p.append(tails_p[:, -1, SUBLANES - (CONV_W - 1):, :])
        cv_s.append(jnp.concatenate([tg_s, tv_s], axis=-1))
        v_s.append(vn_s.reshape(n_db, t_len, -1))

    return (x[:rows_p].reshape(n_b, seq, d), x[rows_p:].reshape(n_db, t_len, d),
            lat_p.reshape(depth, n_b, seq, -1), kr_p.reshape(depth, n_b, seq, -1),
            lat_s.reshape(depth, n_db, t_len, -1), kr_s.reshape(depth, n_db, t_len, -1),
            hg_p, hg_s, jnp.stack(cv_p), jnp.stack(cv_s), jnp.stack(v_s))
```

```python
import functools
import math

import jax
import jax.numpy as jnp
import numpy as np
from jax import lax
from jax.experimental import pallas as pl
from jax.experimental.pallas import tpu as pltpu

F32 = jnp.float32
BF16 = jnp.bfloat16
HIGHEST = lax.Precision.HIGHEST

HG_HEADS = 8
HG_KEY = 128
HG_VAL = 128
MLA_HEADS = 8
MLA_NOPE = 128
MLA_ROPE = 64
MLA_V = 128
MLA_Q_RANK = 768
MLA_KV_RANK = 512
MLA_SCALE = (MLA_NOPE + MLA_ROPE) ** -0.5
ROPE_THETA = 10000.0
CM_GROUPS = 8
CM_CHUNK = 128
N_BRANCH = 3
CONV_W = 3
EPS = 1e-6
LOG2E = math.log2(math.e)

LANES = 128
SUBLANES = 8
VMEM_LIMIT = 56 * 1024 * 1024
HEAD_SLOT = 2 * LANES

IN_TILE = 1024
IN_TILE_SRC = (0, 1024, 2048, 3072, 5440, 6464, 4096, 4864)
COL_GU, COL_GV = 4096, 5120
COL_QA = 6144
COL_KVA = 7168
COL_KR = 7680

NT_DIMS = (((1,), (1,)), ((), ()))
TN_DIMS = (((0,), (0,)), ((), ()))


def _params(sem):
    return pltpu.CompilerParams(dimension_semantics=sem, vmem_limit_bytes=VMEM_LIMIT)


def _rms(x, g):
    ms = jnp.mean(x * x, axis=-1, keepdims=True)
    return x * lax.rsqrt(ms + EPS) * g


def _gelu(x):
    return 0.5 * x * (1.0 + jnp.tanh(math.sqrt(2.0 / math.pi) * (x + 0.044715 * (x * x * x))))


def _sigmoid(x):
    return 0.5 * jnp.tanh(0.5 * x) + 0.5


def _silu(x):
    return x * _sigmoid(x)


def _tile(n, pref, unit=1):
    t = (min(pref, n) // unit) * unit
    while t > unit and n % t:
        t -= unit
    assert t > 0 and n % t == 0, (n, pref, unit)
    return t


def _dot(a, b):
    return jnp.dot(a.astype(BF16), b.astype(BF16), preferred_element_type=F32)


def _dot_nt(a, b):
    return lax.dot_general(a.astype(BF16), b.astype(BF16), NT_DIMS, preferred_element_type=F32)


_ANY_SPEC = pl.BlockSpec(memory_space=pl.ANY)


def _norm_matmul_kernel(x_ref, g_ref, w_ref, o_ref, h_ref, *, act):
    @pl.when(pl.program_id(1) == 0)
    def _():
        h_ref[...] = _rms(x_ref[...], g_ref[...]).astype(BF16)

    acc = jnp.dot(h_ref[...], w_ref[...].astype(BF16), preferred_element_type=F32)
    if act == "sigmoid":
        acc = _sigmoid(acc)
    o_ref[...] = acc.astype(o_ref.dtype)


def norm_matmul(x, g, w, layer=None, *, act=None, out_dtype=F32, tm=1024, tn=512):
    m, k = x.shape
    n = w.shape[-1]
    tm, tn = _tile(m, tm, SUBLANES), _tile(n, tn, LANES)
    if layer is None:
        w_spec = pl.BlockSpec((k, tn), lambda i, j: (0, j))
    else:
        w_spec = pl.BlockSpec((None, k, tn), lambda i, j: (layer, 0, j))
    return pl.pallas_call(
        functools.partial(_norm_matmul_kernel, act=act),
        grid=(m // tm, n // tn),
        in_specs=[
            pl.BlockSpec((tm, k), lambda i, j: (i, 0)),
            pl.BlockSpec((1, k), lambda i, j: (0, 0)),
            w_spec,
        ],
        out_specs=pl.BlockSpec((tm, tn), lambda i, j: (i, j)),
        out_shape=jax.ShapeDtypeStruct((m, n), out_dtype),
        scratch_shapes=[pltpu.VMEM((tm, k), BF16)],
        compiler_params=_params(("parallel", "arbitrary")),
        name="norm_matmul",
    )(x, g.reshape(1, k), w)


def _norm_matmul_t_kernel(off_ref, x_ref, g_ref, wt_ref, o_ref, h_ref):
    @pl.when(pl.program_id(1) == 0)
    def _():
        h_ref[...] = _rms(x_ref[...], g_ref[...]).astype(BF16)

    o_ref[...] = lax.dot_general(h_ref[...], wt_ref[0].astype(BF16), NT_DIMS, preferred_element_type=F32)


def norm_matmul_t(x, g, w_t, layer, col_offsets, *, tm=1024, tn=1024):
    m, k = x.shape
    tm = _tile(m, tm, SUBLANES)
    n_tiles = len(col_offsets)
    assert all(o % SUBLANES == 0 and o + tn <= w_t.shape[1] for o in col_offsets)
    grid_spec = pltpu.PrefetchScalarGridSpec(
        num_scalar_prefetch=1,
        grid=(m // tm, n_tiles),
        in_specs=[pl.BlockSpec((tm, k), lambda i, j, off: (i, 0)),
                  pl.BlockSpec((1, k), lambda i, j, off: (0, 0)),
                  pl.BlockSpec((pl.Element(1), pl.Element(tn), pl.Element(k)),
                               lambda i, j, off: (layer, pl.multiple_of(off[j], SUBLANES), 0))],
        out_specs=pl.BlockSpec((tm, tn), lambda i, j, off: (i, j)),
        scratch_shapes=[pltpu.VMEM((tm, k), BF16)],
    )
    return pl.pallas_call(
        _norm_matmul_t_kernel,
        grid_spec=grid_spec,
        out_shape=jax.ShapeDtypeStruct((m, n_tiles * tn), F32),
        compiler_params=_params(("parallel", "arbitrary")),
        name="norm_matmul_t",
    )(jnp.asarray(col_offsets, jnp.int32), x, g.reshape(1, k), w_t)


def _hgrn_head(hq, hf, v, hg, lb3, ng, st_ref, hd, *, C, nb):
    R = nb * C
    lb = lb3[0:1, :]
    log_lb = lb3[1:2, :]
    log_1m_lb = lb3[2:3, :]
    q = _silu(hq)
    log_sig = jnp.minimum(hf, 0.0) - jnp.log(1.0 + jnp.exp(-jnp.abs(hf)))
    c2 = log_1m_lb + log_sig
    logf = jnp.maximum(log_lb, c2) + jnp.log(1.0 + jnp.exp(-jnp.abs(log_lb - c2)))
    k = (1.0 - lb) * (1.0 / (1.0 + jnp.exp(hf)))

    row = lax.broadcasted_iota(jnp.int32, (R, R), 0)
    col = lax.broadcasted_iota(jnp.int32, (R, R), 1)
    tri = jnp.where((col <= row) & ((row // C) == (col // C)), 1.0, 0.0).astype(F32)
    b = jnp.dot(tri, logf, precision=HIGHEST, preferred_element_type=F32)

    G = R // SUBLANES
    q3 = q.reshape(G, SUBLANES, HG_KEY)
    k3 = k.reshape(G, SUBLANES, HG_KEY)
    b3 = b.reshape(G, SUBLANES, HG_KEY)
    v3 = v.reshape(G, SUBLANES, HG_VAL)
    tpos = lax.broadcasted_iota(jnp.int32, (G, SUBLANES, 1), 1)
    o3 = jnp.zeros((G, SUBLANES, HG_VAL), F32)
    for s in range(SUBLANES):
        dec = jnp.exp(b3 - b3[:, s:s + 1, :])
        a_s = jnp.sum(q3 * k3[:, s:s + 1, :] * dec, axis=-1, keepdims=True)
        o3 = o3 + jnp.where(tpos >= s, a_s, 0.0) * v3[:, s:s + 1, :]
    o = o3.reshape(R, HG_VAL)

    half = C // 2
    a_off = None
    rrow = lax.broadcasted_iota(jnp.int32, (R, 1), 0)
    while half >= SUBLANES:
        blk = 2 * half
        nblk = R // blk
        b4 = b.reshape(nblk, blk, HG_KEY)
        ref_b = jnp.broadcast_to(b4[:, half - 1:half, :], (nblk, blk, HG_KEY)).reshape(R, HG_KEY)
        second = (rrow % blk) >= half
        qe = jnp.where(second, q * jnp.exp(b - ref_b), 0.0)
        ke = jnp.where(second, 0.0, k * jnp.exp(ref_b - b))
        a_l = _dot_nt(qe, ke)
        if blk != R:
            a_l = jnp.where((row // blk) == (col // blk), a_l, 0.0)
        a_off = a_l if a_off is None else a_off + a_l
        half //= 2
    if a_off is not None:
        o = o + _dot(a_off, v)

    qb = q * jnp.exp(b)
    o_parts = []
    for n in range(nb):
        lo, hi = n * C, (n + 1) * C
        st = st_ref[n, hd]
        o_parts.append(_dot_nt(qb[lo:hi], st))
        b_last = b[hi - 1:hi, :]
        kd = k[lo:hi] * jnp.exp(b_last - b[lo:hi])
        st_ref[n, hd] = st * jnp.exp(b_last) + lax.dot_general(
            v[lo:hi].astype(BF16), kd.astype(BF16), TN_DIMS, preferred_element_type=F32)
    o = o + (o_parts[0] if nb == 1 else jnp.concatenate(o_parts, axis=0))
    return _rms(o, ng) * _silu(hg)


def _hgrn_kernel(*refs, C, nb, hh, n_chunks, has_s0):
    hq_ref, hf_ref, hi_ref, hg_ref, lb_ref, ng_ref = refs[:6]
    s0_ref = refs[6] if has_s0 else None
    o_ref, snew_ref, st_ref = refs[-3:]
    ci = pl.program_id(2)

    @pl.when(ci == 0)
    def _():
        for n in range(nb):
            for hd in range(hh):
                st_ref[n, hd] = s0_ref[n, hd].T if has_s0 else jnp.zeros((HG_VAL, HG_KEY), F32)

    for hd in range(hh):
        sl = slice(hd * HG_KEY, (hd + 1) * HG_KEY)
        o = _hgrn_head(hq_ref[:, sl], hf_ref[:, sl], hi_ref[:, sl], hg_ref[:, sl], lb_ref[:, sl], ng_ref[:, sl],
                       st_ref, hd, C=C, nb=nb)
        o_ref[:, sl] = o.astype(o_ref.dtype)

    @pl.when(ci == n_chunks - 1)
    def _():
        for n in range(nb):
            for hd in range(hh):
                snew_ref[n, hd] = st_ref[n, hd].T


def hgrn2(proj, row0, n_seq, seq_len, lb3, norm_g, s0_all, layer, o_prev, snew_prev, *, C, nb, hh):
    R = nb * C
    n_chunks = seq_len // C
    assert seq_len % C == 0 and n_seq % nb == 0 and row0 % R == 0 and HG_HEADS % hh == 0
    assert nb == 1 or n_chunks == 1
    rb0 = row0 // R
    blocks_per_group = nb * seq_len // R
    hb = HG_HEADS // hh
    w = hh * HG_KEY

    def rmap(sec):
        return lambda s, h, c: (rb0 + s * blocks_per_group + c, sec * hb + h)

    in_specs = [pl.BlockSpec((R, w), rmap(sec)) for sec in range(4)]
    in_specs += [pl.BlockSpec((3, w), lambda s, h, c: (0, h)),
                 pl.BlockSpec((1, w), lambda s, h, c: (0, h))]
    args = [proj, proj, proj, proj, lb3, norm_g.reshape(1, -1)]
    state_spec = pl.BlockSpec((None, nb, hh, HG_KEY, HG_VAL), lambda s, h, c: (layer, s, h, 0, 0))
    if s0_all is not None:
        in_specs.append(state_spec)
        args.append(s0_all)
    aliases = {}
    for out_idx, prev in enumerate((o_prev, snew_prev)):
        if prev is not None:
            aliases[len(args)] = out_idx
            in_specs.append(_ANY_SPEC)
            args.append(prev)
    o_shape = o_prev.shape if o_prev is not None else (proj.shape[0], HG_HEADS * HG_VAL)
    return pl.pallas_call(
        functools.partial(_hgrn_kernel, C=C, nb=nb, hh=hh, n_chunks=n_chunks, has_s0=s0_all is not None),
        grid=(n_seq // nb, hb, n_chunks),
        in_specs=in_specs,
        out_specs=[pl.BlockSpec((R, w), lambda s, h, c: (rb0 + s * blocks_per_group + c, h)), state_spec],
        out_shape=[jax.ShapeDtypeStruct(o_shape, BF16),
                   jax.ShapeDtypeStruct(snew_prev.shape, F32)],
        scratch_shapes=[pltpu.VMEM((nb, hh, HG_VAL, HG_KEY), F32)],
        input_output_aliases=aliases,
        compiler_params=_params(("parallel", "parallel", "arbitrary")),
        name="hgrn2",
    )(*args)


def _rope_pair(slot, tab):
    prod = slot * tab
    return prod + pltpu.roll(prod, MLA_ROPE, 1)


def _mla_q_kernel(x_ref, g_ref, w_ref, tab_ref, o_ref):
    qa = x_ref[...][:, :MLA_Q_RANK]
    acc = _dot(_rms(qa, g_ref[...]), w_ref[...]) * (MLA_SCALE * LOG2E)
    tab = tab_ref[...]
    lane = lax.broadcasted_iota(jnp.int32, tab.shape, 1)
    nope_w = MLA_HEADS * MLA_NOPE
    for h in range(MLA_HEADS):
        slot = acc[:, nope_w + h * LANES: nope_w + (h + 1) * LANES]
        qr = jnp.where(lane < MLA_ROPE, _rope_pair(slot, tab), 0.0)
        o_ref[:, h * HEAD_SLOT: h * HEAD_SLOT + MLA_NOPE] = acc[:, h * MLA_NOPE:(h + 1) * MLA_NOPE].astype(BF16)
        o_ref[:, h * HEAD_SLOT + MLA_NOPE:(h + 1) * HEAD_SLOT] = qr.astype(BF16)


def mla_q(proj, q_norm, w_q_ext, tab, *, tm=512):
    m = proj.shape[0]
    tm = _tile(m, tm, SUBLANES)
    n = w_q_ext.shape[1]
    return pl.pallas_call(
        _mla_q_kernel,
        grid=(m // tm,),
        in_specs=[pl.BlockSpec((tm, 1024), lambda i: (i, COL_QA // 1024)),
                  pl.BlockSpec((1, MLA_Q_RANK), lambda i: (0, 0)),
                  pl.BlockSpec((MLA_Q_RANK, n), lambda i: (0, 0)),
                  pl.BlockSpec((tm, LANES), lambda i: (i, 0))],
        out_specs=pl.BlockSpec((tm, MLA_HEADS * HEAD_SLOT), lambda i: (i, 0)),
        out_shape=jax.ShapeDtypeStruct((m, MLA_HEADS * HEAD_SLOT), BF16),
        compiler_params=_params(("parallel",)),
        name="mla_q",
    )(proj, q_norm.reshape(1, -1), w_q_ext, tab)


def _mla_kv_kernel(*refs, with_kv):
    kva_ref, kr_ref, g_ref, w_ref, tab_ref = refs[:5]
    outs = refs[-4:] if with_kv else refs[-2:]
    c_ref, krot_ref = outs[:2]
    c = _rms(kva_ref[...], g_ref[...])
    c_ref[...] = c
    raw = kr_ref[...]
    lane = lax.broadcasted_iota(jnp.int32, raw.shape, 1)
    slot = jnp.where(lane < MLA_ROPE, raw,
                     jnp.where(lane < MLA_ROPE + MLA_ROPE // 2, pltpu.roll(raw, MLA_ROPE // 2, 1),
                               pltpu.roll(raw, LANES - MLA_ROPE // 2, 1)))
    krkr = _rope_pair(slot, tab_ref[...])
    krot_ref[...] = krkr[:, :MLA_ROPE]
    if with_kv:
        kfull_ref, v_ref = outs[2:]
        acc = _dot(c, w_ref[...])
        krkr_b = krkr.astype(BF16)
        for h in range(MLA_HEADS):
            kfull_ref[:, h * HEAD_SLOT: h * HEAD_SLOT + MLA_NOPE] = (
                acc[:, h * HEAD_SLOT: h * HEAD_SLOT + MLA_NOPE].astype(BF16))
            kfull_ref[:, h * HEAD_SLOT + MLA_NOPE:(h + 1) * HEAD_SLOT] = krkr_b
            v_ref[h * MLA_V:(h + 1) * MLA_V, :] = acc[:, h * HEAD_SLOT + MLA_NOPE:(h + 1) * HEAD_SLOT].T.astype(BF16)


def mla_kv(proj, row0, n_rows, kv_norm, w_kv_b, tab, layer, lat_prev, kr_prev, *, with_kv, tm=512):
    tm = _tile(math.gcd(n_rows, row0) if row0 else n_rows, tm, SUBLANES)
    rb0 = row0 // tm
    out_specs = [pl.BlockSpec((None, tm, MLA_KV_RANK), lambda i: (layer, i, 0)),
                 pl.BlockSpec((None, tm, MLA_ROPE), lambda i: (layer, i, 0))]
    out_shape = [jax.ShapeDtypeStruct(lat_prev.shape, F32), jax.ShapeDtypeStruct(kr_prev.shape, F32)]
    if with_kv:
        out_specs += [pl.BlockSpec((tm, MLA_HEADS * HEAD_SLOT), lambda i: (i, 0)),
                      pl.BlockSpec((MLA_HEADS * MLA_V, tm), lambda i: (0, i))]
        out_shape += [jax.ShapeDtypeStruct((n_rows, MLA_HEADS * HEAD_SLOT), BF16),
                      jax.ShapeDtypeStruct((MLA_HEADS * MLA_V, n_rows), BF16)]
    return pl.pallas_call(
        functools.partial(_mla_kv_kernel, with_kv=with_kv),
        grid=(n_rows // tm,),
        in_specs=[pl.BlockSpec((tm, MLA_KV_RANK), lambda i: (rb0 + i, COL_KVA // MLA_KV_RANK)),
                  pl.BlockSpec((tm, LANES), lambda i: (rb0 + i, COL_KR // LANES)),
                  pl.BlockSpec((1, MLA_KV_RANK), lambda i: (0, 0)),
                  pl.BlockSpec((MLA_KV_RANK, MLA_HEADS * HEAD_SLOT), lambda i: (0, 0)),
                  pl.BlockSpec((tm, LANES), lambda i: (rb0 + i, 0)),
                  _ANY_SPEC, _ANY_SPEC],
        out_specs=out_specs,
        out_shape=out_shape,
        input_output_aliases={5: 0, 6: 1},
        compiler_params=_params(("parallel",)),
        name="mla_kv",
    )(proj, proj, kv_norm.reshape(1, -1), w_kv_b, tab, lat_prev, kr_prev)


def _flash_kernel(qi_ref, ki_ref, q_ref, k_ref, vt_ref, prev_ref, o_ref, m_ref, l_ref, acc_ref, s_ref, *, tq, tk,
                  hh):
    t = pl.program_id(2)
    qi = qi_ref[t]
    ki = ki_ref[t]
    last_k = ((qi + 1) * tq - 1) // tk

    @pl.when(ki == 0)
    def _():
        m_ref[...] = jnp.full(m_ref.shape, -jnp.inf, F32)
        l_ref[...] = jnp.zeros(l_ref.shape, F32)
        acc_ref[...] = jnp.zeros(acc_ref.shape, F32)

    def update(masked):
        for hd in range(hh):
            s_ref[hd] = lax.dot_general(k_ref[:, hd * HEAD_SLOT:(hd + 1) * HEAD_SLOT],
                                        q_ref[:, hd * HEAD_SLOT:(hd + 1) * HEAD_SLOT], NT_DIMS,
                                        preferred_element_type=F32)
        for hd in range(hh):
            st = s_ref[hd]
            if masked:
                kpos = ki * tk + lax.broadcasted_iota(jnp.int32, (tk, tq), 0)
                qpos = qi * tq + lax.broadcasted_iota(jnp.int32, (tk, tq), 1)
                st = jnp.where(kpos <= qpos, st, -jnp.inf)
            m_prev = m_ref[hd]
            m_new = jnp.maximum(m_prev, jnp.max(st, axis=0, keepdims=True))
            alpha = jnp.exp2(m_prev - m_new)
            p = jnp.exp2(st - m_new)
            l_ref[hd] = alpha * l_ref[hd] + jnp.sum(p, axis=0, keepdims=True)
            acc_ref[hd] = alpha * acc_ref[hd] + jnp.dot(
                vt_ref[hd * MLA_V:(hd + 1) * MLA_V, :], p.astype(BF16), preferred_element_type=F32)
            m_ref[hd] = m_new

    @pl.when((ki + 1) * tk - 1 > qi * tq)
    def _():
        update(True)

    @pl.when((ki + 1) * tk - 1 <= qi * tq)
    def _():
        update(False)

    @pl.when(ki == last_k)
    def _():
        for hd in range(hh):
            o_ref[:, hd * MLA_V:(hd + 1) * MLA_V] = (acc_ref[hd] / l_ref[hd]).T.astype(o_ref.dtype)


def mla_prompt_attention(q_full, k_full, v_t, n_batch, seq, o_prev, *, tq=512, tk=512, hh=4):
    tq, tk = _tile(seq, tq, LANES), _tile(seq, tk, LANES)
    nq, nk = seq // tq, seq // tk
    pairs = [(a, b) for a in range(nq) for b in range(((a + 1) * tq - 1) // tk + 1)]
    qi_tab = jnp.asarray([p[0] for p in pairs], jnp.int32)
    ki_tab = jnp.asarray([p[1] for p in pairs], jnp.int32)
    grid_spec = pltpu.PrefetchScalarGridSpec(
        num_scalar_prefetch=2,
        grid=(n_batch, MLA_HEADS // hh, len(pairs)),
        in_specs=[pl.BlockSpec((tq, hh * HEAD_SLOT), lambda b, h, t, qt, kt: (b * nq + qt[t], h)),
                  pl.BlockSpec((tk, hh * HEAD_SLOT), lambda b, h, t, qt, kt: (b * nk + kt[t], h)),
                  pl.BlockSpec((hh * MLA_V, tk), lambda b, h, t, qt, kt: (h, b * nk + kt[t])),
                  _ANY_SPEC],
        out_specs=pl.BlockSpec((tq, hh * MLA_V), lambda b, h, t, qt, kt: (b * nq + qt[t], h)),
        scratch_shapes=[pltpu.VMEM((hh, 1, tq), F32), pltpu.VMEM((hh, 1, tq), F32),
                        pltpu.VMEM((hh, MLA_V, tq), F32), pltpu.VMEM((hh, tk, tq), F32)],
    )
    return pl.pallas_call(
        functools.partial(_flash_kernel, tq=tq, tk=tk, hh=hh),
        grid_spec=grid_spec,
        out_shape=jax.ShapeDtypeStruct(o_prev.shape, BF16),
        input_output_aliases={5: 0},
        compiler_params=_params(("parallel", "parallel", "arbitrary")),
        name="mla_prompt_flash",
    )(qi_tab, ki_tab, q_full, k_full, v_t, o_prev)


def _absorb_kernel(q_ref, w_ref, o_ref):
    qh = q_ref[...]
    q_lat = _dot_nt(qh[:, :MLA_NOPE], w_ref[...])
    out = jnp.concatenate([q_lat, qh[:, MLA_NOPE:].astype(F32)], axis=-1)
    o_ref[...] = out.reshape(o_ref.shape)


def mla_absorb(q_full, w_kv_b, row0, n_seq, t_len):
    rows = n_seq * t_len
    assert row0 % rows == 0
    width = MLA_KV_RANK + LANES
    return pl.pallas_call(
        _absorb_kernel,
        grid=(MLA_HEADS,),
        in_specs=[pl.BlockSpec((rows, HEAD_SLOT), lambda h: (row0 // rows, h)),
                  pl.BlockSpec((MLA_KV_RANK, MLA_NOPE), lambda h: (0, 2 * h))],
        out_specs=pl.BlockSpec((n_seq, None, t_len, width), lambda h: (0, h, 0, 0)),
        out_shape=jax.ShapeDtypeStruct((n_seq, MLA_HEADS, t_len, width), F32),
        compiler_params=_params(("parallel",)),
        name="mla_absorb",
    )(q_full, w_kv_b)


def _paged_attn_kernel(pt_ref, q_ref, cnew_ref, krnew_ref, clat_hbm, ckr_hbm, o_ref,
                       cf32, krf32, cbuf, m_ref, l_ref, acc_ref, csem, krsem, *, G, layer, t_len, n_steps, n_seq):
    b = pl.program_id(0)
    g = pl.program_id(1)
    step = b * n_steps + g
    slot = step % 2
    page = cbuf.shape[0] // G

    def page_copies(bb, gg, sl):
        copies = []
        for i in range(G):
            pidx = pt_ref[bb, gg * G + i]
            copies.append(pltpu.make_async_copy(
                clat_hbm.at[layer, pidx], cf32.at[sl, pl.ds(i * page, page), :], csem.at[sl]))
            copies.append(pltpu.make_async_copy(
                ckr_hbm.at[layer, pidx], krf32.at[sl, :, pl.ds(i * page, page)], krsem.at[sl]))
        return copies

    @pl.when(step == 0)
    def _():
        for cp in page_copies(b, g, slot):
            cp.start()

    @pl.when(step + 1 < n_seq * n_steps)
    def _():
        wrap = g + 1 == n_steps
        for cp in page_copies(jnp.where(wrap, b + 1, b), jnp.where(wrap, 0, g + 1), 1 - slot):
            cp.start()

    q = q_ref[0]
    q_lat = q[:, :MLA_KV_RANK]
    q_rope = q[:, MLA_KV_RANK:MLA_KV_RANK + MLA_ROPE]

    @pl.when(g == 0)
    def _():
        c_new = cnew_ref[...]
        kr_new = krnew_ref[...]
        tq = lax.broadcasted_iota(jnp.int32, (q.shape[0], 1), 0) % t_len
        cols = []
        for j in range(t_len):
            sj = (jnp.sum(q_lat * c_new[j:j + 1, :], axis=-1, keepdims=True)
                  + jnp.sum(q_rope * kr_new[j:j + 1, :], axis=-1, keepdims=True))
            cols.append(jnp.where(tq >= j, sj, -jnp.inf))
        m0 = cols[0]
        for j in range(1, t_len):
            m0 = jnp.maximum(m0, cols[j])
        l0 = jnp.zeros_like(m0)
        acc0 = jnp.zeros(acc_ref.shape, F32)
        for j in range(t_len):
            pj = jnp.exp2(cols[j] - m0)
            l0 = l0 + pj
            acc0 = acc0 + pj * c_new[j:j + 1, :]
        m_ref[...] = m0
        l_ref[...] = l0
        acc_ref[...] = acc0

    pltpu.make_async_copy(cf32.at[slot], cf32.at[slot], csem.at[slot]).wait()
    pltpu.make_async_copy(krf32.at[slot], krf32.at[slot], krsem.at[slot]).wait()

    q_lat_b = q_lat.astype(BF16)
    q_rope_b = q_rope.astype(BF16)
    s_parts = []
    for i in range(0, G, 2):
        cb = cf32[slot, pl.ds(i * page, 2 * page), :].astype(BF16)
        cbuf[i * page:(i + 2) * page, :] = cb
        krt = krf32[slot, :, pl.ds(i * page, 2 * page)].astype(BF16)
        s_parts.append(lax.dot_general(q_lat_b, cb, NT_DIMS, preferred_element_type=F32)
                       + jnp.dot(q_rope_b, krt, preferred_element_type=F32))
    s = jnp.concatenate(s_parts, axis=1)
    m_prev = m_ref[...]
    m_new = jnp.maximum(m_prev, jnp.max(s, axis=-1, keepdims=True))
    alpha = jnp.exp2(m_prev - m_new)
    p = jnp.exp2(s - m_new)
    l_ref[...] = alpha * l_ref[...] + jnp.sum(p, axis=-1, keepdims=True)
    acc_ref[...] = alpha * acc_ref[...] + jnp.dot(p.astype(BF16), cbuf[...], preferred_element_type=F32)
    m_ref[...] = m_new

    @pl.when(g == n_steps - 1)
    def _():
        o_ref[0] = acc_ref[...] / l_ref[...]


def mla_paged_attention(q_abs, lat_s, kr_s, cache_latent, cache_krope_t, layer, page_table, *, G=32):
    n_seq, heads, t_len, width = q_abs.shape
    n_pages = page_table.shape[1]
    page = cache_latent.shape[2]
    G = _tile(n_pages, G, 2)
    n_steps = n_pages // G
    rows = heads * t_len
    q3 = q_abs.reshape(n_seq, rows, width)
    grid_spec = pltpu.PrefetchScalarGridSpec(
        num_scalar_prefetch=1,
        grid=(n_seq, n_steps),
        in_specs=[pl.BlockSpec((1, rows, width), lambda b, g, pt: (b, 0, 0)),
                  pl.BlockSpec((None, t_len, MLA_KV_RANK), lambda b, g, pt: (layer, b, 0)),
                  pl.BlockSpec((None, t_len, MLA_ROPE), lambda b, g, pt: (layer, b, 0)),
                  _ANY_SPEC, _ANY_SPEC],
        out_specs=pl.BlockSpec((1, rows, MLA_KV_RANK), lambda b, g, pt: (b, 0, 0)),
        scratch_shapes=[pltpu.VMEM((2, G * page, MLA_KV_RANK), F32),
                        pltpu.VMEM((2, MLA_ROPE, G * page), F32),
                        pltpu.VMEM((G * page, MLA_KV_RANK), BF16),
                        pltpu.VMEM((rows, 1), F32), pltpu.VMEM((rows, 1), F32),
                        pltpu.VMEM((rows, MLA_KV_RANK), F32),
                        pltpu.SemaphoreType.DMA((2,)), pltpu.SemaphoreType.DMA((2,))],
    )
    return pl.pallas_call(
        functools.partial(_paged_attn_kernel, G=G, layer=layer, t_len=t_len, n_steps=n_steps, n_seq=n_seq),
        grid_spec=grid_spec,
        out_shape=jax.ShapeDtypeStruct((n_seq, rows, MLA_KV_RANK), F32),
        compiler_params=_params(("arbitrary", "arbitrary")),
        name="mla_paged_attention",
    )(page_table, q3, lat_s, kr_s, cache_latent, cache_krope_t)


def _uv_kernel(o_ref, w_ref, prev_ref, out_ref):
    o = o_ref[...]
    out_ref[...] = _dot(o.reshape(o.shape[0] * o.shape[1], o.shape[2]), w_ref[...]).astype(out_ref.dtype)


def mla_uv(o_lat, w_kv_b, row0, n_seq, t_len, ob_prev):
    o4 = o_lat.reshape(n_seq, MLA_HEADS, t_len, MLA_KV_RANK)
    rows = n_seq * t_len
    assert row0 % rows == 0
    return pl.pallas_call(
        _uv_kernel,
        grid=(MLA_HEADS,),
        in_specs=[pl.BlockSpec((n_seq, None, t_len, MLA_KV_RANK), lambda h: (0, h, 0, 0)),
                  pl.BlockSpec((MLA_KV_RANK, MLA_V), lambda h: (0, 2 * h + 1)),
                  _ANY_SPEC],
        out_specs=pl.BlockSpec((rows, MLA_V), lambda h: (row0 // rows, h)),
        out_shape=jax.ShapeDtypeStruct(ob_prev.shape, BF16),
        input_output_aliases={2: 0},
        compiler_params=_params(("parallel",)),
        name="mla_uv",
    )(o4, w_kv_b, ob_prev)


def _gmlp_kernel(*refs, emit_vn):
    gu_ref, gv_ref, g_ref, w_ref, bias_ref = refs[:5]
    o_ref = refs[-2] if emit_vn else refs[-1]
    vn = _rms(_gelu(gv_ref[...]), g_ref[...])
    if emit_vn:
        refs[-1][...] = vn
    u = _gelu(gu_ref[...])
    for grp in range(CM_GROUPS):
        sl = slice(grp * LANES, (grp + 1) * LANES)
        z = _dot(w_ref[grp], vn[:, sl]) + bias_ref[:, sl]
        o_ref[:, sl] = (u[:, sl] * z).astype(o_ref.dtype)


def gmlp(proj, row0, n_rows, v_norm, w_mix, bias_full, oc_prev, *, emit_vn):
    width = CM_GROUPS * LANES
    assert row0 % CM_CHUNK == 0 and n_rows % CM_CHUNK == 0
    rb0 = row0 // CM_CHUNK
    in_specs = [pl.BlockSpec((CM_CHUNK, width), lambda i: (rb0 + i, COL_GU // width)),
                pl.BlockSpec((CM_CHUNK, width), lambda i: (rb0 + i, COL_GV // width)),
                pl.BlockSpec((1, width), lambda i: (0, 0)),
                pl.BlockSpec((CM_GROUPS, CM_CHUNK, CM_CHUNK), lambda i: (0, 0, 0)),
                pl.BlockSpec((CM_CHUNK, width), lambda i: (0, 0))]
    args = [proj, proj, v_norm.reshape(1, -1), w_mix, bias_full]
    aliases = {}
    if oc_prev is not None:
        aliases[len(args)] = 0
        in_specs.append(_ANY_SPEC)
        args.append(oc_prev)
    out_specs = [pl.BlockSpec((CM_CHUNK, width), lambda i: (rb0 + i, 0))]
    out_shape = [jax.ShapeDtypeStruct((proj.shape[0], width), BF16)]
    if emit_vn:
        out_specs.append(pl.BlockSpec((CM_CHUNK, width), lambda i: (i, 0)))
        out_shape.append(jax.ShapeDtypeStruct((n_rows, width), F32))
    return pl.pallas_call(
        functools.partial(_gmlp_kernel, emit_vn=emit_vn),
        grid=(n_rows // CM_CHUNK,),
        in_specs=in_specs,
        out_specs=out_specs,
        out_shape=out_shape,
        input_output_aliases=aliases,
        compiler_params=_params(("parallel",)),
        name="gmlp",
    )(*args)


def _merge_kernel(oa_ref, ob_ref, oc_ref, g0_ref, g1_ref, g2_ref, w_ref, o_ref):
    acc = None
    for br, (b_ref, g_ref) in enumerate(((oa_ref, g0_ref), (ob_ref, g1_ref), (oc_ref, g2_ref))):
        part = g_ref[...] * jnp.dot(b_ref[...], w_ref[br], preferred_element_type=F32)
        acc = part if acc is None else acc + part
    o_ref[...] = acc.astype(o_ref.dtype)


def branch_merge(o_a, o_b, o_c, gates, w_branch, *, tm=1024, tn=512):
    m, bw = o_a.shape
    d = w_branch.shape[2]
    tm, tn = _tile(m, tm, SUBLANES), _tile(d, tn, LANES)
    nt = d // tn
    bspec = pl.BlockSpec((tm, bw), lambda i, j: (i, 0))

    def gspec(br):
        return pl.BlockSpec((tm, tn), lambda i, j: (i, br * nt + j))

    return pl.pallas_call(
        _merge_kernel,
        grid=(m // tm, nt),
        in_specs=[bspec, bspec, bspec, gspec(0), gspec(1), gspec(2),
                  pl.BlockSpec((N_BRANCH, bw, tn), lambda i, j: (0, 0, j))],
        out_specs=pl.BlockSpec((tm, tn), lambda i, j: (i, j)),
        out_shape=jax.ShapeDtypeStruct((m, d), BF16),
        compiler_params=_params(("parallel", "arbitrary")),
        name="branch_merge",
    )(o_a, o_b, o_c, gates, gates, gates, w_branch)


def _out_proj_kernel(mix_ref, x_ref, w_ref, g_ref, o_ref):
    y = jnp.dot(mix_ref[...], w_ref[...], preferred_element_type=F32)
    o_ref[...] = x_ref[...] + _rms(y, g_ref[...])


def out_proj(mix, x, w_out, g, *, tm=512):
    m, d = x.shape
    tm = _tile(m, tm, SUBLANES)
    return pl.pallas_call(
        _out_proj_kernel,
        grid=(m // tm,),
        in_specs=[pl.BlockSpec((tm, d), lambda i: (i, 0)),
                  pl.BlockSpec((tm, d), lambda i: (i, 0)),
                  pl.BlockSpec((d, d), lambda i: (0, 0)),
                  pl.BlockSpec((1, d), lambda i: (0, 0))],
        out_specs=pl.BlockSpec((tm, d), lambda i: (i, 0)),
        out_shape=jax.ShapeDtypeStruct((m, d), F32),
        compiler_params=_params(("parallel",)),
        name="out_proj",
    )(mix, x, w_out, g.reshape(1, -1))


def _ffn_up_kernel(*refs, t_len, per_seq_prev, tiles_per_seq, kc, rb):
    x_ref, g_ref, wg_ref, wv_ref, cw_ref, cb_ref = refs[:6]
    if per_seq_prev:
        pg_ref, pv_ref = refs[6:8]
        act_ref, tailg_ref, tailv_ref, h_ref = refs[-4:]
    else:
        act_ref, tailg_ref, tailv_ref, h_ref, carry_ref, shift_ref = refs[-6:]
    i = pl.program_id(0)
    j = pl.program_id(1)
    tm, tn = act_ref.shape

    @pl.when(j == 0)
    def _():
        h_ref[...] = _rms(x_ref[...], g_ref[...]).astype(BF16)

    if not per_seq_prev:
        @pl.when(i == 0)
        def _():
            carry_ref[j] = jnp.zeros(carry_ref.shape[1:], F32)

        seq_start = (i % tiles_per_seq) == 0

    def conv(u, which, sl, p):
        if per_seq_prev:
            u = u.reshape(rb // t_len, t_len, u.shape[-1])
            tpos = lax.broadcasted_iota(jnp.int32, (1, t_len, 1), 1)
            back1 = p[:, 1:2, :]
            back2 = jnp.where(tpos == 0, p[:, 0:1, :], p[:, 1:2, :])
            axis = 1
            m1 = jnp.where(tpos == 0, back1, pltpu.roll(u, 1, axis))
            m2 = jnp.where(tpos <= 1, back2, pltpu.roll(u, 2, axis))
        else:
            shift_ref[which, 0:SUBLANES, :] = p
            shift_ref[which, SUBLANES:SUBLANES + rb, :] = u
            m1 = shift_ref[which, SUBLANES - 1:SUBLANES - 1 + rb, :]
            m2 = shift_ref[which, SUBLANES - 2:SUBLANES - 2 + rb, :]
        y = (cb_ref[which:which + 1, sl] + cw_ref[which, 0:1, sl] * m2 + cw_ref[which, 1:2, sl] * m1
             + cw_ref[which, 2:3, sl] * u)
        return y.reshape(rb, y.shape[-1])

    tails = []
    for c0 in range(0, tn, kc):
        sl = slice(c0, c0 + kc)
        wg = wg_ref[:, sl].astype(BF16)
        wv = wv_ref[:, sl].astype(BF16)
        if not per_seq_prev:
            pg = jnp.where(seq_start, 0.0, carry_ref[j, 0][:, sl])
            pv = jnp.where(seq_start, 0.0, carry_ref[j, 1][:, sl])
        for r0 in range(0, tm, rb):
            hr = h_ref[r0:r0 + rb, :]
            ug = jnp.dot(hr, wg, preferred_element_type=F32)
            uv = jnp.dot(hr, wv, preferred_element_type=F32)
            if per_seq_prev:
                s0, s1 = r0 // t_len, (r0 + rb) // t_len
                pg = pg_ref[s0:s1, :, sl]
                pv = pv_ref[s0:s1, :, sl]
            act = _gelu(conv(ug, 0, sl, pg)) * conv(uv, 1, sl, pv)
            act_ref[r0:r0 + rb, sl] = act.astype(act_ref.dtype)
            if per_seq_prev:
                tailg_ref[s0:s1, :, sl] = ug.reshape(rb // t_len, t_len, kc)[:, t_len - (CONV_W - 1):, :]
                tailv_ref[s0:s1, :, sl] = uv.reshape(rb // t_len, t_len, kc)[:, t_len - (CONV_W - 1):, :]
            else:
                pg, pv = ug[rb - SUBLANES:, :], uv[rb - SUBLANES:, :]
        if not per_seq_prev:
            tailg_ref[:, sl] = pg
            tailv_ref[:, sl] = pv
            tails.append((sl, pg, pv))
    for sl, tg, tv in tails:
        carry_ref[j, 0, :, sl] = tg
        carry_ref[j, 1, :, sl] = tv


def ffn_up(x, g, w_up_all, layer, conv_w2, conv_b2, prev_state, row0, n_rows, t_len, act_prev, *, tm=1024,
           tn=512, kc=256, rb=512):
    d = x.shape[1]
    d_ff = w_up_all.shape[2] // 2
    per_seq_prev = t_len == SUBLANES
    tm = _tile(n_rows if per_seq_prev else t_len, tm, SUBLANES)
    tn = _tile(d_ff, tn, LANES)
    kc = _tile(tn, kc, LANES)
    rb = _tile(tm, rb, t_len if per_seq_prev else SUBLANES)
    n_j = d_ff // tn
    n_i = n_rows // tm
    assert row0 % tm == 0 and (tm % t_len == 0 or t_len % tm == 0)
    rb0 = row0 // tm
    in_specs = [pl.BlockSpec((tm, d), lambda i, j: (rb0 + i, 0)),
                pl.BlockSpec((1, d), lambda i, j: (0, 0)),
                pl.BlockSpec((None, d, tn), lambda i, j: (layer, 0, j)),
                pl.BlockSpec((None, d, tn), lambda i, j: (layer, 0, n_j + j)),
                pl.BlockSpec((2, CONV_W, tn), lambda i, j: (0, 0, j)),
                pl.BlockSpec((2, tn), lambda i, j: (0, j))]
    args = [x, g.reshape(1, -1), w_up_all, w_up_all, conv_w2, conv_b2]
    scratch = [pltpu.VMEM((tm, d), BF16)]
    if per_seq_prev:
        assert t_len == SUBLANES
        ns = tm // t_len
        in_specs += [pl.BlockSpec((None, ns, CONV_W - 1, tn), lambda i, j: (layer, i, 0, j)),
                     pl.BlockSpec((None, ns, CONV_W - 1, tn), lambda i, j: (layer, i, 0, n_j + j))]
        args += [prev_state, prev_state]
        tail_spec = pl.BlockSpec((ns, CONV_W - 1, tn), lambda i, j: (i, 0, j))
        tail_shape = jax.ShapeDtypeStruct((n_rows // t_len, CONV_W - 1, d_ff), F32)
    else:
        scratch.append(pltpu.VMEM((n_j, 2, SUBLANES, tn), F32))
        scratch.append(pltpu.VMEM((2, SUBLANES + rb, kc), F32))
        tail_spec = pl.BlockSpec((None, SUBLANES, tn), lambda i, j: (i, 0, j))
        tail_shape = jax.ShapeDtypeStruct((n_i, SUBLANES, d_ff), F32)
    aliases = {}
    if act_prev is not None:
        aliases[len(args)] = 0
        in_specs.append(_ANY_SPEC)
        args.append(act_prev)
    return pl.pallas_call(
        functools.partial(_ffn_up_kernel, t_len=t_len, per_seq_prev=per_seq_prev,
                          tiles_per_seq=max(t_len // tm, 1), kc=kc, rb=rb),
        grid=(n_i, n_j),
        in_specs=in_specs,
        out_specs=[pl.BlockSpec((tm, tn), lambda i, j: (rb0 + i, j)), tail_spec, tail_spec],
        out_shape=[jax.ShapeDtypeStruct((x.shape[0], d_ff), BF16), tail_shape, tail_shape],
        scratch_shapes=scratch,
        input_output_aliases=aliases,
        compiler_params=_params(("arbitrary", "arbitrary")),
        name="ffn_up",
    )(*args)


def _ffn_down_kernel(a_ref, w_ref, x_ref, g_ref, o_ref, acc_ref, *, n_k):
    kk = pl.program_id(1)
    part = jnp.dot(a_ref[...], w_ref[...], preferred_element_type=F32)
    if n_k == 1:
        o_ref[...] = x_ref[...] + _rms(part, g_ref[...])
        return

    @pl.when(kk == 0)
    def _():
        acc_ref[...] = part

    @pl.when((kk > 0) & (kk < n_k - 1))
    def _():
        acc_ref[...] += part

    @pl.when(kk == n_k - 1)
    def _():
        o_ref[...] = x_ref[...] + _rms(acc_ref[...] + part, g_ref[...])


def ffn_down(act, w_down, x, g, *, tm=512, tk=2816):
    d_ff, d = w_down.shape
    m = x.shape[0]
    tm = _tile(m, tm, SUBLANES)
    tk = _tile(d_ff, tk, LANES)
    n_k = d_ff // tk
    return pl.pallas_call(
        functools.partial(_ffn_down_kernel, n_k=n_k),
        grid=(m // tm, n_k),
        in_specs=[pl.BlockSpec((tm, tk), lambda i, k: (i, k)),
                  pl.BlockSpec((tk, d), lambda i, k: (k, 0)),
                  pl.BlockSpec((tm, d), lambda i, k: (i, 0)),
                  pl.BlockSpec((1, d), lambda i, k: (0, 0))],
        out_specs=pl.BlockSpec((tm, d), lambda i, k: (i, 0)),
        out_shape=jax.ShapeDtypeStruct(x.shape, F32),
        scratch_shapes=[pltpu.VMEM((tm, d), F32)],
        compiler_params=_params(("parallel", "arbitrary")),
        name="ffn_down",
    )(act, w_down, x, g.reshape(1, -1))


def _ple_kernel(x_ref, pe_ref, g_ref, wg_ref, wp_ref, *o_refs, n_first):
    x = x_ref[...]
    gate = _sigmoid(_dot(_rms(x, g_ref[...]), wg_ref[...]))
    y = x + _dot(pe_ref[...], wp_ref[...]) * gate
    if n_first is None:
        o_refs[0][...] = y
        return
    i = pl.program_id(0)

    @pl.when(i < n_first)
    def _():
        o_refs[0][...] = y

    @pl.when(i >= n_first)
    def _():
        o_refs[1][...] = y


def ple(x, pe, g, w_gate, w_proj, *, split_rows=None, tm=512):
    m, d = x.shape
    pd = pe.shape[1]
    tm = _tile(m if split_rows is None else math.gcd(split_rows, m - split_rows), tm, SUBLANES)
    if split_rows is None:
        n_first = None
        out_specs = pl.BlockSpec((tm, d), lambda i: (i, 0))
        out_shape = jax.ShapeDtypeStruct((m, d), F32)
    else:
        n_first = split_rows // tm
        out_specs = [pl.BlockSpec((tm, d), lambda i: (jnp.minimum(i, n_first - 1), 0)),
                     pl.BlockSpec((tm, d), lambda i: (jnp.maximum(i - n_first, 0), 0))]
        out_shape = [jax.ShapeDtypeStruct((split_rows, d), F32), jax.ShapeDtypeStruct((m - split_rows, d), F32)]
    return pl.pallas_call(
        functools.partial(_ple_kernel, n_first=n_first),
        grid=(m // tm,),
        in_specs=[pl.BlockSpec((tm, d), lambda i: (i, 0)),
                  pl.BlockSpec((tm, pd), lambda i: (i, 0)),
                  pl.BlockSpec((1, d), lambda i: (0, 0)),
                  pl.BlockSpec((d, d), lambda i: (0, 0)),
                  pl.BlockSpec((pd, d), lambda i: (0, 0))],
        out_specs=out_specs,
        out_shape=out_shape,
        compiler_params=_params(("arbitrary",)),
        name="ple",
    )(x, pe, g.reshape(1, -1), w_gate, w_proj)


def _rope_table(pos):
    inv = 1.0 / (ROPE_THETA ** (jnp.arange(0, MLA_ROPE, 2, dtype=F32) / MLA_ROPE))
    ang = pos.astype(F32)[:, None] * inv[None, :]
    cos, sin = jnp.cos(ang), jnp.sin(ang)
    return jnp.concatenate([cos, cos, -sin, sin], axis=-1)


def _swap_halves(w, width):
    half = width // 2
    return jnp.concatenate([w[..., half:], w[..., :half]], axis=-1)


def kernel(x_prompt, x_sample, cache_latent, cache_krope, state_hgrn, state_ffn_conv, page_table, p_prompt, p_sample, ln_mix_pre, ln_mix_post, ln_ffn_pre, ln_ffn_post, ln_ple, w_in, hgrn_lower_bounds, hgrn_norm, mla_q_norm, mla_kv_norm, w_q_b, w_kv_b, cm_v_norm, cm_w_s, cm_b, w_merge_gate, w_branch, w_out, w_ffn_up, ffn_conv_w, ffn_conv_b, w_ffn_down, w_ple_proj, w_ple_gate):
    n_b, seq, d = x_prompt.shape
    n_db, t_len, _ = x_sample.shape
    depth = w_in.shape[0]
    d_ff = w_ffn_down.shape[1]
    page = cache_latent.shape[2]
    past_len = page_table.shape[1] * page
    rows_p, rows_s = n_b * seq, n_db * t_len
    rows = rows_p + rows_s
    hk = HG_HEADS * HG_KEY

    lb_soft = jax.nn.softmax(hgrn_lower_bounds.astype(F32), axis=0)
    lbs = jnp.concatenate([jnp.zeros_like(lb_soft[:1]), jnp.cumsum(lb_soft[1:], axis=0)], axis=0)

    tab = jnp.concatenate([jnp.tile(_rope_table(jnp.arange(seq)), (n_b, 1)),
                           jnp.tile(_rope_table(past_len + jnp.arange(t_len)), (n_db, 1))], axis=0)
    cache_krope_t = jnp.swapaxes(cache_krope, 2, 3)
    w_in_t = jnp.swapaxes(w_in, 1, 2)

    x = jnp.concatenate([x_prompt.reshape(rows_p, d), x_sample.reshape(rows_s, d)], axis=0)
    tri = jnp.tril(jnp.ones((CM_CHUNK, CM_CHUNK), F32))
    seq_per_chunk = CM_CHUNK // t_len

    lat_p = jnp.zeros((depth, rows_p, MLA_KV_RANK), F32)
    kr_p = jnp.zeros((depth, rows_p, MLA_ROPE), F32)
    lat_s = jnp.zeros((depth, rows_s, MLA_KV_RANK), F32)
    kr_s = jnp.zeros((depth, rows_s, MLA_ROPE), F32)
    hg_p = jnp.zeros((depth, n_b, HG_HEADS, HG_KEY, HG_VAL), F32)
    hg_s = jnp.zeros((depth, n_db, HG_HEADS, HG_KEY, HG_VAL), F32)
    cv_p, cv_s, v_s = [], [], []

    for i in range(depth):
        wq = w_q_b[i].reshape(MLA_Q_RANK, MLA_HEADS, MLA_NOPE + MLA_ROPE)
        wq_rope = wq[..., MLA_NOPE:]
        w_q_ext = jnp.concatenate(
            [wq[..., :MLA_NOPE].reshape(MLA_Q_RANK, -1),
             jnp.concatenate([wq_rope, _swap_halves(wq_rope, MLA_ROPE)], axis=-1).reshape(MLA_Q_RANK, -1)],
            axis=1).astype(BF16)
        w_kvb = w_kv_b[i].astype(BF16)
        lb = lbs[i]
        lb3 = jnp.stack([lb, jnp.log(lb), jnp.log1p(-lb)], axis=0)
        w_mix_p = cm_w_s[i] * tri
        w_mix_s = jnp.einsum("ab,gts->gatbs", jnp.eye(seq_per_chunk, dtype=F32),
                             w_mix_p[:, :t_len, :t_len]).reshape(CM_GROUPS, CM_CHUNK, CM_CHUNK)
        bias_p = jnp.repeat(cm_b[i].T, LANES, axis=1)
        bias_s = jnp.tile(bias_p[:t_len], (seq_per_chunk, 1))
        conv_w2 = ffn_conv_w[i].reshape(CONV_W, 2, d_ff).transpose(1, 0, 2)
        conv_b2 = ffn_conv_b[i].reshape(2, d_ff)

        zeros_branch = jnp.zeros((rows, HG_HEADS * HG_VAL), BF16)
        proj = norm_matmul_t(x, ln_mix_pre[i], w_in_t, i, IN_TILE_SRC, tn=IN_TILE)
        gates = norm_matmul(x, ln_mix_pre[i], w_merge_gate, i, act="sigmoid", tn=1024)

        o_a, hg_p = hgrn2(proj, 0, n_b, seq, lb3, hgrn_norm[i], None, i, zeros_branch, hg_p,
                          C=_tile(seq, 128, SUBLANES), nb=1, hh=4)
        o_a, hg_s = hgrn2(proj, rows_p, n_db, t_len, lb3, hgrn_norm[i], state_hgrn, i, o_a, hg_s,
                          C=t_len, nb=_tile(n_db, LANES // t_len), hh=2)

        q_full = mla_q(proj, mla_q_norm[i], w_q_ext, tab)
        lat_p, kr_p, k_full, v_all = mla_kv(proj, 0, rows_p, mla_kv_norm[i], w_kvb, tab, i, lat_p, kr_p,
                                            with_kv=True)
        lat_s, kr_s = mla_kv(proj, rows_p, rows_s, mla_kv_norm[i], w_kvb, tab, i, lat_s, kr_s, with_kv=False)
        o_b = mla_prompt_attention(q_full, k_full, v_all, n_b, seq, zeros_branch)
        q_abs = mla_absorb(q_full, w_kvb, rows_p, n_db, t_len)
        o_lat = mla_paged_attention(q_abs, lat_s, kr_s, cache_latent, cache_krope_t, i, page_table)
        o_b = mla_uv(o_lat, w_kvb, rows_p, n_db, t_len, o_b)

        (o_c,) = gmlp(proj, 0, rows_p, cm_v_norm[i], w_mix_p, bias_p, zeros_branch, emit_vn=False)
        o_c, vn_s = gmlp(proj, rows_p, rows_s, cm_v_norm[i], w_mix_s, bias_s, o_c, emit_vn=True)

        mix = branch_merge(o_a, o_b, o_c, gates, w_branch[i].astype(BF16))
        x = out_proj(mix, x, w_out[i].astype(BF16), ln_mix_post[i])

        act, tg_p, tv_p = ffn_up(x, ln_ffn_pre[i], w_ffn_up, i, conv_w2, conv_b2, None, 0, rows_p, seq,
                                 jnp.zeros((rows, d_ff), BF16))
        act, tg_s, tv_s = ffn_up(x, ln_ffn_pre[i], w_ffn_up, i, conv_w2, conv_b2, state_ffn_conv, rows_p, rows_s,
                                 t_len, act)
        x = ffn_down(act, w_ffn_down[i].astype(BF16), x, ln_ffn_post[i])

        pe = jnp.concatenate([p_prompt[i].reshape(rows_p, -1), p_sample[i].reshape(rows_s, -1)], axis=0)
        x = ple(x, pe, ln_ple[i], w_ple_gate[i].astype(BF16), w_ple_proj[i].astype(BF16),
                split_rows=rows_p if i == depth - 1 else None)

        tails_p = jnp.concatenate([tg_p, tv_p], axis=-1).reshape(n_b, -1, SUBLANES, 2 * d_ff)
        cv_p.append(tails_p[:, -1, SUBLANES - (CONV_W - 1):, :])
        cv_s.append(jnp.concatenate([tg_s, tv_s], axis=-1))
        v_s.append(vn_s.reshape(n_db, t_len, -1))

    y_p, y_s = x
    return (y_p.reshape(n_b, seq, d), y_s.reshape(n_db, t_len, d),
            lat_p.reshape(depth, n_b, seq, -1), kr_p.reshape(depth, n_b, seq, -1),
            lat_s.reshape(depth, n_db, t_len, -1), kr_s.reshape(depth, n_db, t_len, -1),
            hg_p, hg_s, jnp.stack(cv_p), jnp.stack(cv_s), jnp.stack(v_s))
```

```python
import functools
import math

import jax
import jax.numpy as jnp
import numpy as np
from jax import lax
from jax.experimental import pallas as pl
from jax.experimental.pallas import tpu as pltpu

F32 = jnp.float32
BF16 = jnp.bfloat16
HIGHEST = lax.Precision.HIGHEST

HG_HEADS = 8
HG_KEY = 128
HG_VAL = 128
MLA_HEADS = 8
MLA_NOPE = 128
MLA_ROPE = 64
MLA_V = 128
MLA_Q_RANK = 768
MLA_KV_RANK = 512
MLA_SCALE = (MLA_NOPE + MLA_ROPE) ** -0.5
ROPE_THETA = 10000.0
CM_GROUPS = 8
CM_CHUNK = 128
N_BRANCH = 3
CONV_W = 3
EPS = 1e-6
LOG2E = math.log2(math.e)

LANES = 128
SUBLANES = 8
VMEM_LIMIT = 56 * 1024 * 1024
HEAD_SLOT = 2 * LANES

IN_TILE = 1024
IN_TILE_SRC = (0, 1024, 2048, 3072, 5440, 6464, 4096, 4864)
COL_GU, COL_GV = 4096, 5120
COL_QA = 6144
COL_KVA = 7168
COL_KR = 7680

NT_DIMS = (((1,), (1,)), ((), ()))
TN_DIMS = (((0,), (0,)), ((), ()))


def _params(sem):
    return pltpu.CompilerParams(dimension_semantics=sem, vmem_limit_bytes=VMEM_LIMIT)


def _rms(x, g):
    ms = jnp.mean(x * x, axis=-1, keepdims=True)
    return x * lax.rsqrt(ms + EPS) * g


def _gelu(x):
    return 0.5 * x * (1.0 + jnp.tanh(math.sqrt(2.0 / math.pi) * (x + 0.044715 * (x * x * x))))


def _sigmoid(x):
    return 0.5 * jnp.tanh(0.5 * x) + 0.5


def _silu(x):
    return x * _sigmoid(x)


def _tile(n, pref, unit=1):
    t = (min(pref, n) // unit) * unit
    while t > unit and n % t:
        t -= unit
    assert t > 0 and n % t == 0, (n, pref, unit)
    return t


def _dot(a, b):
    return jnp.dot(a.astype(BF16), b.astype(BF16), preferred_element_type=F32)


def _dot_nt(a, b):
    return lax.dot_general(a.astype(BF16), b.astype(BF16), NT_DIMS, preferred_element_type=F32)


_ANY_SPEC = pl.BlockSpec(memory_space=pl.ANY)


def _rms_cast_kernel(x_ref, g_ref, o_ref):
    o_ref[...] = _rms(x_ref[...], g_ref[...]).astype(BF16)


def rms_cast(x, g, *, tm=1024):
    m, k = x.shape
    tm = _tile(m, tm, SUBLANES)
    return pl.pallas_call(
        _rms_cast_kernel,
        grid=(m // tm,),
        in_specs=[pl.BlockSpec((tm, k), lambda i: (i, 0)), pl.BlockSpec((1, k), lambda i: (0, 0))],
        out_specs=pl.BlockSpec((tm, k), lambda i: (i, 0)),
        out_shape=jax.ShapeDtypeStruct((m, k), BF16),
        compiler_params=_params(("parallel",)),
        name="rms_cast",
    )(x, g.reshape(1, k))


def _matmul_ws_kernel(*refs, transposed, act):
    h_ref, w_ref, o_ref, wb_ref = refs[-4:]

    @pl.when(pl.program_id(1) == 0)
    def _():
        wb_ref[...] = (w_ref[0] if transposed else w_ref[...]).astype(BF16)

    if transposed:
        acc = lax.dot_general(h_ref[...], wb_ref[...], NT_DIMS, preferred_element_type=F32)
    else:
        acc = jnp.dot(h_ref[...], wb_ref[...], preferred_element_type=F32)
    if act == "sigmoid":
        acc = _sigmoid(acc)
    o_ref[...] = acc


def matmul_ws(hn, w_all, layer, *, act=None, tm=1024, tn=1024):
    m, k = hn.shape
    n = w_all.shape[2]
    tm, tn = _tile(m, tm, SUBLANES), _tile(n, tn, LANES)
    return pl.pallas_call(
        functools.partial(_matmul_ws_kernel, transposed=False, act=act),
        grid=(n // tn, m // tm),
        in_specs=[pl.BlockSpec((tm, k), lambda j, i: (i, 0)),
                  pl.BlockSpec((None, k, tn), lambda j, i: (layer, 0, j))],
        out_specs=pl.BlockSpec((tm, tn), lambda j, i: (i, j)),
        out_shape=jax.ShapeDtypeStruct((m, n), F32),
        scratch_shapes=[pltpu.VMEM((k, tn), BF16)],
        compiler_params=_params(("arbitrary", "arbitrary")),
        name="matmul_ws",
    )(hn, w_all)


def matmul_ws_t(hn, w_t, layer, col_offsets, *, tm=1024, tn=1024):
    m, k = hn.shape
    tm = _tile(m, tm, SUBLANES)
    n_tiles = len(col_offsets)
    assert all(o % SUBLANES == 0 and o + tn <= w_t.shape[1] for o in col_offsets)
    grid_spec = pltpu.PrefetchScalarGridSpec(
        num_scalar_prefetch=1,
        grid=(n_tiles, m // tm),
        in_specs=[pl.BlockSpec((tm, k), lambda j, i, off: (i, 0)),
                  pl.BlockSpec((pl.Element(1), pl.Element(tn), pl.Element(k)),
                               lambda j, i, off: (layer, pl.multiple_of(off[j], SUBLANES), 0))],
        out_specs=pl.BlockSpec((tm, tn), lambda j, i, off: (i, j)),
        scratch_shapes=[pltpu.VMEM((tn, k), BF16)],
    )
    return pl.pallas_call(
        functools.partial(_matmul_ws_kernel, transposed=True, act=None),
        grid_spec=grid_spec,
        out_shape=jax.ShapeDtypeStruct((m, n_tiles * tn), F32),
        compiler_params=_params(("arbitrary", "arbitrary")),
        name="matmul_ws_t",
    )(jnp.asarray(col_offsets, jnp.int32), hn, w_t)


def _hgrn_head(hq, hf, v, hg, lb3, ng, st_ref, hd, *, C, nb):
    R = nb * C
    lb = lb3[0:1, :]
    log_lb = lb3[1:2, :]
    log_1m_lb = lb3[2:3, :]
    q = _silu(hq)
    log_sig = jnp.minimum(hf, 0.0) - jnp.log(1.0 + jnp.exp(-jnp.abs(hf)))
    c2 = log_1m_lb + log_sig
    logf = jnp.maximum(log_lb, c2) + jnp.log(1.0 + jnp.exp(-jnp.abs(log_lb - c2)))
    k = (1.0 - lb) * (1.0 / (1.0 + jnp.exp(hf)))

    row = lax.broadcasted_iota(jnp.int32, (R, R), 0)
    col = lax.broadcasted_iota(jnp.int32, (R, R), 1)
    tri = jnp.where((col <= row) & ((row // C) == (col // C)), 1.0, 0.0).astype(F32)
    b = jnp.dot(tri, logf, precision=HIGHEST, preferred_element_type=F32)

    G = R // SUBLANES
    q3 = q.reshape(G, SUBLANES, HG_KEY)
    k3 = k.reshape(G, SUBLANES, HG_KEY)
    b3 = b.reshape(G, SUBLANES, HG_KEY)
    v3 = v.reshape(G, SUBLANES, HG_VAL)
    tpos = lax.broadcasted_iota(jnp.int32, (G, SUBLANES, 1), 1)
    o3 = jnp.zeros((G, SUBLANES, HG_VAL), F32)
    for s in range(SUBLANES):
        dec = jnp.exp(b3 - b3[:, s:s + 1, :])
        a_s = jnp.sum(q3 * k3[:, s:s + 1, :] * dec, axis=-1, keepdims=True)
        o3 = o3 + jnp.where(tpos >= s, a_s, 0.0) * v3[:, s:s + 1, :]
    o = o3.reshape(R, HG_VAL)

    half = C // 2
    a_off = None
    rrow = lax.broadcasted_iota(jnp.int32, (R, 1), 0)
    while half >= SUBLANES:
        blk = 2 * half
        nblk = R // blk
        b4 = b.reshape(nblk, blk, HG_KEY)
        ref_b = jnp.broadcast_to(b4[:, half - 1:half, :], (nblk, blk, HG_KEY)).reshape(R, HG_KEY)
        second = (rrow % blk) >= half
        qe = jnp.where(second, q * jnp.exp(b - ref_b), 0.0)
        ke = jnp.where(second, 0.0, k * jnp.exp(ref_b - b))
        a_l = _dot_nt(qe, ke)
        if blk != R:
            a_l = jnp.where((row // blk) == (col // blk), a_l, 0.0)
        a_off = a_l if a_off is None else a_off + a_l
        half //= 2
    if a_off is not None:
        o = o + _dot(a_off, v)

    qb = q * jnp.exp(b)
    o_parts = []
    for n in range(nb):
        lo, hi = n * C, (n + 1) * C
        st = st_ref[n, hd]
        o_parts.append(_dot_nt(qb[lo:hi], st))
        b_last = b[hi - 1:hi, :]
        kd = k[lo:hi] * jnp.exp(b_last - b[lo:hi])
        st_ref[n, hd] = st * jnp.exp(b_last) + lax.dot_general(
            v[lo:hi].astype(BF16), kd.astype(BF16), TN_DIMS, preferred_element_type=F32)
    o = o + (o_parts[0] if nb == 1 else jnp.concatenate(o_parts, axis=0))
    return _rms(o, ng) * _silu(hg)


def _hgrn_kernel(*refs, C, nb, hh, n_chunks, has_s0):
    hq_ref, hf_ref, hi_ref, hg_ref, lb_ref, ng_ref = refs[:6]
    s0_ref = refs[6] if has_s0 else None
    o_ref, snew_ref, st_ref = refs[-3:]
    ci = pl.program_id(2)

    @pl.when(ci == 0)
    def _():
        for n in range(nb):
            for hd in range(hh):
                st_ref[n, hd] = s0_ref[n, hd].T if has_s0 else jnp.zeros((HG_VAL, HG_KEY), F32)

    for hd in range(hh):
        sl = slice(hd * HG_KEY, (hd + 1) * HG_KEY)
        o = _hgrn_head(hq_ref[:, sl], hf_ref[:, sl], hi_ref[:, sl], hg_ref[:, sl], lb_ref[:, sl], ng_ref[:, sl],
                       st_ref, hd, C=C, nb=nb)
        o_ref[:, sl] = o.astype(o_ref.dtype)

    @pl.when(ci == n_chunks - 1)
    def _():
        for n in range(nb):
            for hd in range(hh):
                snew_ref[n, hd] = st_ref[n, hd].T


def hgrn2(proj, row0, n_seq, seq_len, lb3, norm_g, s0_all, layer, o_prev, snew_prev, *, C, nb, hh):
    R = nb * C
    n_chunks = seq_len // C
    assert seq_len % C == 0 and n_seq % nb == 0 and row0 % R == 0 and HG_HEADS % hh == 0
    assert nb == 1 or n_chunks == 1
    rb0 = row0 // R
    blocks_per_group = nb * seq_len // R
    hb = HG_HEADS // hh
    w = hh * HG_KEY

    def rmap(sec):
        return lambda s, h, c: (rb0 + s * blocks_per_group + c, sec * hb + h)

    in_specs = [pl.BlockSpec((R, w), rmap(sec)) for sec in range(4)]
    in_specs += [pl.BlockSpec((3, w), lambda s, h, c: (0, h)),
                 pl.BlockSpec((1, w), lambda s, h, c: (0, h))]
    args = [proj, proj, proj, proj, lb3, norm_g.reshape(1, -1)]
    state_spec = pl.BlockSpec((None, nb, hh, HG_KEY, HG_VAL), lambda s, h, c: (layer, s, h, 0, 0))
    if s0_all is not None:
        in_specs.append(state_spec)
        args.append(s0_all)
    aliases = {}
    for out_idx, prev in enumerate((o_prev, snew_prev)):
        if prev is not None:
            aliases[len(args)] = out_idx
            in_specs.append(_ANY_SPEC)
            args.append(prev)
    o_shape = o_prev.shape if o_prev is not None else (proj.shape[0], HG_HEADS * HG_VAL)
    return pl.pallas_call(
        functools.partial(_hgrn_kernel, C=C, nb=nb, hh=hh, n_chunks=n_chunks, has_s0=s0_all is not None),
        grid=(n_seq // nb, hb, n_chunks),
        in_specs=in_specs,
        out_specs=[pl.BlockSpec((R, w), lambda s, h, c: (rb0 + s * blocks_per_group + c, h)), state_spec],
        out_shape=[jax.ShapeDtypeStruct(o_shape, BF16),
                   jax.ShapeDtypeStruct(snew_prev.shape, F32)],
        scratch_shapes=[pltpu.VMEM((nb, hh, HG_VAL, HG_KEY), F32)],
        input_output_aliases=aliases,
        compiler_params=_params(("parallel", "parallel", "arbitrary")),
        name="hgrn2",
    )(*args)


def _rope_pair(slot, tab):
    prod = slot * tab
    return prod + pltpu.roll(prod, MLA_ROPE, 1)


def _mla_q_kernel(x_ref, g_ref, w_ref, tab_ref, o_ref):
    qa = x_ref[...][:, :MLA_Q_RANK]
    acc = _dot(_rms(qa, g_ref[...]), w_ref[...]) * (MLA_SCALE * LOG2E)
    tab = tab_ref[...]
    lane = lax.broadcasted_iota(jnp.int32, tab.shape, 1)
    nope_w = MLA_HEADS * MLA_NOPE
    for h in range(MLA_HEADS):
        slot = acc[:, nope_w + h * LANES: nope_w + (h + 1) * LANES]
        qr = jnp.where(lane < MLA_ROPE, _rope_pair(slot, tab), 0.0)
        o_ref[:, h * HEAD_SLOT: h * HEAD_SLOT + MLA_NOPE] = acc[:, h * MLA_NOPE:(h + 1) * MLA_NOPE].astype(BF16)
        o_ref[:, h * HEAD_SLOT + MLA_NOPE:(h + 1) * HEAD_SLOT] = qr.astype(BF16)


def mla_q(proj, q_norm, w_q_ext, tab, *, tm=512):
    m = proj.shape[0]
    tm = _tile(m, tm, SUBLANES)
    n = w_q_ext.shape[1]
    return pl.pallas_call(
        _mla_q_kernel,
        grid=(m // tm,),
        in_specs=[pl.BlockSpec((tm, 1024), lambda i: (i, COL_QA // 1024)),
                  pl.BlockSpec((1, MLA_Q_RANK), lambda i: (0, 0)),
                  pl.BlockSpec((MLA_Q_RANK, n), lambda i: (0, 0)),
                  pl.BlockSpec((tm, LANES), lambda i: (i, 0))],
        out_specs=pl.BlockSpec((tm, MLA_HEADS * HEAD_SLOT), lambda i: (i, 0)),
        out_shape=jax.ShapeDtypeStruct((m, MLA_HEADS * HEAD_SLOT), BF16),
        compiler_params=_params(("parallel",)),
        name="mla_q",
    )(proj, q_norm.reshape(1, -1), w_q_ext, tab)


def _mla_kv_kernel(*refs, with_kv):
    kva_ref, kr_ref, g_ref, w_ref, tab_ref = refs[:5]
    outs = refs[-4:] if with_kv else refs[-2:]
    c_ref, krot_ref = outs[:2]
    c = _rms(kva_ref[...], g_ref[...])
    c_ref[...] = c
    raw = kr_ref[...]
    lane = lax.broadcasted_iota(jnp.int32, raw.shape, 1)
    slot = jnp.where(lane < MLA_ROPE, raw,
                     jnp.where(lane < MLA_ROPE + MLA_ROPE // 2, pltpu.roll(raw, MLA_ROPE // 2, 1),
                               pltpu.roll(raw, LANES - MLA_ROPE // 2, 1)))
    krkr = _rope_pair(slot, tab_ref[...])
    krot_ref[...] = krkr[:, :MLA_ROPE]
    if with_kv:
        kfull_ref, v_ref = outs[2:]
        acc = _dot(c, w_ref[...])
        krkr_b = krkr.astype(BF16)
        for h in range(MLA_HEADS):
            kfull_ref[:, h * HEAD_SLOT: h * HEAD_SLOT + MLA_NOPE] = (
                acc[:, h * HEAD_SLOT: h * HEAD_SLOT + MLA_NOPE].astype(BF16))
            kfull_ref[:, h * HEAD_SLOT + MLA_NOPE:(h + 1) * HEAD_SLOT] = krkr_b
            v_ref[h * MLA_V:(h + 1) * MLA_V, :] = acc[:, h * HEAD_SLOT + MLA_NOPE:(h + 1) * HEAD_SLOT].T.astype(BF16)


def mla_kv(proj, row0, n_rows, kv_norm, w_kv_b, tab, layer, lat_prev, kr_prev, *, with_kv, tm=512):
    tm = _tile(math.gcd(n_rows, row0) if row0 else n_rows, tm, SUBLANES)
    rb0 = row0 // tm
    out_specs = [pl.BlockSpec((None, tm, MLA_KV_RANK), lambda i: (layer, i, 0)),
                 pl.BlockSpec((None, tm, MLA_ROPE), lambda i: (layer, i, 0))]
    out_shape = [jax.ShapeDtypeStruct(lat_prev.shape, F32), jax.ShapeDtypeStruct(kr_prev.shape, F32)]
    if with_kv:
        out_specs += [pl.BlockSpec((tm, MLA_HEADS * HEAD_SLOT), lambda i: (i, 0)),
                      pl.BlockSpec((MLA_HEADS * MLA_V, tm), lambda i: (0, i))]
        out_shape += [jax.ShapeDtypeStruct((n_rows, MLA_HEADS * HEAD_SLOT), BF16),
                      jax.ShapeDtypeStruct((MLA_HEADS * MLA_V, n_rows), BF16)]
    return pl.pallas_call(
        functools.partial(_mla_kv_kernel, with_kv=with_kv),
        grid=(n_rows // tm,),
        in_specs=[pl.BlockSpec((tm, MLA_KV_RANK), lambda i: (rb0 + i, COL_KVA // MLA_KV_RANK)),
                  pl.BlockSpec((tm, LANES), lambda i: (rb0 + i, COL_KR // LANES)),
                  pl.BlockSpec((1, MLA_KV_RANK), lambda i: (0, 0)),
                  pl.BlockSpec((MLA_KV_RANK, MLA_HEADS * HEAD_SLOT), lambda i: (0, 0)),
                  pl.BlockSpec((tm, LANES), lambda i: (rb0 + i, 0)),
                  _ANY_SPEC, _ANY_SPEC],
        out_specs=out_specs,
        out_shape=out_shape,
        input_output_aliases={5: 0, 6: 1},
        compiler_params=_params(("parallel",)),
        name="mla_kv",
    )(proj, proj, kv_norm.reshape(1, -1), w_kv_b, tab, lat_prev, kr_prev)


def _flash_kernel(qi_ref, ki_ref, q_ref, k_ref, vt_ref, prev_ref, o_ref, m_ref, l_ref, acc_ref, s_ref, *, tq, tk,
                  hh):
    t = pl.program_id(2)
    qi = qi_ref[t]
    ki = ki_ref[t]
    last_k = ((qi + 1) * tq - 1) // tk

    @pl.when(ki == 0)
    def _():
        m_ref[...] = jnp.full(m_ref.shape, -jnp.inf, F32)
        l_ref[...] = jnp.zeros(l_ref.shape, F32)
        acc_ref[...] = jnp.zeros(acc_ref.shape, F32)

    def update(masked):
        for hd in range(hh):
            s_ref[hd] = lax.dot_general(k_ref[:, hd * HEAD_SLOT:(hd + 1) * HEAD_SLOT],
                                        q_ref[:, hd * HEAD_SLOT:(hd + 1) * HEAD_SLOT], NT_DIMS,
                                        preferred_element_type=F32)
        for hd in range(hh):
            st = s_ref[hd]
            if masked:
                kpos = ki * tk + lax.broadcasted_iota(jnp.int32, (tk, tq), 0)
                qpos = qi * tq + lax.broadcasted_iota(jnp.int32, (tk, tq), 1)
                st = jnp.where(kpos <= qpos, st, -jnp.inf)
            m_prev = m_ref[hd]
            m_new = jnp.maximum(m_prev, jnp.max(st, axis=0, keepdims=True))
            alpha = jnp.exp2(m_prev - m_new)
            p = jnp.exp2(st - m_new)
            l_ref[hd] = alpha * l_ref[hd] + jnp.sum(p, axis=0, keepdims=True)
            acc_ref[hd] = alpha * acc_ref[hd] + jnp.dot(
                vt_ref[hd * MLA_V:(hd + 1) * MLA_V, :], p.astype(BF16), preferred_element_type=F32)
            m_ref[hd] = m_new

    @pl.when((ki + 1) * tk - 1 > qi * tq)
    def _():
        update(True)

    @pl.when((ki + 1) * tk - 1 <= qi * tq)
    def _():
        update(False)

    @pl.when(ki == last_k)
    def _():
        for hd in range(hh):
            o_ref[:, hd * MLA_V:(hd + 1) * MLA_V] = (acc_ref[hd] / l_ref[hd]).T.astype(o_ref.dtype)


def mla_prompt_attention(q_full, k_full, v_t, n_batch, seq, o_prev, *, tq=512, tk=512, hh=4):
    tq, tk = _tile(seq, tq, LANES), _tile(seq, tk, LANES)
    nq, nk = seq // tq, seq // tk
    pairs = [(a, b) for a in range(nq) for b in range(((a + 1) * tq - 1) // tk + 1)]
    qi_tab = jnp.asarray([p[0] for p in pairs], jnp.int32)
    ki_tab = jnp.asarray([p[1] for p in pairs], jnp.int32)
    grid_spec = pltpu.PrefetchScalarGridSpec(
        num_scalar_prefetch=2,
        grid=(n_batch, MLA_HEADS // hh, len(pairs)),
        in_specs=[pl.BlockSpec((tq, hh * HEAD_SLOT), lambda b, h, t, qt, kt: (b * nq + qt[t], h)),
                  pl.BlockSpec((tk, hh * HEAD_SLOT), lambda b, h, t, qt, kt: (b * nk + kt[t], h)),
                  pl.BlockSpec((hh * MLA_V, tk), lambda b, h, t, qt, kt: (h, b * nk + kt[t])),
                  _ANY_SPEC],
        out_specs=pl.BlockSpec((tq, hh * MLA_V), lambda b, h, t, qt, kt: (b * nq + qt[t], h)),
        scratch_shapes=[pltpu.VMEM((hh, 1, tq), F32), pltpu.VMEM((hh, 1, tq), F32),
                        pltpu.VMEM((hh, MLA_V, tq), F32), pltpu.VMEM((hh, tk, tq), F32)],
    )
    return pl.pallas_call(
        functools.partial(_flash_kernel, tq=tq, tk=tk, hh=hh),
        grid_spec=grid_spec,
        out_shape=jax.ShapeDtypeStruct(o_prev.shape, BF16),
        input_output_aliases={5: 0},
        compiler_params=_params(("parallel", "parallel", "arbitrary")),
        name="mla_prompt_flash",
    )(qi_tab, ki_tab, q_full, k_full, v_t, o_prev)


def _absorb_kernel(q_ref, w_ref, o_ref):
    qh = q_ref[...]
    q_lat = _dot_nt(qh[:, :MLA_NOPE], w_ref[...])
    out = jnp.concatenate([q_lat, qh[:, MLA_NOPE:].astype(F32)], axis=-1)
    o_ref[...] = out.reshape(o_ref.shape)


def mla_absorb(q_full, w_kv_b, row0, n_seq, t_len):
    rows = n_seq * t_len
    assert row0 % rows == 0
    width = MLA_KV_RANK + LANES
    return pl.pallas_call(
        _absorb_kernel,
        grid=(MLA_HEADS,),
        in_specs=[pl.BlockSpec((rows, HEAD_SLOT), lambda h: (row0 // rows, h)),
                  pl.BlockSpec((MLA_KV_RANK, MLA_NOPE), lambda h: (0, 2 * h))],
        out_specs=pl.BlockSpec((n_seq, None, t_len, width), lambda h: (0, h, 0, 0)),
        out_shape=jax.ShapeDtypeStruct((n_seq, MLA_HEADS, t_len, width), F32),
        compiler_params=_params(("parallel",)),
        name="mla_absorb",
    )(q_full, w_kv_b)


def _paged_attn_kernel(pt_ref, q_ref, cnew_ref, krnew_ref, clat_hbm, ckr_hbm, o_ref,
                       cf32, krf32, cbuf, m_ref, l_ref, acc_ref, csem, krsem, *, G, layer, t_len, n_steps, n_seq):
    b = pl.program_id(0)
    g = pl.program_id(1)
    step = b * n_steps + g
    slot = step % 2
    page = cbuf.shape[0] // G

    def page_copies(bb, gg, sl):
        copies = []
        for i in range(G):
            pidx = pt_ref[bb, gg * G + i]
            copies.append(pltpu.make_async_copy(
                clat_hbm.at[layer, pidx], cf32.at[sl, pl.ds(i * page, page), :], csem.at[sl]))
            copies.append(pltpu.make_async_copy(
                ckr_hbm.at[layer, pidx], krf32.at[sl, :, pl.ds(i * page, page)], krsem.at[sl]))
        return copies

    @pl.when(step == 0)
    def _():
        for cp in page_copies(b, g, slot):
            cp.start()

    @pl.when(step + 1 < n_seq * n_steps)
    def _():
        wrap = g + 1 == n_steps
        for cp in page_copies(jnp.where(wrap, b + 1, b), jnp.where(wrap, 0, g + 1), 1 - slot):
            cp.start()

    q = q_ref[0]
    q_lat = q[:, :MLA_KV_RANK]
    q_rope = q[:, MLA_KV_RANK:MLA_KV_RANK + MLA_ROPE]

    @pl.when(g == 0)
    def _():
        c_new = cnew_ref[...]
        kr_new = krnew_ref[...]
        tq = lax.broadcasted_iota(jnp.int32, (q.shape[0], 1), 0) % t_len
        cols = []
        for j in range(t_len):
            sj = (jnp.sum(q_lat * c_new[j:j + 1, :], axis=-1, keepdims=True)
                  + jnp.sum(q_rope * kr_new[j:j + 1, :], axis=-1, keepdims=True))
            cols.append(jnp.where(tq >= j, sj, -jnp.inf))
        m0 = cols[0]
        for j in range(1, t_len):
            m0 = jnp.maximum(m0, cols[j])
        l0 = jnp.zeros_like(m0)
        acc0 = jnp.zeros(acc_ref.shape, F32)
        for j in range(t_len):
            pj = jnp.exp2(cols[j] - m0)
            l0 = l0 + pj
            acc0 = acc0 + pj * c_new[j:j + 1, :]
        m_ref[...] = m0
        l_ref[...] = l0
        acc_ref[...] = acc0

    pltpu.make_async_copy(cf32.at[slot], cf32.at[slot], csem.at[slot]).wait()
    pltpu.make_async_copy(krf32.at[slot], krf32.at[slot], krsem.at[slot]).wait()

    q_lat_b = q_lat.astype(BF16)
    q_rope_b = q_rope.astype(BF16)
    s_parts = []
    for i in range(0, G, 2):
        cb = cf32[slot, pl.ds(i * page, 2 * page), :].astype(BF16)
        cbuf[i * page:(i + 2) * page, :] = cb
        krt = krf32[slot, :, pl.ds(i * page, 2 * page)].astype(BF16)
        s_parts.append(lax.dot_general(q_lat_b, cb, NT_DIMS, preferred_element_type=F32)
                       + jnp.dot(q_rope_b, krt, preferred_element_type=F32))
    s = jnp.concatenate(s_parts, axis=1)
    m_prev = m_ref[...]
    m_new = jnp.maximum(m_prev, jnp.max(s, axis=-1, keepdims=True))
    alpha = jnp.exp2(m_prev - m_new)
    p = jnp.exp2(s - m_new)
    l_ref[...] = alpha * l_ref[...] + jnp.sum(p, axis=-1, keepdims=True)
    acc_ref[...] = alpha * acc_ref[...] + jnp.dot(p.astype(BF16), cbuf[...], preferred_element_type=F32)
    m_ref[...] = m_new

    @pl.when(g == n_steps - 1)
    def _():
        o_ref[0] = acc_ref[...] / l_ref[...]


def mla_paged_attention(q_abs, lat_s, kr_s, cache_latent, cache_krope_t, layer, page_table, *, G=32):
    n_seq, heads, t_len, width = q_abs.shape
    n_pages = page_table.shape[1]
    page = cache_latent.shape[2]
    G = _tile(n_pages, G, 2)
    n_steps = n_pages // G
    rows = heads * t_len
    q3 = q_abs.reshape(n_seq, rows, width)
    grid_spec = pltpu.PrefetchScalarGridSpec(
        num_scalar_prefetch=1,
        grid=(n_seq, n_steps),
        in_specs=[pl.BlockSpec((1, rows, width), lambda b, g, pt: (b, 0, 0)),
                  pl.BlockSpec((None, t_len, MLA_KV_RANK), lambda b, g, pt: (layer, b, 0)),
                  pl.BlockSpec((None, t_len, MLA_ROPE), lambda b, g, pt: (layer, b, 0)),
                  _ANY_SPEC, _ANY_SPEC],
        out_specs=pl.BlockSpec((1, rows, MLA_KV_RANK), lambda b, g, pt: (b, 0, 0)),
        scratch_shapes=[pltpu.VMEM((2, G * page, MLA_KV_RANK), F32),
                        pltpu.VMEM((2, MLA_ROPE, G * page), F32),
                        pltpu.VMEM((G * page, MLA_KV_RANK), BF16),
                        pltpu.VMEM((rows, 1), F32), pltpu.VMEM((rows, 1), F32),
                        pltpu.VMEM((rows, MLA_KV_RANK), F32),
                        pltpu.SemaphoreType.DMA((2,)), pltpu.SemaphoreType.DMA((2,))],
    )
    return pl.pallas_call(
        functools.partial(_paged_attn_kernel, G=G, layer=layer, t_len=t_len, n_steps=n_steps, n_seq=n_seq),
        grid_spec=grid_spec,
        out_shape=jax.ShapeDtypeStruct((n_seq, rows, MLA_KV_RANK), F32),
        compiler_params=_params(("arbitrary", "arbitrary")),
        name="mla_paged_attention",
    )(page_table, q3, lat_s, kr_s, cache_latent, cache_krope_t)


def _uv_kernel(o_ref, w_ref, prev_ref, out_ref):
    o = o_ref[...]
    out_ref[...] = _dot(o.reshape(o.shape[0] * o.shape[1], o.shape[2]), w_ref[...]).astype(out_ref.dtype)


def mla_uv(o_lat, w_kv_b, row0, n_seq, t_len, ob_prev):
    o4 = o_lat.reshape(n_seq, MLA_HEADS, t_len, MLA_KV_RANK)
    rows = n_seq * t_len
    assert row0 % rows == 0
    return pl.pallas_call(
        _uv_kernel,
        grid=(MLA_HEADS,),
        in_specs=[pl.BlockSpec((n_seq, None, t_len, MLA_KV_RANK), lambda h: (0, h, 0, 0)),
                  pl.BlockSpec((MLA_KV_RANK, MLA_V), lambda h: (0, 2 * h + 1)),
                  _ANY_SPEC],
        out_specs=pl.BlockSpec((rows, MLA_V), lambda h: (row0 // rows, h)),
        out_shape=jax.ShapeDtypeStruct(ob_prev.shape, BF16),
        input_output_aliases={2: 0},
        compiler_params=_params(("parallel",)),
        name="mla_uv",
    )(o4, w_kv_b, ob_prev)


def _gmlp_kernel(*refs, emit_vn):
    gu_ref, gv_ref, g_ref, w_ref, bias_ref = refs[:5]
    o_ref = refs[-2] if emit_vn else refs[-1]
    vn = _rms(_gelu(gv_ref[...]), g_ref[...])
    if emit_vn:
        refs[-1][...] = vn
    u = _gelu(gu_ref[...])
    for grp in range(CM_GROUPS):
        sl = slice(grp * LANES, (grp + 1) * LANES)
        z = _dot(w_ref[grp], vn[:, sl]) + bias_ref[:, sl]
        o_ref[:, sl] = (u[:, sl] * z).astype(o_ref.dtype)


def gmlp(proj, row0, n_rows, v_norm, w_mix, bias_full, oc_prev, *, emit_vn):
    width = CM_GROUPS * LANES
    assert row0 % CM_CHUNK == 0 and n_rows % CM_CHUNK == 0
    rb0 = row0 // CM_CHUNK
    in_specs = [pl.BlockSpec((CM_CHUNK, width), lambda i: (rb0 + i, COL_GU // width)),
                pl.BlockSpec((CM_CHUNK, width), lambda i: (rb0 + i, COL_GV // width)),
                pl.BlockSpec((1, width), lambda i: (0, 0)),
                pl.BlockSpec((CM_GROUPS, CM_CHUNK, CM_CHUNK), lambda i: (0, 0, 0)),
                pl.BlockSpec((CM_CHUNK, width), lambda i: (0, 0))]
    args = [proj, proj, v_norm.reshape(1, -1), w_mix, bias_full]
    aliases = {}
    if oc_prev is not None:
        aliases[len(args)] = 0
        in_specs.append(_ANY_SPEC)
        args.append(oc_prev)
    out_specs = [pl.BlockSpec((CM_CHUNK, width), lambda i: (rb0 + i, 0))]
    out_shape = [jax.ShapeDtypeStruct((proj.shape[0], width), BF16)]
    if emit_vn:
        out_specs.append(pl.BlockSpec((CM_CHUNK, width), lambda i: (i, 0)))
        out_shape.append(jax.ShapeDtypeStruct((n_rows, width), F32))
    return pl.pallas_call(
        functools.partial(_gmlp_kernel, emit_vn=emit_vn),
        grid=(n_rows // CM_CHUNK,),
        in_specs=in_specs,
        out_specs=out_specs,
        out_shape=out_shape,
        input_output_aliases=aliases,
        compiler_params=_params(("parallel",)),
        name="gmlp",
    )(*args)


def _merge_kernel(oa_ref, ob_ref, oc_ref, g0_ref, g1_ref, g2_ref, w_ref, o_ref):
    acc = None
    for br, (b_ref, g_ref) in enumerate(((oa_ref, g0_ref), (ob_ref, g1_ref), (oc_ref, g2_ref))):
        part = g_ref[...] * jnp.dot(b_ref[...], w_ref[br], preferred_element_type=F32)
        acc = part if acc is None else acc + part
    o_ref[...] = acc.astype(o_ref.dtype)


def branch_merge(o_a, o_b, o_c, gates, w_branch, *, tm=1024, tn=512):
    m, bw = o_a.shape
    d = w_branch.shape[2]
    tm, tn = _tile(m, tm, SUBLANES), _tile(d, tn, LANES)
    nt = d // tn
    bspec = pl.BlockSpec((tm, bw), lambda i, j: (i, 0))

    def gspec(br):
        return pl.BlockSpec((tm, tn), lambda i, j: (i, br * nt + j))

    return pl.pallas_call(
        _merge_kernel,
        grid=(m // tm, nt),
        in_specs=[bspec, bspec, bspec, gspec(0), gspec(1), gspec(2),
                  pl.BlockSpec((N_BRANCH, bw, tn), lambda i, j: (0, 0, j))],
        out_specs=pl.BlockSpec((tm, tn), lambda i, j: (i, j)),
        out_shape=jax.ShapeDtypeStruct((m, d), BF16),
        compiler_params=_params(("parallel", "arbitrary")),
        name="branch_merge",
    )(o_a, o_b, o_c, gates, gates, gates, w_branch)


def _out_proj_kernel(mix_ref, x_ref, w_ref, g_ref, gn_ref, o_ref, hn_ref):
    y = jnp.dot(mix_ref[...], w_ref[...], preferred_element_type=F32)
    x_new = x_ref[...] + _rms(y, g_ref[...])
    o_ref[...] = x_new
    hn_ref[...] = _rms(x_new, gn_ref[...]).astype(BF16)


def out_proj(mix, x, w_out, g, g_next, *, tm=512):
    m, d = x.shape
    tm = _tile(m, tm, SUBLANES)
    row = pl.BlockSpec((tm, d), lambda i: (i, 0))
    vec = pl.BlockSpec((1, d), lambda i: (0, 0))
    return pl.pallas_call(
        _out_proj_kernel,
        grid=(m // tm,),
        in_specs=[row, row, pl.BlockSpec((d, d), lambda i: (0, 0)), vec, vec],
        out_specs=[row, row],
        out_shape=[jax.ShapeDtypeStruct((m, d), F32), jax.ShapeDtypeStruct((m, d), BF16)],
        compiler_params=_params(("parallel",)),
        name="out_proj",
    )(mix, x, w_out, g.reshape(1, -1), g_next.reshape(1, -1))


def _ffn_up_kernel(*refs, t_len, per_seq_prev, tiles_per_seq, kc, rb):
    h_ref, wg_ref, wv_ref, cw_ref, cb_ref = refs[:5]
    if per_seq_prev:
        pg_ref, pv_ref = refs[5:7]
        act_ref, tailg_ref, tailv_ref, wb_ref = refs[-4:]
    else:
        act_ref, tailg_ref, tailv_ref, wb_ref, carry_ref, shift_ref = refs[-6:]
    i = pl.program_id(1)
    tm, tn = act_ref.shape

    @pl.when(i == 0)
    def _():
        wb_ref[0] = wg_ref[...].astype(BF16)
        wb_ref[1] = wv_ref[...].astype(BF16)
        if not per_seq_prev:
            carry_ref[...] = jnp.zeros(carry_ref.shape, F32)

    if not per_seq_prev:
        seq_start = (i % tiles_per_seq) == 0

    def conv(u, which, sl, p):
        if per_seq_prev:
            u = u.reshape(rb // t_len, t_len, u.shape[-1])
            tpos = lax.broadcasted_iota(jnp.int32, (1, t_len, 1), 1)
            back1 = p[:, 1:2, :]
            back2 = jnp.where(tpos == 0, p[:, 0:1, :], p[:, 1:2, :])
            axis = 1
            m1 = jnp.where(tpos == 0, back1, pltpu.roll(u, 1, axis))
            m2 = jnp.where(tpos <= 1, back2, pltpu.roll(u, 2, axis))
        else:
            shift_ref[which, 0:SUBLANES, :] = p
            shift_ref[which, SUBLANES:SUBLANES + rb, :] = u
            m1 = shift_ref[which, SUBLANES - 1:SUBLANES - 1 + rb, :]
            m2 = shift_ref[which, SUBLANES - 2:SUBLANES - 2 + rb, :]
        y = (cb_ref[which:which + 1, sl] + cw_ref[which, 0:1, sl] * m2 + cw_ref[which, 1:2, sl] * m1
             + cw_ref[which, 2:3, sl] * u)
        return y.reshape(rb, y.shape[-1])

    tails = []
    for c0 in range(0, tn, kc):
        sl = slice(c0, c0 + kc)
        wg = wb_ref[0, :, sl]
        wv = wb_ref[1, :, sl]
        if not per_seq_prev:
            pg = jnp.where(seq_start, 0.0, carry_ref[0][:, sl])
            pv = jnp.where(seq_start, 0.0, carry_ref[1][:, sl])
        for r0 in range(0, tm, rb):
            hr = h_ref[r0:r0 + rb, :]
            ug = jnp.dot(hr, wg, preferred_element_type=F32)
            uv = jnp.dot(hr, wv, preferred_element_type=F32)
            if per_seq_prev:
                s0, s1 = r0 // t_len, (r0 + rb) // t_len
                pg = pg_ref[s0:s1, :, sl]
                pv = pv_ref[s0:s1, :, sl]
            act = _gelu(conv(ug, 0, sl, pg)) * conv(uv, 1, sl, pv)
            act_ref[r0:r0 + rb, sl] = act.astype(act_ref.dtype)
            if per_seq_prev:
                tailg_ref[s0:s1, :, sl] = ug.reshape(rb // t_len, t_len, kc)[:, t_len - (CONV_W - 1):, :]
                tailv_ref[s0:s1, :, sl] = uv.reshape(rb // t_len, t_len, kc)[:, t_len - (CONV_W - 1):, :]
            else:
                pg, pv = ug[rb - SUBLANES:, :], uv[rb - SUBLANES:, :]
        if not per_seq_prev:
            tailg_ref[:, sl] = pg
            tailv_ref[:, sl] = pv
            tails.append((sl, pg, pv))
    for sl, tg, tv in tails:
        carry_ref[0, :, sl] = tg
        carry_ref[1, :, sl] = tv


def ffn_up(hn, w_up_all, layer, conv_w2, conv_b2, prev_state, row0, n_rows, t_len, act_prev, *, tm=1024,
           tn=512, kc=256, rb=512):
    d = hn.shape[1]
    d_ff = w_up_all.shape[2] // 2
    per_seq_prev = t_len == SUBLANES
    tm = _tile(n_rows if per_seq_prev else t_len, tm, SUBLANES)
    tn = _tile(d_ff, tn, LANES)
    kc = _tile(tn, kc, LANES)
    rb = _tile(tm, rb, t_len if per_seq_prev else SUBLANES)
    n_j = d_ff // tn
    n_i = n_rows // tm
    assert row0 % tm == 0 and (tm % t_len == 0 or t_len % tm == 0)
    rb0 = row0 // tm
    in_specs = [pl.BlockSpec((tm, d), lambda j, i: (rb0 + i, 0)),
                pl.BlockSpec((None, d, tn), lambda j, i: (layer, 0, j)),
                pl.BlockSpec((None, d, tn), lambda j, i: (layer, 0, n_j + j)),
                pl.BlockSpec((2, CONV_W, tn), lambda j, i: (0, 0, j)),
                pl.BlockSpec((2, tn), lambda j, i: (0, j))]
    args = [hn, w_up_all, w_up_all, conv_w2, conv_b2]
    scratch = [pltpu.VMEM((2, d, tn), BF16)]
    if per_seq_prev:
        ns = tm // t_len
        in_specs += [pl.BlockSpec((None, ns, CONV_W - 1, tn), lambda j, i: (layer, i, 0, j)),
                     pl.BlockSpec((None, ns, CONV_W - 1, tn), lambda j, i: (layer, i, 0, n_j + j))]
        args += [prev_state, prev_state]
        tail_spec = pl.BlockSpec((ns, CONV_W - 1, tn), lambda j, i: (i, 0, j))
        tail_shape = jax.ShapeDtypeStruct((n_rows // t_len, CONV_W - 1, d_ff), F32)
    else:
        scratch.append(pltpu.VMEM((2, SUBLANES, tn), F32))
        scratch.append(pltpu.VMEM((2, SUBLANES + rb, kc), F32))
        tail_spec = pl.BlockSpec((None, SUBLANES, tn), lambda j, i: (i, 0, j))
        tail_shape = jax.ShapeDtypeStruct((n_i, SUBLANES, d_ff), F32)
    aliases = {len(args): 0}
    in_specs.append(_ANY_SPEC)
    args.append(act_prev)
    return pl.pallas_call(
        functools.partial(_ffn_up_kernel, t_len=t_len, per_seq_prev=per_seq_prev,
                          tiles_per_seq=max(t_len // tm, 1), kc=kc, rb=rb),
        grid=(n_j, n_i),
        in_specs=in_specs,
        out_specs=[pl.BlockSpec((tm, tn), lambda j, i: (rb0 + i, j)), tail_spec, tail_spec],
        out_shape=[jax.ShapeDtypeStruct(act_prev.shape, BF16), tail_shape, tail_shape],
        scratch_shapes=scratch,
        input_output_aliases=aliases,
        compiler_params=_params(("arbitrary", "arbitrary")),
        name="ffn_up",
    )(*args)


def _ffn_down_kernel(a_ref, w_ref, x_ref, g_ref, o_ref, acc_ref, *, n_k):
    kk = pl.program_id(1)
    part = jnp.dot(a_ref[...], w_ref[...], preferred_element_type=F32)
    if n_k == 1:
        o_ref[...] = x_ref[...] + _rms(part, g_ref[...])
        return

    @pl.when(kk == 0)
    def _():
        acc_ref[...] = part

    @pl.when((kk > 0) & (kk < n_k - 1))
    def _():
        acc_ref[...] += part

    @pl.when(kk == n_k - 1)
    def _():
        o_ref[...] = x_ref[...] + _rms(acc_ref[...] + part, g_ref[...])


def ffn_down(act, w_down, x, g, *, tm=512, tk=2816):
    d_ff, d = w_down.shape
    m = x.shape[0]
    tm = _tile(m, tm, SUBLANES)
    tk = _tile(d_ff, tk, LANES)
    n_k = d_ff // tk
    return pl.pallas_call(
        functools.partial(_ffn_down_kernel, n_k=n_k),
        grid=(m // tm, n_k),
        in_specs=[pl.BlockSpec((tm, tk), lambda i, k: (i, k)),
                  pl.BlockSpec((tk, d), lambda i, k: (k, 0)),
                  pl.BlockSpec((tm, d), lambda i, k: (i, 0)),
                  pl.BlockSpec((1, d), lambda i, k: (0, 0))],
        out_specs=pl.BlockSpec((tm, d), lambda i, k: (i, 0)),
        out_shape=jax.ShapeDtypeStruct(x.shape, F32),
        scratch_shapes=[pltpu.VMEM((tm, d), F32)],
        compiler_params=_params(("parallel", "arbitrary")),
        name="ffn_down",
    )(act, w_down, x, g.reshape(1, -1))


def _ple_kernel(x_ref, pe_ref, g_ref, wg_ref, wp_ref, gn_ref, *o_refs, n_first):
    x = x_ref[...]
    gate = _sigmoid(_dot(_rms(x, g_ref[...]), wg_ref[...]))
    y = x + _dot(pe_ref[...], wp_ref[...]) * gate
    if n_first is None:
        o_refs[0][...] = y
        o_refs[1][...] = _rms(y, gn_ref[...]).astype(BF16)
        return
    i = pl.program_id(0)

    @pl.when(i < n_first)
    def _():
        o_refs[0][...] = y

    @pl.when(i >= n_first)
    def _():
        o_refs[1][...] = y


def ple(x, pe, g, w_gate, w_proj, g_next, *, split_rows=None, tm=512):
    m, d = x.shape
    pd = pe.shape[1]
    tm = _tile(m if split_rows is None else math.gcd(split_rows, m - split_rows), tm, SUBLANES)
    if split_rows is None:
        n_first = None
        out_specs = [pl.BlockSpec((tm, d), lambda i: (i, 0)), pl.BlockSpec((tm, d), lambda i: (i, 0))]
        out_shape = [jax.ShapeDtypeStruct((m, d), F32), jax.ShapeDtypeStruct((m, d), BF16)]
    else:
        n_first = split_rows // tm
        out_specs = [pl.BlockSpec((tm, d), lambda i: (jnp.minimum(i, n_first - 1), 0)),
                     pl.BlockSpec((tm, d), lambda i: (jnp.maximum(i - n_first, 0), 0))]
        out_shape = [jax.ShapeDtypeStruct((split_rows, d), F32), jax.ShapeDtypeStruct((m - split_rows, d), F32)]
    return pl.pallas_call(
        functools.partial(_ple_kernel, n_first=n_first),
        grid=(m // tm,),
        in_specs=[pl.BlockSpec((tm, d), lambda i: (i, 0)),
                  pl.BlockSpec((tm, pd), lambda i: (i, 0)),
                  pl.BlockSpec((1, d), lambda i: (0, 0)),
                  pl.BlockSpec((d, d), lambda i: (0, 0)),
                  pl.BlockSpec((pd, d), lambda i: (0, 0)),
                  pl.BlockSpec((1, d), lambda i: (0, 0))],
        out_specs=out_specs,
        out_shape=out_shape,
        compiler_params=_params(("arbitrary",)),
        name="ple",
    )(x, pe, g.reshape(1, -1), w_gate, w_proj, g_next.reshape(1, -1))


def _rope_table(pos):
    inv = 1.0 / (ROPE_THETA ** (jnp.arange(0, MLA_ROPE, 2, dtype=F32) / MLA_ROPE))
    ang = pos.astype(F32)[:, None] * inv[None, :]
    cos, sin = jnp.cos(ang), jnp.sin(ang)
    return jnp.concatenate([cos, cos, -sin, sin], axis=-1)


def _swap_halves(w, width):
    half = width // 2
    return jnp.concatenate([w[..., half:], w[..., :half]], axis=-1)


def kernel(x_prompt, x_sample, cache_latent, cache_krope, state_hgrn, state_ffn_conv, page_table, p_prompt, p_sample, ln_mix_pre, ln_mix_post, ln_ffn_pre, ln_ffn_post, ln_ple, w_in, hgrn_lower_bounds, hgrn_norm, mla_q_norm, mla_kv_norm, w_q_b, w_kv_b, cm_v_norm, cm_w_s, cm_b, w_merge_gate, w_branch, w_out, w_ffn_up, ffn_conv_w, ffn_conv_b, w_ffn_down, w_ple_proj, w_ple_gate):
    n_b, seq, d = x_prompt.shape
    n_db, t_len, _ = x_sample.shape
    depth = w_in.shape[0]
    d_ff = w_ffn_down.shape[1]
    page = cache_latent.shape[2]
    past_len = page_table.shape[1] * page
    rows_p, rows_s = n_b * seq, n_db * t_len
    rows = rows_p + rows_s
    hk = HG_HEADS * HG_KEY

    lb_soft = jax.nn.softmax(hgrn_lower_bounds.astype(F32), axis=0)
    lbs = jnp.concatenate([jnp.zeros_like(lb_soft[:1]), jnp.cumsum(lb_soft[1:], axis=0)], axis=0)

    tab = jnp.concatenate([jnp.tile(_rope_table(jnp.arange(seq)), (n_b, 1)),
                           jnp.tile(_rope_table(past_len + jnp.arange(t_len)), (n_db, 1))], axis=0)
    cache_krope_t = jnp.swapaxes(cache_krope, 2, 3)
    w_in_t = jnp.swapaxes(w_in, 1, 2)

    x = jnp.concatenate([x_prompt.reshape(rows_p, d), x_sample.reshape(rows_s, d)], axis=0)
    tri = jnp.tril(jnp.ones((CM_CHUNK, CM_CHUNK), F32))
    seq_per_chunk = CM_CHUNK // t_len

    lat_p = jnp.zeros((depth, rows_p, MLA_KV_RANK), F32)
    kr_p = jnp.zeros((depth, rows_p, MLA_ROPE), F32)
    lat_s = jnp.zeros((depth, rows_s, MLA_KV_RANK), F32)
    kr_s = jnp.zeros((depth, rows_s, MLA_ROPE), F32)
    hg_p = jnp.zeros((depth, n_b, HG_HEADS, HG_KEY, HG_VAL), F32)
    hg_s = jnp.zeros((depth, n_db, HG_HEADS, HG_KEY, HG_VAL), F32)
    cv_p, cv_s, v_s = [], [], []

    for i in range(depth):
        wq = w_q_b[i].reshape(MLA_Q_RANK, MLA_HEADS, MLA_NOPE + MLA_ROPE)
        wq_rope = wq[..., MLA_NOPE:]
        w_q_ext = jnp.concatenate(
            [wq[..., :MLA_NOPE].reshape(MLA_Q_RANK, -1),
             jnp.concatenate([wq_rope, _swap_halves(wq_rope, MLA_ROPE)], axis=-1).reshape(MLA_Q_RANK, -1)],
            axis=1).astype(BF16)
        w_kvb = w_kv_b[i].astype(BF16)
        lb = lbs[i]
        lb3 = jnp.stack([lb, jnp.log(lb), jnp.log1p(-lb)], axis=0)
        w_mix_p = cm_w_s[i] * tri
        w_mix_s = jnp.einsum("ab,gts->gatbs", jnp.eye(seq_per_chunk, dtype=F32),
                             w_mix_p[:, :t_len, :t_len]).reshape(CM_GROUPS, CM_CHUNK, CM_CHUNK)
        bias_p = jnp.repeat(cm_b[i].T, LANES, axis=1)
        bias_s = jnp.tile(bias_p[:t_len], (seq_per_chunk, 1))
        conv_w2 = ffn_conv_w[i].reshape(CONV_W, 2, d_ff).transpose(1, 0, 2)
        conv_b2 = ffn_conv_b[i].reshape(2, d_ff)

        zeros_branch = jnp.zeros((rows, HG_HEADS * HG_VAL), BF16)
        if i == 0:
            hn = rms_cast(x, ln_mix_pre[i])
        proj = matmul_ws_t(hn, w_in_t, i, IN_TILE_SRC, tn=IN_TILE)
        gates = matmul_ws(hn, w_merge_gate, i, act="sigmoid")

        o_a, hg_p = hgrn2(proj, 0, n_b, seq, lb3, hgrn_norm[i], None, i, zeros_branch, hg_p,
                          C=_tile(seq, 128, SUBLANES), nb=1, hh=4)
        o_a, hg_s = hgrn2(proj, rows_p, n_db, t_len, lb3, hgrn_norm[i], state_hgrn, i, o_a, hg_s,
                          C=t_len, nb=_tile(n_db, LANES // t_len), hh=2)

        q_full = mla_q(proj, mla_q_norm[i], w_q_ext, tab)
        lat_p, kr_p, k_full, v_all = mla_kv(proj, 0, rows_p, mla_kv_norm[i], w_kvb, tab, i, lat_p, kr_p,
                                            with_kv=True)
        lat_s, kr_s = mla_kv(proj, rows_p, rows_s, mla_kv_norm[i], w_kvb, tab, i, lat_s, kr_s, with_kv=False)
        o_b = mla_prompt_attention(q_full, k_full, v_all, n_b, seq, zeros_branch)
        q_abs = mla_absorb(q_full, w_kvb, rows_p, n_db, t_len)
        o_lat = mla_paged_attention(q_abs, lat_s, kr_s, cache_latent, cache_krope_t, i, page_table)
        o_b = mla_uv(o_lat, w_kvb, rows_p, n_db, t_len, o_b)

        (o_c,) = gmlp(proj, 0, rows_p, cm_v_norm[i], w_mix_p, bias_p, zeros_branch, emit_vn=False)
        o_c, vn_s = gmlp(proj, rows_p, rows_s, cm_v_norm[i], w_mix_s, bias_s, o_c, emit_vn=True)

        mix = branch_merge(o_a, o_b, o_c, gates, w_branch[i].astype(BF16))
        x, hn_ffn = out_proj(mix, x, w_out[i].astype(BF16), ln_mix_post[i], ln_ffn_pre[i])

        act, tg_p, tv_p = ffn_up(hn_ffn, w_ffn_up, i, conv_w2, conv_b2, None, 0, rows_p, seq,
                                 jnp.zeros((rows, d_ff), BF16))
        act, tg_s, tv_s = ffn_up(hn_ffn, w_ffn_up, i, conv_w2, conv_b2, state_ffn_conv, rows_p, rows_s, t_len, act)
        x = ffn_down(act, w_ffn_down[i].astype(BF16), x, ln_ffn_post[i])

        pe = jnp.concatenate([p_prompt[i].reshape(rows_p, -1), p_sample[i].reshape(rows_s, -1)], axis=0)
        last = i == depth - 1
        x = ple(x, pe, ln_ple[i], w_ple_gate[i].astype(BF16), w_ple_proj[i].astype(BF16),
                ln_ple[i] if last else ln_mix_pre[i + 1], split_rows=rows_p if last else None)
        if not last:
            x, hn = x

        tails_p = jnp.concatenate([tg_p, tv_p], axis=-1).reshape(n_b, -1, SUBLANES, 2 * d_ff)
        cv_p.append(tails_p[:, -1, SUBLANES - (CONV_W - 1):, :])
        cv_s.append(jnp.concatenate([tg_s, tv_s], axis=-1))
        v_s.append(vn_s.reshape(n_db, t_len, -1))

    y_p, y_s = x
    return (y_p.reshape(n_b, seq, d), y_s.reshape(n_db, t_len, d),
            lat_p.reshape(depth, n_b, seq, -1), kr_p.reshape(depth, n_b, seq, -1),
            lat_s.reshape(depth, n_db, t_len, -1), kr_s.reshape(depth, n_db, t_len, -1),
            hg_p, hg_s, jnp.stack(cv_p), jnp.stack(cv_s), jnp.stack(v_s))
```

```python
import functools
import math

import jax
import jax.numpy as jnp
import numpy as np
from jax import lax
from jax.experimental import pallas as pl
from jax.experimental.pallas import tpu as pltpu

F32 = jnp.float32
BF16 = jnp.bfloat16
HIGHEST = lax.Precision.HIGHEST

HG_HEADS = 8
HG_KEY = 128
HG_VAL = 128
MLA_HEADS = 8
MLA_NOPE = 128
MLA_ROPE = 64
MLA_V = 128
MLA_Q_RANK = 768
MLA_KV_RANK = 512
MLA_SCALE = (MLA_NOPE + MLA_ROPE) ** -0.5
ROPE_THETA = 10000.0
CM_GROUPS = 8
CM_CHUNK = 128
N_BRANCH = 3
CONV_W = 3
EPS = 1e-6
LOG2E = math.log2(math.e)

LANES = 128
SUBLANES = 8
VMEM_LIMIT = 56 * 1024 * 1024
HEAD_SLOT = 2 * LANES

IN_TILE = 1024
IN_TILE_SRC = (0, 1024, 2048, 3072, 5440, 6464, 4096, 4864)
COL_GU, COL_GV = 4096, 5120
COL_QA = 6144
COL_KVA = 7168
COL_KR = 7680

NT_DIMS = (((1,), (1,)), ((), ()))
TN_DIMS = (((0,), (0,)), ((), ()))


def _params(sem):
    return pltpu.CompilerParams(dimension_semantics=sem, vmem_limit_bytes=VMEM_LIMIT)


def _rms(x, g):
    ms = jnp.mean(x * x, axis=-1, keepdims=True)
    return x * lax.rsqrt(ms + EPS) * g


def _gelu(x):
    return 0.5 * x * (1.0 + jnp.tanh(math.sqrt(2.0 / math.pi) * (x + 0.044715 * (x * x * x))))


def _sigmoid(x):
    return 0.5 * jnp.tanh(0.5 * x) + 0.5


def _silu(x):
    return x * _sigmoid(x)


def _tile(n, pref, unit=1):
    t = (min(pref, n) // unit) * unit
    while t > unit and n % t:
        t -= unit
    assert t > 0 and n % t == 0, (n, pref, unit)
    return t


def _dot(a, b):
    return jnp.dot(a.astype(BF16), b.astype(BF16), preferred_element_type=F32)


def _dot_nt(a, b):
    return lax.dot_general(a.astype(BF16), b.astype(BF16), NT_DIMS, preferred_element_type=F32)


_ANY_SPEC = pl.BlockSpec(memory_space=pl.ANY)


def _two_source_specs(xs, tm):
    n_first = xs[0].shape[0] // tm
    k = xs[0].shape[1]
    return n_first, [pl.BlockSpec((tm, k), lambda i, *_: (jnp.minimum(i, n_first - 1), 0)),
                     pl.BlockSpec((tm, k), lambda i, *_: (jnp.maximum(i - n_first, 0), 0))]


def _rms_cast_kernel(xa_ref, xb_ref, g_ref, o_ref, *, n_first):
    x = jnp.where(pl.program_id(0) < n_first, xa_ref[...], xb_ref[...])
    o_ref[...] = _rms(x, g_ref[...]).astype(BF16)


def rms_cast(xs, g, *, tm=1024):
    k = xs[0].shape[1]
    m = xs[0].shape[0] + xs[1].shape[0]
    tm = _tile(math.gcd(xs[0].shape[0], xs[1].shape[0]), tm, SUBLANES)
    n_first, x_specs = _two_source_specs(xs, tm)
    return pl.pallas_call(
        functools.partial(_rms_cast_kernel, n_first=n_first),
        grid=(m // tm,),
        in_specs=x_specs + [pl.BlockSpec((1, k), lambda i: (0, 0))],
        out_specs=pl.BlockSpec((tm, k), lambda i: (i, 0)),
        out_shape=jax.ShapeDtypeStruct((m, k), BF16),
        compiler_params=_params(("parallel",)),
        name="rms_cast",
    )(xs[0], xs[1], g.reshape(1, k))


def _matmul_ws_kernel(*refs, transposed, act):
    h_ref, w_ref, o_ref, wb_ref = refs[-4:]

    @pl.when(pl.program_id(1) == 0)
    def _():
        wb_ref[...] = (w_ref[0] if transposed else w_ref[...]).astype(BF16)

    if transposed:
        acc = lax.dot_general(h_ref[...], wb_ref[...], NT_DIMS, preferred_element_type=F32)
    else:
        acc = jnp.dot(h_ref[...], wb_ref[...], preferred_element_type=F32)
    if act == "sigmoid":
        acc = _sigmoid(acc)
    o_ref[...] = acc


def matmul_ws(hn, w_all, layer, *, act=None, tm=1024, tn=1024):
    m, k = hn.shape
    n = w_all.shape[2]
    tm, tn = _tile(m, tm, SUBLANES), _tile(n, tn, LANES)
    return pl.pallas_call(
        functools.partial(_matmul_ws_kernel, transposed=False, act=act),
        grid=(n // tn, m // tm),
        in_specs=[pl.BlockSpec((tm, k), lambda j, i: (i, 0)),
                  pl.BlockSpec((None, k, tn), lambda j, i: (layer, 0, j))],
        out_specs=pl.BlockSpec((tm, tn), lambda j, i: (i, j)),
        out_shape=jax.ShapeDtypeStruct((m, n), F32),
        scratch_shapes=[pltpu.VMEM((k, tn), BF16)],
        compiler_params=_params(("arbitrary", "arbitrary")),
        name="matmul_ws",
    )(hn, w_all)


def matmul_ws_t(hn, w_t, layer, col_offsets, *, tm=1024, tn=1024):
    m, k = hn.shape
    tm = _tile(m, tm, SUBLANES)
    n_tiles = len(col_offsets)
    assert all(o % SUBLANES == 0 and o + tn <= w_t.shape[1] for o in col_offsets)
    grid_spec = pltpu.PrefetchScalarGridSpec(
        num_scalar_prefetch=1,
        grid=(n_tiles, m // tm),
        in_specs=[pl.BlockSpec((tm, k), lambda j, i, off: (i, 0)),
                  pl.BlockSpec((pl.Element(1), pl.Element(tn), pl.Element(k)),
                               lambda j, i, off: (layer, pl.multiple_of(off[j], SUBLANES), 0))],
        out_specs=pl.BlockSpec((tm, tn), lambda j, i, off: (i, j)),
        scratch_shapes=[pltpu.VMEM((tn, k), BF16)],
    )
    return pl.pallas_call(
        functools.partial(_matmul_ws_kernel, transposed=True, act=None),
        grid_spec=grid_spec,
        out_shape=jax.ShapeDtypeStruct((m, n_tiles * tn), F32),
        compiler_params=_params(("arbitrary", "arbitrary")),
        name="matmul_ws_t",
    )(jnp.asarray(col_offsets, jnp.int32), hn, w_t)


def _hgrn_head(hq, hf, v, hg, lb3, ng, st_ref, hd, *, C, nb):
    R = nb * C
    lb = lb3[0:1, :]
    log_lb = lb3[1:2, :]
    log_1m_lb = lb3[2:3, :]
    q = _silu(hq)
    log_sig = jnp.minimum(hf, 0.0) - jnp.log(1.0 + jnp.exp(-jnp.abs(hf)))
    c2 = log_1m_lb + log_sig
    logf = jnp.maximum(log_lb, c2) + jnp.log(1.0 + jnp.exp(-jnp.abs(log_lb - c2)))
    k = (1.0 - lb) * (1.0 / (1.0 + jnp.exp(hf)))

    row = lax.broadcasted_iota(jnp.int32, (R, R), 0)
    col = lax.broadcasted_iota(jnp.int32, (R, R), 1)
    tri = jnp.where((col <= row) & ((row // C) == (col // C)), 1.0, 0.0).astype(F32)
    b = jnp.dot(tri, logf, precision=HIGHEST, preferred_element_type=F32)

    G = R // SUBLANES
    q3 = q.reshape(G, SUBLANES, HG_KEY)
    k3 = k.reshape(G, SUBLANES, HG_KEY)
    b3 = b.reshape(G, SUBLANES, HG_KEY)
    v3 = v.reshape(G, SUBLANES, HG_VAL)
    tpos = lax.broadcasted_iota(jnp.int32, (G, SUBLANES, 1), 1)
    o3 = jnp.zeros((G, SUBLANES, HG_VAL), F32)
    for s in range(SUBLANES):
        dec = jnp.exp(b3 - b3[:, s:s + 1, :])
        a_s = jnp.sum(q3 * k3[:, s:s + 1, :] * dec, axis=-1, keepdims=True)
        o3 = o3 + jnp.where(tpos >= s, a_s, 0.0) * v3[:, s:s + 1, :]
    o = o3.reshape(R, HG_VAL)

    half = C // 2
    a_off = None
    rrow = lax.broadcasted_iota(jnp.int32, (R, 1), 0)
    while half >= SUBLANES:
        blk = 2 * half
        nblk = R // blk
        b4 = b.reshape(nblk, blk, HG_KEY)
        ref_b = jnp.broadcast_to(b4[:, half - 1:half, :], (nblk, blk, HG_KEY)).reshape(R, HG_KEY)
        second = (rrow % blk) >= half
        qe = jnp.where(second, q * jnp.exp(b - ref_b), 0.0)
        ke = jnp.where(second, 0.0, k * jnp.exp(ref_b - b))
        a_l = _dot_nt(qe, ke)
        if blk != R:
            a_l = jnp.where((row // blk) == (col // blk), a_l, 0.0)
        a_off = a_l if a_off is None else a_off + a_l
        half //= 2
    if a_off is not None:
        o = o + _dot(a_off, v)

    qb = q * jnp.exp(b)
    o_parts = []
    for n in range(nb):
        lo, hi = n * C, (n + 1) * C
        st = st_ref[n, hd]
        o_parts.append(_dot_nt(qb[lo:hi], st))
        b_last = b[hi - 1:hi, :]
        kd = k[lo:hi] * jnp.exp(b_last - b[lo:hi])
        st_ref[n, hd] = st * jnp.exp(b_last) + lax.dot_general(
            v[lo:hi].astype(BF16), kd.astype(BF16), TN_DIMS, preferred_element_type=F32)
    o = o + (o_parts[0] if nb == 1 else jnp.concatenate(o_parts, axis=0))
    return _rms(o, ng) * _silu(hg)


def _hgrn_kernel(*refs, C, nb, hh, n_chunks, has_s0):
    hq_ref, hf_ref, hi_ref, hg_ref, lb_ref, ng_ref = refs[:6]
    s0_ref = refs[6] if has_s0 else None
    o_ref, snew_ref, st_ref = refs[-3:]
    ci = pl.program_id(2)

    @pl.when(ci == 0)
    def _():
        for n in range(nb):
            for hd in range(hh):
                st_ref[n, hd] = s0_ref[n, hd].T if has_s0 else jnp.zeros((HG_VAL, HG_KEY), F32)

    for hd in range(hh):
        sl = slice(hd * HG_KEY, (hd + 1) * HG_KEY)
        o = _hgrn_head(hq_ref[:, sl], hf_ref[:, sl], hi_ref[:, sl], hg_ref[:, sl], lb_ref[:, sl], ng_ref[:, sl],
                       st_ref, hd, C=C, nb=nb)
        o_ref[:, sl] = o.astype(o_ref.dtype)

    @pl.when(ci == n_chunks - 1)
    def _():
        for n in range(nb):
            for hd in range(hh):
                snew_ref[n, hd] = st_ref[n, hd].T


def hgrn2(proj, row0, n_seq, seq_len, lb3, norm_g, s0_all, layer, o_prev, snew_prev, *, C, nb, hh):
    R = nb * C
    n_chunks = seq_len // C
    assert seq_len % C == 0 and n_seq % nb == 0 and row0 % R == 0 and HG_HEADS % hh == 0
    assert nb == 1 or n_chunks == 1
    rb0 = row0 // R
    blocks_per_group = nb * seq_len // R
    hb = HG_HEADS // hh
    w = hh * HG_KEY

    def rmap(sec):
        return lambda s, h, c: (rb0 + s * blocks_per_group + c, sec * hb + h)

    in_specs = [pl.BlockSpec((R, w), rmap(sec)) for sec in range(4)]
    in_specs += [pl.BlockSpec((3, w), lambda s, h, c: (0, h)),
                 pl.BlockSpec((1, w), lambda s, h, c: (0, h))]
    args = [proj, proj, proj, proj, lb3, norm_g.reshape(1, -1)]
    state_spec = pl.BlockSpec((None, nb, hh, HG_KEY, HG_VAL), lambda s, h, c: (layer, s, h, 0, 0))
    if s0_all is not None:
        in_specs.append(state_spec)
        args.append(s0_all)
    aliases = {}
    for out_idx, prev in enumerate((o_prev, snew_prev)):
        if prev is not None:
            aliases[len(args)] = out_idx
            in_specs.append(_ANY_SPEC)
            args.append(prev)
    o_shape = o_prev.shape if o_prev is not None else (proj.shape[0], HG_HEADS * HG_VAL)
    return pl.pallas_call(
        functools.partial(_hgrn_kernel, C=C, nb=nb, hh=hh, n_chunks=n_chunks, has_s0=s0_all is not None),
        grid=(n_seq // nb, hb, n_chunks),
        in_specs=in_specs,
        out_specs=[pl.BlockSpec((R, w), lambda s, h, c: (rb0 + s * blocks_per_group + c, h)), state_spec],
        out_shape=[jax.ShapeDtypeStruct(o_shape, BF16),
                   jax.ShapeDtypeStruct(snew_prev.shape, F32)],
        scratch_shapes=[pltpu.VMEM((nb, hh, HG_VAL, HG_KEY), F32)],
        input_output_aliases=aliases,
        compiler_params=_params(("parallel", "parallel", "arbitrary")),
        name="hgrn2",
    )(*args)


def _rope_pair(slot, tab):
    prod = slot * tab
    return prod + pltpu.roll(prod, MLA_ROPE, 1)


def _mla_q_kernel(x_ref, g_ref, w_ref, tab_ref, o_ref):
    qa = x_ref[...][:, :MLA_Q_RANK]
    acc = _dot(_rms(qa, g_ref[...]), w_ref[...]) * (MLA_SCALE * LOG2E)
    tab = tab_ref[...]
    lane = lax.broadcasted_iota(jnp.int32, tab.shape, 1)
    nope_w = MLA_HEADS * MLA_NOPE
    for h in range(MLA_HEADS):
        slot = acc[:, nope_w + h * LANES: nope_w + (h + 1) * LANES]
        qr = jnp.where(lane < MLA_ROPE, _rope_pair(slot, tab), 0.0)
        o_ref[:, h * HEAD_SLOT: h * HEAD_SLOT + MLA_NOPE] = acc[:, h * MLA_NOPE:(h + 1) * MLA_NOPE].astype(BF16)
        o_ref[:, h * HEAD_SLOT + MLA_NOPE:(h + 1) * HEAD_SLOT] = qr.astype(BF16)


def mla_q(proj, q_norm, w_q_ext, tab, *, tm=512):
    m = proj.shape[0]
    tm = _tile(m, tm, SUBLANES)
    n = w_q_ext.shape[1]
    return pl.pallas_call(
        _mla_q_kernel,
        grid=(m // tm,),
        in_specs=[pl.BlockSpec((tm, 1024), lambda i: (i, COL_QA // 1024)),
                  pl.BlockSpec((1, MLA_Q_RANK), lambda i: (0, 0)),
                  pl.BlockSpec((MLA_Q_RANK, n), lambda i: (0, 0)),
                  pl.BlockSpec((tm, LANES), lambda i: (i, 0))],
        out_specs=pl.BlockSpec((tm, MLA_HEADS * HEAD_SLOT), lambda i: (i, 0)),
        out_shape=jax.ShapeDtypeStruct((m, MLA_HEADS * HEAD_SLOT), BF16),
        compiler_params=_params(("parallel",)),
        name="mla_q",
    )(proj, q_norm.reshape(1, -1), w_q_ext, tab)


def _mla_kv_kernel(*refs, with_kv):
    kva_ref, kr_ref, g_ref, w_ref, tab_ref = refs[:5]
    outs = refs[-4:] if with_kv else refs[-2:]
    c_ref, krot_ref = outs[:2]
    c = _rms(kva_ref[...], g_ref[...])
    c_ref[...] = c
    raw = kr_ref[...]
    lane = lax.broadcasted_iota(jnp.int32, raw.shape, 1)
    slot = jnp.where(lane < MLA_ROPE, raw,
                     jnp.where(lane < MLA_ROPE + MLA_ROPE // 2, pltpu.roll(raw, MLA_ROPE // 2, 1),
                               pltpu.roll(raw, LANES - MLA_ROPE // 2, 1)))
    krkr = _rope_pair(slot, tab_ref[...])
    krot_ref[...] = krkr[:, :MLA_ROPE]
    if with_kv:
        kfull_ref, v_ref = outs[2:]
        acc = _dot(c, w_ref[...])
        krkr_b = krkr.astype(BF16)
        for h in range(MLA_HEADS):
            kfull_ref[:, h * HEAD_SLOT: h * HEAD_SLOT + MLA_NOPE] = (
                acc[:, h * HEAD_SLOT: h * HEAD_SLOT + MLA_NOPE].astype(BF16))
            kfull_ref[:, h * HEAD_SLOT + MLA_NOPE:(h + 1) * HEAD_SLOT] = krkr_b
            v_ref[h * MLA_V:(h + 1) * MLA_V, :] = acc[:, h * HEAD_SLOT + MLA_NOPE:(h + 1) * HEAD_SLOT].T.astype(BF16)


def mla_kv(proj, row0, n_rows, kv_norm, w_kv_b, tab, layer, lat_prev, kr_prev, *, with_kv, tm=512):
    tm = _tile(math.gcd(n_rows, row0) if row0 else n_rows, tm, SUBLANES)
    rb0 = row0 // tm
    out_specs = [pl.BlockSpec((None, tm, MLA_KV_RANK), lambda i: (layer, i, 0)),
                 pl.BlockSpec((None, tm, MLA_ROPE), lambda i: (layer, i, 0))]
    out_shape = [jax.ShapeDtypeStruct(lat_prev.shape, F32), jax.ShapeDtypeStruct(kr_prev.shape, F32)]
    if with_kv:
        out_specs += [pl.BlockSpec((tm, MLA_HEADS * HEAD_SLOT), lambda i: (i, 0)),
                      pl.BlockSpec((MLA_HEADS * MLA_V, tm), lambda i: (0, i))]
        out_shape += [jax.ShapeDtypeStruct((n_rows, MLA_HEADS * HEAD_SLOT), BF16),
                      jax.ShapeDtypeStruct((MLA_HEADS * MLA_V, n_rows), BF16)]
    return pl.pallas_call(
        functools.partial(_mla_kv_kernel, with_kv=with_kv),
        grid=(n_rows // tm,),
        in_specs=[pl.BlockSpec((tm, MLA_KV_RANK), lambda i: (rb0 + i, COL_KVA // MLA_KV_RANK)),
                  pl.BlockSpec((tm, LANES), lambda i: (rb0 + i, COL_KR // LANES)),
                  pl.BlockSpec((1, MLA_KV_RANK), lambda i: (0, 0)),
                  pl.BlockSpec((MLA_KV_RANK, MLA_HEADS * HEAD_SLOT), lambda i: (0, 0)),
                  pl.BlockSpec((tm, LANES), lambda i: (rb0 + i, 0)),
                  _ANY_SPEC, _ANY_SPEC],
        out_specs=out_specs,
        out_shape=out_shape,
        input_output_aliases={5: 0, 6: 1},
        compiler_params=_params(("parallel",)),
        name="mla_kv",
    )(proj, proj, kv_norm.reshape(1, -1), w_kv_b, tab, lat_prev, kr_prev)


def _flash_kernel(qi_ref, ki_ref, q_ref, k_ref, vt_ref, prev_ref, o_ref, m_ref, l_ref, acc_ref, s_ref, *, tq, tk,
                  hh):
    t = pl.program_id(2)
    qi = qi_ref[t]
    ki = ki_ref[t]
    last_k = ((qi + 1) * tq - 1) // tk

    @pl.when(ki == 0)
    def _():
        m_ref[...] = jnp.full(m_ref.shape, -jnp.inf, F32)
        l_ref[...] = jnp.zeros(l_ref.shape, F32)
        acc_ref[...] = jnp.zeros(acc_ref.shape, F32)

    def update(masked):
        for hd in range(hh):
            s_ref[hd] = lax.dot_general(k_ref[:, hd * HEAD_SLOT:(hd + 1) * HEAD_SLOT],
                                        q_ref[:, hd * HEAD_SLOT:(hd + 1) * HEAD_SLOT], NT_DIMS,
                                        preferred_element_type=F32)
        for hd in range(hh):
            st = s_ref[hd]
            if masked:
                kpos = ki * tk + lax.broadcasted_iota(jnp.int32, (tk, tq), 0)
                qpos = qi * tq + lax.broadcasted_iota(jnp.int32, (tk, tq), 1)
                st = jnp.where(kpos <= qpos, st, -jnp.inf)
            m_prev = m_ref[hd]
            m_new = jnp.maximum(m_prev, jnp.max(st, axis=0, keepdims=True))
            alpha = jnp.exp2(m_prev - m_new)
            p = jnp.exp2(st - m_new)
            l_ref[hd] = alpha * l_ref[hd] + jnp.sum(p, axis=0, keepdims=True)
            acc_ref[hd] = alpha * acc_ref[hd] + jnp.dot(
                vt_ref[hd * MLA_V:(hd + 1) * MLA_V, :], p.astype(BF16), preferred_element_type=F32)
            m_ref[hd] = m_new

    @pl.when((ki + 1) * tk - 1 > qi * tq)
    def _():
        update(True)

    @pl.when((ki + 1) * tk - 1 <= qi * tq)
    def _():
        update(False)

    @pl.when(ki == last_k)
    def _():
        for hd in range(hh):
            o_ref[:, hd * MLA_V:(hd + 1) * MLA_V] = (acc_ref[hd] / l_ref[hd]).T.astype(o_ref.dtype)


def mla_prompt_attention(q_full, k_full, v_t, n_batch, seq, o_prev, *, tq=512, tk=512, hh=4):
    tq, tk = _tile(seq, tq, LANES), _tile(seq, tk, LANES)
    nq, nk = seq // tq, seq // tk
    pairs = [(a, b) for a in range(nq) for b in range(((a + 1) * tq - 1) // tk + 1)]
    qi_tab = jnp.asarray([p[0] for p in pairs], jnp.int32)
    ki_tab = jnp.asarray([p[1] for p in pairs], jnp.int32)
    grid_spec = pltpu.PrefetchScalarGridSpec(
        num_scalar_prefetch=2,
        grid=(n_batch, MLA_HEADS // hh, len(pairs)),
        in_specs=[pl.BlockSpec((tq, hh * HEAD_SLOT), lambda b, h, t, qt, kt: (b * nq + qt[t], h)),
                  pl.BlockSpec((tk, hh * HEAD_SLOT), lambda b, h, t, qt, kt: (b * nk + kt[t], h)),
                  pl.BlockSpec((hh * MLA_V, tk), lambda b, h, t, qt, kt: (h, b * nk + kt[t])),
                  _ANY_SPEC],
        out_specs=pl.BlockSpec((tq, hh * MLA_V), lambda b, h, t, qt, kt: (b * nq + qt[t], h)),
        scratch_shapes=[pltpu.VMEM((hh, 1, tq), F32), pltpu.VMEM((hh, 1, tq), F32),
                        pltpu.VMEM((hh, MLA_V, tq), F32), pltpu.VMEM((hh, tk, tq), F32)],
    )
    return pl.pallas_call(
        functools.partial(_flash_kernel, tq=tq, tk=tk, hh=hh),
        grid_spec=grid_spec,
        out_shape=jax.ShapeDtypeStruct(o_prev.shape, BF16),
        input_output_aliases={5: 0},
        compiler_params=_params(("parallel", "parallel", "arbitrary")),
        name="mla_prompt_flash",
    )(qi_tab, ki_tab, q_full, k_full, v_t, o_prev)


def _absorb_kernel(q_ref, w_ref, o_ref):
    qh = q_ref[...]
    q_lat = _dot_nt(qh[:, :MLA_NOPE], w_ref[...])
    out = jnp.concatenate([q_lat, qh[:, MLA_NOPE:].astype(F32)], axis=-1)
    o_ref[...] = out.reshape(o_ref.shape)


def mla_absorb(q_full, w_kv_b, row0, n_seq, t_len):
    rows = n_seq * t_len
    assert row0 % rows == 0
    width = MLA_KV_RANK + LANES
    return pl.pallas_call(
        _absorb_kernel,
        grid=(MLA_HEADS,),
        in_specs=[pl.BlockSpec((rows, HEAD_SLOT), lambda h: (row0 // rows, h)),
                  pl.BlockSpec((MLA_KV_RANK, MLA_NOPE), lambda h: (0, 2 * h))],
        out_specs=pl.BlockSpec((n_seq, None, t_len, width), lambda h: (0, h, 0, 0)),
        out_shape=jax.ShapeDtypeStruct((n_seq, MLA_HEADS, t_len, width), F32),
        compiler_params=_params(("parallel",)),
        name="mla_absorb",
    )(q_full, w_kv_b)


def _paged_attn_kernel(pt_ref, q_ref, cnew_ref, krnew_ref, clat_hbm, ckr_hbm, o_ref,
                       cf32, krf32, cbuf, m_ref, l_ref, acc_ref, csem, krsem, *, G, layer, t_len, n_steps, n_seq):
    b = pl.program_id(0)
    g = pl.program_id(1)
    step = b * n_steps + g
    slot = step % 2
    page = cbuf.shape[0] // G

    def page_copies(bb, gg, sl):
        copies = []
        for i in range(G):
            pidx = pt_ref[bb, gg * G + i]
            copies.append(pltpu.make_async_copy(
                clat_hbm.at[layer, pidx], cf32.at[sl, pl.ds(i * page, page), :], csem.at[sl]))
            copies.append(pltpu.make_async_copy(
                ckr_hbm.at[layer, pidx], krf32.at[sl, :, pl.ds(i * page, page)], krsem.at[sl]))
        return copies

    @pl.when(step == 0)
    def _():
        for cp in page_copies(b, g, slot):
            cp.start()

    @pl.when(step + 1 < n_seq * n_steps)
    def _():
        wrap = g + 1 == n_steps
        for cp in page_copies(jnp.where(wrap, b + 1, b), jnp.where(wrap, 0, g + 1), 1 - slot):
            cp.start()

    q = q_ref[0]
    q_lat = q[:, :MLA_KV_RANK]
    q_rope = q[:, MLA_KV_RANK:MLA_KV_RANK + MLA_ROPE]

    @pl.when(g == 0)
    def _():
        c_new = cnew_ref[...]
        kr_new = krnew_ref[...]
        tq = lax.broadcasted_iota(jnp.int32, (q.shape[0], 1), 0) % t_len
        cols = []
        for j in range(t_len):
            sj = (jnp.sum(q_lat * c_new[j:j + 1, :], axis=-1, keepdims=True)
                  + jnp.sum(q_rope * kr_new[j:j + 1, :], axis=-1, keepdims=True))
            cols.append(jnp.where(tq >= j, sj, -jnp.inf))
        m0 = cols[0]
        for j in range(1, t_len):
            m0 = jnp.maximum(m0, cols[j])
        l0 = jnp.zeros_like(m0)
        acc0 = jnp.zeros(acc_ref.shape, F32)
        for j in range(t_len):
            pj = jnp.exp2(cols[j] - m0)
            l0 = l0 + pj
            acc0 = acc0 + pj * c_new[j:j + 1, :]
        m_ref[...] = m0
        l_ref[...] = l0
        acc_ref[...] = acc0

    pltpu.make_async_copy(cf32.at[slot], cf32.at[slot], csem.at[slot]).wait()
    pltpu.make_async_copy(krf32.at[slot], krf32.at[slot], krsem.at[slot]).wait()

    q_lat_b = q_lat.astype(BF16)
    q_rope_b = q_rope.astype(BF16)
    s_parts = []
    for i in range(0, G, 2):
        cb = cf32[slot, pl.ds(i * page, 2 * page), :].astype(BF16)
        cbuf[i * page:(i + 2) * page, :] = cb
        krt = krf32[slot, :, pl.ds(i * page, 2 * page)].astype(BF16)
        s_parts.append(lax.dot_general(q_lat_b, cb, NT_DIMS, preferred_element_type=F32)
                       + jnp.dot(q_rope_b, krt, preferred_element_type=F32))
    s = jnp.concatenate(s_parts, axis=1)
    m_prev = m_ref[...]
    m_new = jnp.maximum(m_prev, jnp.max(s, axis=-1, keepdims=True))
    alpha = jnp.exp2(m_prev - m_new)
    p = jnp.exp2(s - m_new)
    l_ref[...] = alpha * l_ref[...] + jnp.sum(p, axis=-1, keepdims=True)
    acc_ref[...] = alpha * acc_ref[...] + jnp.dot(p.astype(BF16), cbuf[...], preferred_element_type=F32)
    m_ref[...] = m_new

    @pl.when(g == n_steps - 1)
    def _():
        o_ref[0] = acc_ref[...] / l_ref[...]


def mla_paged_attention(q_abs, lat_s, kr_s, cache_latent, cache_krope_t, layer, page_table, *, G=32):
    n_seq, heads, t_len, width = q_abs.shape
    n_pages = page_table.shape[1]
    page = cache_latent.shape[2]
    G = _tile(n_pages, G, 2)
    n_steps = n_pages // G
    rows = heads * t_len
    q3 = q_abs.reshape(n_seq, rows, width)
    grid_spec = pltpu.PrefetchScalarGridSpec(
        num_scalar_prefetch=1,
        grid=(n_seq, n_steps),
        in_specs=[pl.BlockSpec((1, rows, width), lambda b, g, pt: (b, 0, 0)),
                  pl.BlockSpec((None, t_len, MLA_KV_RANK), lambda b, g, pt: (layer, b, 0)),
                  pl.BlockSpec((None, t_len, MLA_ROPE), lambda b, g, pt: (layer, b, 0)),
                  _ANY_SPEC, _ANY_SPEC],
        out_specs=pl.BlockSpec((1, rows, MLA_KV_RANK), lambda b, g, pt: (b, 0, 0)),
        scratch_shapes=[pltpu.VMEM((2, G * page, MLA_KV_RANK), F32),
                        pltpu.VMEM((2, MLA_ROPE, G * page), F32),
                        pltpu.VMEM((G * page, MLA_KV_RANK), BF16),
                        pltpu.VMEM((rows, 1), F32), pltpu.VMEM((rows, 1), F32),
                        pltpu.VMEM((rows, MLA_KV_RANK), F32),
                        pltpu.SemaphoreType.DMA((2,)), pltpu.SemaphoreType.DMA((2,))],
    )
    return pl.pallas_call(
        functools.partial(_paged_attn_kernel, G=G, layer=layer, t_len=t_len, n_steps=n_steps, n_seq=n_seq),
        grid_spec=grid_spec,
        out_shape=jax.ShapeDtypeStruct((n_seq, rows, MLA_KV_RANK), F32),
        compiler_params=_params(("arbitrary", "arbitrary")),
        name="mla_paged_attention",
    )(page_table, q3, lat_s, kr_s, cache_latent, cache_krope_t)


def _uv_kernel(o_ref, w_ref, prev_ref, out_ref):
    o = o_ref[...]
    out_ref[...] = _dot(o.reshape(o.shape[0] * o.shape[1], o.shape[2]), w_ref[...]).astype(out_ref.dtype)


def mla_uv(o_lat, w_kv_b, row0, n_seq, t_len, ob_prev):
    o4 = o_lat.reshape(n_seq, MLA_HEADS, t_len, MLA_KV_RANK)
    rows = n_seq * t_len
    assert row0 % rows == 0
    return pl.pallas_call(
        _uv_kernel,
        grid=(MLA_HEADS,),
        in_specs=[pl.BlockSpec((n_seq, None, t_len, MLA_KV_RANK), lambda h: (0, h, 0, 0)),
                  pl.BlockSpec((MLA_KV_RANK, MLA_V), lambda h: (0, 2 * h + 1)),
                  _ANY_SPEC],
        out_specs=pl.BlockSpec((rows, MLA_V), lambda h: (row0 // rows, h)),
        out_shape=jax.ShapeDtypeStruct(ob_prev.shape, BF16),
        input_output_aliases={2: 0},
        compiler_params=_params(("parallel",)),
        name="mla_uv",
    )(o4, w_kv_b, ob_prev)


def _gmlp_kernel(*refs, emit_vn):
    gu_ref, gv_ref, g_ref, w_ref, bias_ref = refs[:5]
    o_ref = refs[-2] if emit_vn else refs[-1]
    vn = _rms(_gelu(gv_ref[...]), g_ref[...])
    if emit_vn:
        refs[-1][...] = vn
    u = _gelu(gu_ref[...])
    for grp in range(CM_GROUPS):
        sl = slice(grp * LANES, (grp + 1) * LANES)
        z = _dot(w_ref[grp], vn[:, sl]) + bias_ref[:, sl]
        o_ref[:, sl] = (u[:, sl] * z).astype(o_ref.dtype)


def gmlp(proj, row0, n_rows, v_norm, w_mix, bias_full, oc_prev, *, emit_vn):
    width = CM_GROUPS * LANES
    assert row0 % CM_CHUNK == 0 and n_rows % CM_CHUNK == 0
    rb0 = row0 // CM_CHUNK
    in_specs = [pl.BlockSpec((CM_CHUNK, width), lambda i: (rb0 + i, COL_GU // width)),
                pl.BlockSpec((CM_CHUNK, width), lambda i: (rb0 + i, COL_GV // width)),
                pl.BlockSpec((1, width), lambda i: (0, 0)),
                pl.BlockSpec((CM_GROUPS, CM_CHUNK, CM_CHUNK), lambda i: (0, 0, 0)),
                pl.BlockSpec((CM_CHUNK, width), lambda i: (0, 0))]
    args = [proj, proj, v_norm.reshape(1, -1), w_mix, bias_full]
    aliases = {}
    if oc_prev is not None:
        aliases[len(args)] = 0
        in_specs.append(_ANY_SPEC)
        args.append(oc_prev)
    out_specs = [pl.BlockSpec((CM_CHUNK, width), lambda i: (rb0 + i, 0))]
    out_shape = [jax.ShapeDtypeStruct((proj.shape[0], width), BF16)]
    if emit_vn:
        out_specs.append(pl.BlockSpec((CM_CHUNK, width), lambda i: (i, 0)))
        out_shape.append(jax.ShapeDtypeStruct((n_rows, width), F32))
    return pl.pallas_call(
        functools.partial(_gmlp_kernel, emit_vn=emit_vn),
        grid=(n_rows // CM_CHUNK,),
        in_specs=in_specs,
        out_specs=out_specs,
        out_shape=out_shape,
        input_output_aliases=aliases,
        compiler_params=_params(("parallel",)),
        name="gmlp",
    )(*args)


def _merge_kernel(oa_ref, ob_ref, oc_ref, g0_ref, g1_ref, g2_ref, w_ref, o_ref):
    acc = None
    for br, (b_ref, g_ref) in enumerate(((oa_ref, g0_ref), (ob_ref, g1_ref), (oc_ref, g2_ref))):
        part = g_ref[...] * jnp.dot(b_ref[...], w_ref[br], preferred_element_type=F32)
        acc = part if acc is None else acc + part
    o_ref[...] = acc.astype(o_ref.dtype)


def branch_merge(o_a, o_b, o_c, gates, w_branch, *, tm=1024, tn=512):
    m, bw = o_a.shape
    d = w_branch.shape[2]
    tm, tn = _tile(m, tm, SUBLANES), _tile(d, tn, LANES)
    nt = d // tn
    bspec = pl.BlockSpec((tm, bw), lambda i, j: (i, 0))

    def gspec(br):
        return pl.BlockSpec((tm, tn), lambda i, j: (i, br * nt + j))

    return pl.pallas_call(
        _merge_kernel,
        grid=(m // tm, nt),
        in_specs=[bspec, bspec, bspec, gspec(0), gspec(1), gspec(2),
                  pl.BlockSpec((N_BRANCH, bw, tn), lambda i, j: (0, 0, j))],
        out_specs=pl.BlockSpec((tm, tn), lambda i, j: (i, j)),
        out_shape=jax.ShapeDtypeStruct((m, d), BF16),
        compiler_params=_params(("parallel", "arbitrary")),
        name="branch_merge",
    )(o_a, o_b, o_c, gates, gates, gates, w_branch)


def _out_proj_kernel(mix_ref, w_ref, g_ref, gn_ref, *refs, n_first):
    o_ref, hn_ref = refs[-2:]
    if n_first is None:
        x = refs[0][...]
    else:
        x = jnp.where(pl.program_id(0) < n_first, refs[0][...], refs[1][...])
    y = jnp.dot(mix_ref[...], w_ref[...], preferred_element_type=F32)
    x_new = x + _rms(y, g_ref[...])
    o_ref[...] = x_new
    hn_ref[...] = _rms(x_new, gn_ref[...]).astype(BF16)


def out_proj(mix, x, w_out, g, g_next, *, tm=512):
    m, d = mix.shape
    row = lambda t: pl.BlockSpec((t, d), lambda i: (i, 0))
    vec = pl.BlockSpec((1, d), lambda i: (0, 0))
    if isinstance(x, tuple):
        tm = _tile(math.gcd(x[0].shape[0], x[1].shape[0]), tm, SUBLANES)
        n_first, x_specs = _two_source_specs(x, tm)
        x_args = list(x)
    else:
        tm = _tile(m, tm, SUBLANES)
        n_first, x_specs, x_args = None, [row(tm)], [x]
    return pl.pallas_call(
        functools.partial(_out_proj_kernel, n_first=n_first),
        grid=(m // tm,),
        in_specs=[row(tm), pl.BlockSpec((d, d), lambda i: (0, 0)), vec, vec] + x_specs,
        out_specs=[row(tm), row(tm)],
        out_shape=[jax.ShapeDtypeStruct((m, d), F32), jax.ShapeDtypeStruct((m, d), BF16)],
        compiler_params=_params(("parallel",)),
        name="out_proj",
    )(mix, w_out, g.reshape(1, -1), g_next.reshape(1, -1), *x_args)


def _ffn_up_kernel(*refs, t_len, per_seq_prev, tiles_per_seq, kc, rb, n_valid):
    h_ref, wg_ref, wv_ref, cw_ref, cb_ref = refs[:5]
    if per_seq_prev:
        pg_ref, pv_ref = refs[5:7]
        act_ref, tailg_ref, tailv_ref, wb_ref = refs[-4:]
    else:
        act_ref, tailg_ref, tailv_ref, wb_ref, carry_ref, shift_ref = refs[-6:]
    i = pl.program_id(1)
    tm, tn = act_ref.shape

    @pl.when(i == 0)
    def _():
        wb_ref[0] = wg_ref[...].astype(BF16)
        wb_ref[1] = wv_ref[...].astype(BF16)
        if not per_seq_prev:
            carry_ref[...] = jnp.zeros(carry_ref.shape, F32)

    if not per_seq_prev:
        seq_start = (i % tiles_per_seq) == 0

    def conv(u, which, sl, p):
        if per_seq_prev:
            u = u.reshape(rb // t_len, t_len, u.shape[-1])
            tpos = lax.broadcasted_iota(jnp.int32, (1, t_len, 1), 1)
            back1 = p[:, 1:2, :]
            back2 = jnp.where(tpos == 0, p[:, 0:1, :], p[:, 1:2, :])
            axis = 1
            m1 = jnp.where(tpos == 0, back1, pltpu.roll(u, 1, axis))
            m2 = jnp.where(tpos <= 1, back2, pltpu.roll(u, 2, axis))
        else:
            shift_ref[which, 0:SUBLANES, :] = p
            shift_ref[which, SUBLANES:SUBLANES + rb, :] = u
            m1 = shift_ref[which, SUBLANES - 1:SUBLANES - 1 + rb, :]
            m2 = shift_ref[which, SUBLANES - 2:SUBLANES - 2 + rb, :]
        y = (cb_ref[which:which + 1, sl] + cw_ref[which, 0:1, sl] * m2 + cw_ref[which, 1:2, sl] * m1
             + cw_ref[which, 2:3, sl] * u)
        return y.reshape(rb, y.shape[-1])

    def compute():
        tails = []
        for c0 in range(0, tn, kc):
            sl = slice(c0, c0 + kc)
            wg = wb_ref[0, :, sl]
            wv = wb_ref[1, :, sl]
            if not per_seq_prev:
                pg = jnp.where(seq_start, 0.0, carry_ref[0][:, sl])
                pv = jnp.where(seq_start, 0.0, carry_ref[1][:, sl])
            for r0 in range(0, tm, rb):
                hr = h_ref[r0:r0 + rb, :]
                ug = jnp.dot(hr, wg, preferred_element_type=F32)
                uv = jnp.dot(hr, wv, preferred_element_type=F32)
                if per_seq_prev:
                    s0, s1 = r0 // t_len, (r0 + rb) // t_len
                    pg = pg_ref[s0:s1, :, sl]
                    pv = pv_ref[s0:s1, :, sl]
                act = _gelu(conv(ug, 0, sl, pg)) * conv(uv, 1, sl, pv)
                act_ref[r0:r0 + rb, sl] = act.astype(act_ref.dtype)
                if per_seq_prev:
                    tailg_ref[s0:s1, :, sl] = ug.reshape(rb // t_len, t_len, kc)[:, t_len - (CONV_W - 1):, :]
                    tailv_ref[s0:s1, :, sl] = uv.reshape(rb // t_len, t_len, kc)[:, t_len - (CONV_W - 1):, :]
                else:
                    pg, pv = ug[rb - SUBLANES:, :], uv[rb - SUBLANES:, :]
            if not per_seq_prev:
                tailg_ref[:, sl] = pg
                tailv_ref[:, sl] = pv
                tails.append((sl, pg, pv))
        for sl, tg, tv in tails:
            carry_ref[0, :, sl] = tg
            carry_ref[1, :, sl] = tv

    if n_valid is None:
        compute()
    else:
        pl.when(i < n_valid)(compute)

        @pl.when(i >= n_valid)
        def _():
            act_ref[...] = jnp.zeros(act_ref.shape, act_ref.dtype)


def ffn_up(hn, w_up_all, layer, conv_w2, conv_b2, prev_state, row0, n_rows, t_len, act_prev, *, tm=1024,
           tn=512, kc=256, rb=512):
    d = hn.shape[1]
    d_ff = w_up_all.shape[2] // 2
    per_seq_prev = t_len == SUBLANES
    tm = _tile(n_rows if per_seq_prev else t_len, tm, SUBLANES)
    tn = _tile(d_ff, tn, LANES)
    kc = _tile(tn, kc, LANES)
    rb = _tile(tm, rb, t_len if per_seq_prev else SUBLANES)
    n_j = d_ff // tn
    n_i = n_rows // tm
    assert row0 % tm == 0 and (tm % t_len == 0 or t_len % tm == 0)
    rb0 = row0 // tm
    in_specs = [pl.BlockSpec((tm, d), lambda j, i: (rb0 + i, 0)),
                pl.BlockSpec((None, d, tn), lambda j, i: (layer, 0, j)),
                pl.BlockSpec((None, d, tn), lambda j, i: (layer, 0, n_j + j)),
                pl.BlockSpec((2, CONV_W, tn), lambda j, i: (0, 0, j)),
                pl.BlockSpec((2, tn), lambda j, i: (0, j))]
    args = [hn, w_up_all, w_up_all, conv_w2, conv_b2]
    scratch = [pltpu.VMEM((2, d, tn), BF16)]
    if per_seq_prev:
        ns = tm // t_len
        in_specs += [pl.BlockSpec((None, ns, CONV_W - 1, tn), lambda j, i: (layer, i, 0, j)),
                     pl.BlockSpec((None, ns, CONV_W - 1, tn), lambda j, i: (layer, i, 0, n_j + j))]
        args += [prev_state, prev_state]
        tail_spec = pl.BlockSpec((ns, CONV_W - 1, tn), lambda j, i: (i, 0, j))
        tail_shape = jax.ShapeDtypeStruct((n_rows // t_len, CONV_W - 1, d_ff), F32)
    else:
        scratch.append(pltpu.VMEM((2, SUBLANES, tn), F32))
        scratch.append(pltpu.VMEM((2, SUBLANES + rb, kc), F32))
        tail_spec = pl.BlockSpec((None, SUBLANES, tn), lambda j, i: (jnp.minimum(i, n_i - 1), 0, j))
        tail_shape = jax.ShapeDtypeStruct((n_i, SUBLANES, d_ff), F32)
    aliases = {}
    n_valid = None
    n_grid_i = n_i
    if act_prev is None and (row0 != 0 or hn.shape[0] % tm != 0 or per_seq_prev):
        act_prev = jnp.zeros((hn.shape[0], d_ff), BF16)
    if act_prev is not None:
        aliases[len(args)] = 0
        in_specs.append(_ANY_SPEC)
        args.append(act_prev)
    else:
        n_valid = n_i
        n_grid_i = hn.shape[0] // tm
    return pl.pallas_call(
        functools.partial(_ffn_up_kernel, t_len=t_len, per_seq_prev=per_seq_prev,
                          tiles_per_seq=max(t_len // tm, 1), kc=kc, rb=rb, n_valid=n_valid),
        grid=(n_j, n_grid_i),
        in_specs=in_specs,
        out_specs=[pl.BlockSpec((tm, tn), lambda j, i: (rb0 + i, j)), tail_spec, tail_spec],
        out_shape=[jax.ShapeDtypeStruct((hn.shape[0], d_ff), BF16), tail_shape, tail_shape],
        scratch_shapes=scratch,
        input_output_aliases=aliases,
        compiler_params=_params(("arbitrary", "arbitrary")),
        name="ffn_up",
    )(*args)


def _ffn_down_kernel(a_ref, w_ref, x_ref, g_ref, o_ref, acc_ref, *, n_k):
    kk = pl.program_id(1)
    part = jnp.dot(a_ref[...], w_ref[...], preferred_element_type=F32)
    if n_k == 1:
        o_ref[...] = x_ref[...] + _rms(part, g_ref[...])
        return

    @pl.when(kk == 0)
    def _():
        acc_ref[...] = part

    @pl.when((kk > 0) & (kk < n_k - 1))
    def _():
        acc_ref[...] += part

    @pl.when(kk == n_k - 1)
    def _():
        o_ref[...] = x_ref[...] + _rms(acc_ref[...] + part, g_ref[...])


def ffn_down(act, w_down, x, g, *, tm=512, tk=2816):
    d_ff, d = w_down.shape
    m = x.shape[0]
    tm = _tile(m, tm, SUBLANES)
    tk = _tile(d_ff, tk, LANES)
    n_k = d_ff // tk
    return pl.pallas_call(
        functools.partial(_ffn_down_kernel, n_k=n_k),
        grid=(m // tm, n_k),
        in_specs=[pl.BlockSpec((tm, tk), lambda i, k: (i, k)),
                  pl.BlockSpec((tk, d), lambda i, k: (k, 0)),
                  pl.BlockSpec((tm, d), lambda i, k: (i, 0)),
                  pl.BlockSpec((1, d), lambda i, k: (0, 0))],
        out_specs=pl.BlockSpec((tm, d), lambda i, k: (i, 0)),
        out_shape=jax.ShapeDtypeStruct(x.shape, F32),
        scratch_shapes=[pltpu.VMEM((tm, d), F32)],
        compiler_params=_params(("parallel", "arbitrary")),
        name="ffn_down",
    )(act, w_down, x, g.reshape(1, -1))


def _ple_kernel(x_ref, pe_ref, g_ref, wg_ref, wp_ref, gn_ref, *o_refs, n_first):
    x = x_ref[...]
    gate = _sigmoid(_dot(_rms(x, g_ref[...]), wg_ref[...]))
    y = x + _dot(pe_ref[...], wp_ref[...]) * gate
    if n_first is None:
        o_refs[0][...] = y
        o_refs[1][...] = _rms(y, gn_ref[...]).astype(BF16)
        return
    i = pl.program_id(0)

    @pl.when(i < n_first)
    def _():
        o_refs[0][...] = y

    @pl.when(i >= n_first)
    def _():
        o_refs[1][...] = y


def ple(x, pe, g, w_gate, w_proj, g_next, *, split_rows=None, tm=512):
    m, d = x.shape
    pd = pe.shape[1]
    tm = _tile(m if split_rows is None else math.gcd(split_rows, m - split_rows), tm, SUBLANES)
    if split_rows is None:
        n_first = None
        out_specs = [pl.BlockSpec((tm, d), lambda i: (i, 0)), pl.BlockSpec((tm, d), lambda i: (i, 0))]
        out_shape = [jax.ShapeDtypeStruct((m, d), F32), jax.ShapeDtypeStruct((m, d), BF16)]
    else:
        n_first = split_rows // tm
        out_specs = [pl.BlockSpec((tm, d), lambda i: (jnp.minimum(i, n_first - 1), 0)),
                     pl.BlockSpec((tm, d), lambda i: (jnp.maximum(i - n_first, 0), 0))]
        out_shape = [jax.ShapeDtypeStruct((split_rows, d), F32), jax.ShapeDtypeStruct((m - split_rows, d), F32)]
    return pl.pallas_call(
        functools.partial(_ple_kernel, n_first=n_first),
        grid=(m // tm,),
        in_specs=[pl.BlockSpec((tm, d), lambda i: (i, 0)),
                  pl.BlockSpec((tm, pd), lambda i: (i, 0)),
                  pl.BlockSpec((1, d), lambda i: (0, 0)),
                  pl.BlockSpec((d, d), lambda i: (0, 0)),
                  pl.BlockSpec((pd, d), lambda i: (0, 0)),
                  pl.BlockSpec((1, d), lambda i: (0, 0))],
        out_specs=out_specs,
        out_shape=out_shape,
        compiler_params=_params(("arbitrary",)),
        name="ple",
    )(x, pe, g.reshape(1, -1), w_gate, w_proj, g_next.reshape(1, -1))


def _rope_table(pos):
    inv = 1.0 / (ROPE_THETA ** (jnp.arange(0, MLA_ROPE, 2, dtype=F32) / MLA_ROPE))
    ang = pos.astype(F32)[:, None] * inv[None, :]
    cos, sin = jnp.cos(ang), jnp.sin(ang)
    return jnp.concatenate([cos, cos, -sin, sin], axis=-1)


def _swap_halves(w, width):
    half = width // 2
    return jnp.concatenate([w[..., half:], w[..., :half]], axis=-1)


def kernel(x_prompt, x_sample, cache_latent, cache_krope, state_hgrn, state_ffn_conv, page_table, p_prompt, p_sample, ln_mix_pre, ln_mix_post, ln_ffn_pre, ln_ffn_post, ln_ple, w_in, hgrn_lower_bounds, hgrn_norm, mla_q_norm, mla_kv_norm, w_q_b, w_kv_b, cm_v_norm, cm_w_s, cm_b, w_merge_gate, w_branch, w_out, w_ffn_up, ffn_conv_w, ffn_conv_b, w_ffn_down, w_ple_proj, w_ple_gate):
    n_b, seq, d = x_prompt.shape
    n_db, t_len, _ = x_sample.shape
    depth = w_in.shape[0]
    d_ff = w_ffn_down.shape[1]
    page = cache_latent.shape[2]
    past_len = page_table.shape[1] * page
    rows_p, rows_s = n_b * seq, n_db * t_len
    rows = rows_p + rows_s
    hk = HG_HEADS * HG_KEY

    lb_soft = jax.nn.softmax(hgrn_lower_bounds.astype(F32), axis=0)
    lbs = jnp.concatenate([jnp.zeros_like(lb_soft[:1]), jnp.cumsum(lb_soft[1:], axis=0)], axis=0)

    tab = jnp.concatenate([jnp.tile(_rope_table(jnp.arange(seq)), (n_b, 1)),
                           jnp.tile(_rope_table(past_len + jnp.arange(t_len)), (n_db, 1))], axis=0)
    cache_krope_t = jnp.swapaxes(cache_krope, 2, 3)
    w_in_t = jnp.swapaxes(w_in, 1, 2)

    x = (x_prompt.reshape(rows_p, d), x_sample.reshape(rows_s, d))
    tri = jnp.tril(jnp.ones((CM_CHUNK, CM_CHUNK), F32))
    seq_per_chunk = CM_CHUNK // t_len

    lat_p = jnp.zeros((depth, rows_p, MLA_KV_RANK), F32)
    kr_p = jnp.zeros((depth, rows_p, MLA_ROPE), F32)
    lat_s = jnp.zeros((depth, rows_s, MLA_KV_RANK), F32)
    kr_s = jnp.zeros((depth, rows_s, MLA_ROPE), F32)
    hg_p = jnp.zeros((depth, n_b, HG_HEADS, HG_KEY, HG_VAL), F32)
    hg_s = jnp.zeros((depth, n_db, HG_HEADS, HG_KEY, HG_VAL), F32)
    cv_p, cv_s, v_s = [], [], []

    for i in range(depth):
        wq = w_q_b[i].reshape(MLA_Q_RANK, MLA_HEADS, MLA_NOPE + MLA_ROPE)
        wq_rope = wq[..., MLA_NOPE:]
        w_q_ext = jnp.concatenate(
            [wq[..., :MLA_NOPE].reshape(MLA_Q_RANK, -1),
             jnp.concatenate([wq_rope, _swap_halves(wq_rope, MLA_ROPE)], axis=-1).reshape(MLA_Q_RANK, -1)],
            axis=1).astype(BF16)
        w_kvb = w_kv_b[i].astype(BF16)
        lb = lbs[i]
        lb3 = jnp.stack([lb, jnp.log(lb), jnp.log1p(-lb)], axis=0)
        w_mix_p = cm_w_s[i] * tri
        w_mix_s = jnp.einsum("ab,gts->gatbs", jnp.eye(seq_per_chunk, dtype=F32),
                             w_mix_p[:, :t_len, :t_len]).reshape(CM_GROUPS, CM_CHUNK, CM_CHUNK)
        bias_p = jnp.repeat(cm_b[i].T, LANES, axis=1)
        bias_s = jnp.tile(bias_p[:t_len], (seq_per_chunk, 1))
        conv_w2 = ffn_conv_w[i].reshape(CONV_W, 2, d_ff).transpose(1, 0, 2)
        conv_b2 = ffn_conv_b[i].reshape(2, d_ff)

        zeros_branch = jnp.zeros((rows, HG_HEADS * HG_VAL), BF16)
        if i == 0:
            hn = rms_cast(x, ln_mix_pre[i])
        proj = matmul_ws_t(hn, w_in_t, i, IN_TILE_SRC, tn=IN_TILE)
        gates = matmul_ws(hn, w_merge_gate, i, act="sigmoid")

        o_a, hg_p = hgrn2(proj, 0, n_b, seq, lb3, hgrn_norm[i], None, i, zeros_branch, hg_p,
                          C=_tile(seq, 128, SUBLANES), nb=1, hh=4)
        o_a, hg_s = hgrn2(proj, rows_p, n_db, t_len, lb3, hgrn_norm[i], state_hgrn, i, o_a, hg_s,
                          C=t_len, nb=_tile(n_db, LANES // t_len), hh=2)

        q_full = mla_q(proj, mla_q_norm[i], w_q_ext, tab)
        lat_p, kr_p, k_full, v_all = mla_kv(proj, 0, rows_p, mla_kv_norm[i], w_kvb, tab, i, lat_p, kr_p,
                                            with_kv=True)
        lat_s, kr_s = mla_kv(proj, rows_p, rows_s, mla_kv_norm[i], w_kvb, tab, i, lat_s, kr_s, with_kv=False)
        o_b = mla_prompt_attention(q_full, k_full, v_all, n_b, seq, zeros_branch)
        q_abs = mla_absorb(q_full, w_kvb, rows_p, n_db, t_len)
        o_lat = mla_paged_attention(q_abs, lat_s, kr_s, cache_latent, cache_krope_t, i, page_table)
        o_b = mla_uv(o_lat, w_kvb, rows_p, n_db, t_len, o_b)

        (o_c,) = gmlp(proj, 0, rows_p, cm_v_norm[i], w_mix_p, bias_p, zeros_branch, emit_vn=False)
        o_c, vn_s = gmlp(proj, rows_p, rows_s, cm_v_norm[i], w_mix_s, bias_s, o_c, emit_vn=True)

        mix = branch_merge(o_a, o_b, o_c, gates, w_branch[i].astype(BF16))
        x, hn_ffn = out_proj(mix, x, w_out[i].astype(BF16), ln_mix_post[i], ln_ffn_pre[i])

        act, tg_p, tv_p = ffn_up(hn_ffn, w_ffn_up, i, conv_w2, conv_b2, None, 0, rows_p, seq, None)
        act, tg_s, tv_s = ffn_up(hn_ffn, w_ffn_up, i, conv_w2, conv_b2, state_ffn_conv, rows_p, rows_s, t_len, act)
        x = ffn_down(act, w_ffn_down[i].astype(BF16), x, ln_ffn_post[i])

        pe = jnp.concatenate([p_prompt[i].reshape(rows_p, -1), p_sample[i].reshape(rows_s, -1)], axis=0)
        last = i == depth - 1
        x = ple(x, pe, ln_ple[i], w_ple_gate[i].astype(BF16), w_ple_proj[i].astype(BF16),
                ln_ple[i] if last else ln_mix_pre[i + 1], split_rows=rows_p if last else None)
        if not last:
            x, hn = x

        tails_p = jnp.concatenate([tg_p, tv_p], axis=-1).reshape(n_b, -1, SUBLANES, 2 * d_ff)
        cv_p.append(tails_p[:, -1, SUBLANES - (CONV_W - 1):, :])
        cv_s.append(jnp.concatenate([tg_s, tv_s], axis=-1))
        v_s.append(vn_s.reshape(n_db, t_len, -1))

    y_p, y_s = x
    return (y_p.reshape(n_b, seq, d), y_s.reshape(n_db, t_len, d),
            lat_p.reshape(depth, n_b, seq, -1), kr_p.reshape(depth, n_b, seq, -1),
            lat_s.reshape(depth, n_db, t_len, -1), kr_s.reshape(depth, n_db, t_len, -1),
            hg_p, hg_s, jnp.stack(cv_p), jnp.stack(cv_s), jnp.stack(v_s))
```

```python
import functools
import math

import jax
import jax.numpy as jnp
import numpy as np
from jax import lax
from jax.experimental import pallas as pl
from jax.experimental.pallas import tpu as pltpu

F32 = jnp.float32
BF16 = jnp.bfloat16
HIGHEST = lax.Precision.HIGHEST

HG_HEADS = 8
HG_KEY = 128
HG_VAL = 128
MLA_HEADS = 8
MLA_NOPE = 128
MLA_ROPE = 64
MLA_V = 128
MLA_Q_RANK = 768
MLA_KV_RANK = 512
MLA_SCALE = (MLA_NOPE + MLA_ROPE) ** -0.5
ROPE_THETA = 10000.0
CM_GROUPS = 8
CM_CHUNK = 128
N_BRANCH = 3
CONV_W = 3
EPS = 1e-6
LOG2E = math.log2(math.e)

LANES = 128
SUBLANES = 8
VMEM_LIMIT = 56 * 1024 * 1024
HEAD_SLOT = 2 * LANES

IN_TILE = 1024
IN_TILE_SRC = (0, 1024, 2048, 3072, 5440, 6464, 4096, 4864)
COL_GU, COL_GV = 4096, 5120
COL_QA = 6144
COL_KVA = 7168
COL_KR = 7680

NT_DIMS = (((1,), (1,)), ((), ()))
TN_DIMS = (((0,), (0,)), ((), ()))


def _params(sem):
    return pltpu.CompilerParams(dimension_semantics=sem, vmem_limit_bytes=VMEM_LIMIT)


def _rms(x, g):
    ms = jnp.mean(x * x, axis=-1, keepdims=True)
    return x * lax.rsqrt(ms + EPS) * g


def _gelu(x):
    return 0.5 * x * (1.0 + jnp.tanh(math.sqrt(2.0 / math.pi) * (x + 0.044715 * (x * x * x))))


def _sigmoid(x):
    return 0.5 * jnp.tanh(0.5 * x) + 0.5


def _silu(x):
    return x * _sigmoid(x)


def _tile(n, pref, unit=1):
    t = (min(pref, n) // unit) * unit
    while t > unit and n % t:
        t -= unit
    assert t > 0 and n % t == 0, (n, pref, unit)
    return t


def _dot(a, b):
    return jnp.dot(a.astype(BF16), b.astype(BF16), preferred_element_type=F32)


def _dot_nt(a, b):
    return lax.dot_general(a.astype(BF16), b.astype(BF16), NT_DIMS, preferred_element_type=F32)


_ANY_SPEC = pl.BlockSpec(memory_space=pl.ANY)


def _two_source_specs(xs, tm):
    n_first = xs[0].shape[0] // tm
    k = xs[0].shape[1]
    return n_first, [pl.BlockSpec((tm, k), lambda i, *_: (jnp.minimum(i, n_first - 1), 0)),
                     pl.BlockSpec((tm, k), lambda i, *_: (jnp.maximum(i - n_first, 0), 0))]


def _rms_cast_kernel(xa_ref, xb_ref, g_ref, o_ref, *, n_first):
    x = jnp.where(pl.program_id(0) < n_first, xa_ref[...], xb_ref[...])
    o_ref[...] = _rms(x, g_ref[...]).astype(BF16)


def rms_cast(xs, g, *, tm=1024):
    k = xs[0].shape[1]
    m = xs[0].shape[0] + xs[1].shape[0]
    tm = _tile(math.gcd(xs[0].shape[0], xs[1].shape[0]), tm, SUBLANES)
    n_first, x_specs = _two_source_specs(xs, tm)
    return pl.pallas_call(
        functools.partial(_rms_cast_kernel, n_first=n_first),
        grid=(m // tm,),
        in_specs=x_specs + [pl.BlockSpec((1, k), lambda i: (0, 0))],
        out_specs=pl.BlockSpec((tm, k), lambda i: (i, 0)),
        out_shape=jax.ShapeDtypeStruct((m, k), BF16),
        compiler_params=_params(("parallel",)),
        name="rms_cast",
    )(xs[0], xs[1], g.reshape(1, k))


def _matmul_ws_kernel(*refs, transposed, act):
    h_ref, w_ref, o_ref, wb_ref = refs[-4:]

    @pl.when(pl.program_id(1) == 0)
    def _():
        wb_ref[...] = (w_ref[0] if transposed else w_ref[...]).astype(BF16)

    if transposed:
        acc = lax.dot_general(h_ref[...], wb_ref[...], NT_DIMS, preferred_element_type=F32)
    else:
        acc = jnp.dot(h_ref[...], wb_ref[...], preferred_element_type=F32)
    if act == "sigmoid":
        acc = _sigmoid(acc)
    o_ref[...] = acc


def matmul_ws(hn, w_all, layer, *, act=None, tm=1024, tn=1024):
    m, k = hn.shape
    n = w_all.shape[2]
    tm, tn = _tile(m, tm, SUBLANES), _tile(n, tn, LANES)
    return pl.pallas_call(
        functools.partial(_matmul_ws_kernel, transposed=False, act=act),
        grid=(n // tn, m // tm),
        in_specs=[pl.BlockSpec((tm, k), lambda j, i: (i, 0)),
                  pl.BlockSpec((None, k, tn), lambda j, i: (layer, 0, j))],
        out_specs=pl.BlockSpec((tm, tn), lambda j, i: (i, j)),
        out_shape=jax.ShapeDtypeStruct((m, n), F32),
        scratch_shapes=[pltpu.VMEM((k, tn), BF16)],
        compiler_params=_params(("arbitrary", "arbitrary")),
        name="matmul_ws",
    )(hn, w_all)


def matmul_ws_t(hn, w_t, layer, col_offsets, *, tm=1024, tn=1024):
    m, k = hn.shape
    tm = _tile(m, tm, SUBLANES)
    n_tiles = len(col_offsets)
    assert all(o % SUBLANES == 0 and o + tn <= w_t.shape[1] for o in col_offsets)
    grid_spec = pltpu.PrefetchScalarGridSpec(
        num_scalar_prefetch=1,
        grid=(n_tiles, m // tm),
        in_specs=[pl.BlockSpec((tm, k), lambda j, i, off: (i, 0)),
                  pl.BlockSpec((pl.Element(1), pl.Element(tn), pl.Element(k)),
                               lambda j, i, off: (layer, pl.multiple_of(off[j], SUBLANES), 0))],
        out_specs=pl.BlockSpec((tm, tn), lambda j, i, off: (i, j)),
        scratch_shapes=[pltpu.VMEM((tn, k), BF16)],
    )
    return pl.pallas_call(
        functools.partial(_matmul_ws_kernel, transposed=True, act=None),
        grid_spec=grid_spec,
        out_shape=jax.ShapeDtypeStruct((m, n_tiles * tn), F32),
        compiler_params=_params(("arbitrary", "arbitrary")),
        name="matmul_ws_t",
    )(jnp.asarray(col_offsets, jnp.int32), hn, w_t)


def _hgrn_head(hq, hf, v, hg, lb3, ng, st_ref, hd, *, C, nb):
    R = nb * C
    lb = lb3[0:1, :]
    log_lb = lb3[1:2, :]
    log_1m_lb = lb3[2:3, :]
    q = _silu(hq)
    log_sig = jnp.minimum(hf, 0.0) - jnp.log(1.0 + jnp.exp(-jnp.abs(hf)))
    c2 = log_1m_lb + log_sig
    logf = jnp.maximum(log_lb, c2) + jnp.log(1.0 + jnp.exp(-jnp.abs(log_lb - c2)))
    k = (1.0 - lb) * (1.0 / (1.0 + jnp.exp(hf)))

    row = lax.broadcasted_iota(jnp.int32, (R, R), 0)
    col = lax.broadcasted_iota(jnp.int32, (R, R), 1)
    tri = jnp.where((col <= row) & ((row // C) == (col // C)), 1.0, 0.0).astype(F32)
    b = jnp.dot(tri, logf, precision=HIGHEST, preferred_element_type=F32)

    G = R // SUBLANES
    q3 = q.reshape(G, SUBLANES, HG_KEY)
    k3 = k.reshape(G, SUBLANES, HG_KEY)
    b3 = b.reshape(G, SUBLANES, HG_KEY)
    v3 = v.reshape(G, SUBLANES, HG_VAL)
    tpos = lax.broadcasted_iota(jnp.int32, (G, SUBLANES, 1), 1)
    o3 = jnp.zeros((G, SUBLANES, HG_VAL), F32)
    for s in range(SUBLANES):
        dec = jnp.exp(b3 - b3[:, s:s + 1, :])
        a_s = jnp.sum(q3 * k3[:, s:s + 1, :] * dec, axis=-1, keepdims=True)
        o3 = o3 + jnp.where(tpos >= s, a_s, 0.0) * v3[:, s:s + 1, :]
    o = o3.reshape(R, HG_VAL)

    half = C // 2
    a_off = None
    rrow = lax.broadcasted_iota(jnp.int32, (R, 1), 0)
    while half >= SUBLANES:
        blk = 2 * half
        nblk = R // blk
        b4 = b.reshape(nblk, blk, HG_KEY)
        ref_b = jnp.broadcast_to(b4[:, half - 1:half, :], (nblk, blk, HG_KEY)).reshape(R, HG_KEY)
        second = (rrow % blk) >= half
        qe = jnp.where(second, q * jnp.exp(b - ref_b), 0.0)
        ke = jnp.where(second, 0.0, k * jnp.exp(ref_b - b))
        a_l = _dot_nt(qe, ke)
        if blk != R:
            a_l = jnp.where((row // blk) == (col // blk), a_l, 0.0)
        a_off = a_l if a_off is None else a_off + a_l
        half //= 2
    if a_off is not None:
        o = o + _dot(a_off, v)

    qb = q * jnp.exp(b)
    o_parts = []
    for n in range(nb):
        lo, hi = n * C, (n + 1) * C
        st = st_ref[n, hd]
        o_parts.append(_dot_nt(qb[lo:hi], st))
        b_last = b[hi - 1:hi, :]
        kd = k[lo:hi] * jnp.exp(b_last - b[lo:hi])
        st_ref[n, hd] = st * jnp.exp(b_last) + lax.dot_general(
            v[lo:hi].astype(BF16), kd.astype(BF16), TN_DIMS, preferred_element_type=F32)
    o = o + (o_parts[0] if nb == 1 else jnp.concatenate(o_parts, axis=0))
    return _rms(o, ng) * _silu(hg)


def _hgrn_kernel(*refs, C, nb, hh, n_chunks, has_s0):
    hq_ref, hf_ref, hi_ref, hg_ref, lb_ref, ng_ref = refs[:6]
    s0_ref = refs[6] if has_s0 else None
    o_ref, snew_ref, st_ref = refs[-3:]
    ci = pl.program_id(2)

    @pl.when(ci == 0)
    def _():
        for n in range(nb):
            for hd in range(hh):
                st_ref[n, hd] = s0_ref[n, hd].T if has_s0 else jnp.zeros((HG_VAL, HG_KEY), F32)

    for hd in range(hh):
        sl = slice(hd * HG_KEY, (hd + 1) * HG_KEY)
        o = _hgrn_head(hq_ref[:, sl], hf_ref[:, sl], hi_ref[:, sl], hg_ref[:, sl], lb_ref[:, sl], ng_ref[:, sl],
                       st_ref, hd, C=C, nb=nb)
        o_ref[:, sl] = o.astype(o_ref.dtype)

    @pl.when(ci == n_chunks - 1)
    def _():
        for n in range(nb):
            for hd in range(hh):
                snew_ref[n, hd] = st_ref[n, hd].T


def hgrn2(proj, row0, n_seq, seq_len, lb3, norm_g, s0_all, layer, o_prev, snew_prev, *, C, nb, hh):
    R = nb * C
    n_chunks = seq_len // C
    assert seq_len % C == 0 and n_seq % nb == 0 and row0 % R == 0 and HG_HEADS % hh == 0
    assert nb == 1 or n_chunks == 1
    rb0 = row0 // R
    blocks_per_group = nb * seq_len // R
    hb = HG_HEADS // hh
    w = hh * HG_KEY

    def rmap(sec):
        return lambda s, h, c: (rb0 + s * blocks_per_group + c, sec * hb + h)

    in_specs = [pl.BlockSpec((R, w), rmap(sec)) for sec in range(4)]
    in_specs += [pl.BlockSpec((3, w), lambda s, h, c: (0, h)),
                 pl.BlockSpec((1, w), lambda s, h, c: (0, h))]
    args = [proj, proj, proj, proj, lb3, norm_g.reshape(1, -1)]
    state_spec = pl.BlockSpec((None, nb, hh, HG_KEY, HG_VAL), lambda s, h, c: (layer, s, h, 0, 0))
    if s0_all is not None:
        in_specs.append(state_spec)
        args.append(s0_all)
    aliases = {}
    for out_idx, prev in enumerate((o_prev, snew_prev)):
        if prev is not None:
            aliases[len(args)] = out_idx
            in_specs.append(_ANY_SPEC)
            args.append(prev)
    o_shape = o_prev.shape if o_prev is not None else (proj.shape[0], HG_HEADS * HG_VAL)
    return pl.pallas_call(
        functools.partial(_hgrn_kernel, C=C, nb=nb, hh=hh, n_chunks=n_chunks, has_s0=s0_all is not None),
        grid=(n_seq // nb, hb, n_chunks),
        in_specs=in_specs,
        out_specs=[pl.BlockSpec((R, w), lambda s, h, c: (rb0 + s * blocks_per_group + c, h)), state_spec],
        out_shape=[jax.ShapeDtypeStruct(o_shape, BF16),
                   jax.ShapeDtypeStruct(snew_prev.shape, F32)],
        scratch_shapes=[pltpu.VMEM((nb, hh, HG_VAL, HG_KEY), F32)],
        input_output_aliases=aliases,
        compiler_params=_params(("parallel", "parallel", "arbitrary")),
        name="hgrn2",
    )(*args)


def _rope_pair(slot, tab):
    prod = slot * tab
    return prod + pltpu.roll(prod, MLA_ROPE, 1)


def _mla_q_kernel(x_ref, g_ref, w_ref, tab_ref, o_ref):
    qa = x_ref[...][:, :MLA_Q_RANK]
    acc = _dot(_rms(qa, g_ref[...]), w_ref[...]) * (MLA_SCALE * LOG2E)
    tab = tab_ref[...]
    lane = lax.broadcasted_iota(jnp.int32, tab.shape, 1)
    nope_w = MLA_HEADS * MLA_NOPE
    for h in range(MLA_HEADS):
        slot = acc[:, nope_w + h * LANES: nope_w + (h + 1) * LANES]
        qr = jnp.where(lane < MLA_ROPE, _rope_pair(slot, tab), 0.0)
        o_ref[:, h * HEAD_SLOT: h * HEAD_SLOT + MLA_NOPE] = acc[:, h * MLA_NOPE:(h + 1) * MLA_NOPE].astype(BF16)
        o_ref[:, h * HEAD_SLOT + MLA_NOPE:(h + 1) * HEAD_SLOT] = qr.astype(BF16)


def mla_q(proj, q_norm, w_q_ext, tab, *, tm=512):
    m = proj.shape[0]
    tm = _tile(m, tm, SUBLANES)
    n = w_q_ext.shape[1]
    return pl.pallas_call(
        _mla_q_kernel,
        grid=(m // tm,),
        in_specs=[pl.BlockSpec((tm, 1024), lambda i: (i, COL_QA // 1024)),
                  pl.BlockSpec((1, MLA_Q_RANK), lambda i: (0, 0)),
                  pl.BlockSpec((MLA_Q_RANK, n), lambda i: (0, 0)),
                  pl.BlockSpec((tm, LANES), lambda i: (i, 0))],
        out_specs=pl.BlockSpec((tm, MLA_HEADS * HEAD_SLOT), lambda i: (i, 0)),
        out_shape=jax.ShapeDtypeStruct((m, MLA_HEADS * HEAD_SLOT), BF16),
        compiler_params=_params(("parallel",)),
        name="mla_q",
    )(proj, q_norm.reshape(1, -1), w_q_ext, tab)


def _mla_kv_kernel(*refs, with_kv):
    kva_ref, kr_ref, g_ref, w_ref, tab_ref = refs[:5]
    outs = refs[-4:] if with_kv else refs[-2:]
    c_ref, krot_ref = outs[:2]
    c = _rms(kva_ref[...], g_ref[...])
    c_ref[...] = c
    raw = kr_ref[...]
    lane = lax.broadcasted_iota(jnp.int32, raw.shape, 1)
    slot = jnp.where(lane < MLA_ROPE, raw,
                     jnp.where(lane < MLA_ROPE + MLA_ROPE // 2, pltpu.roll(raw, MLA_ROPE // 2, 1),
                               pltpu.roll(raw, LANES - MLA_ROPE // 2, 1)))
    krkr = _rope_pair(slot, tab_ref[...])
    krot_ref[...] = krkr[:, :MLA_ROPE]
    if with_kv:
        kfull_ref, v_ref = outs[2:]
        acc = _dot(c, w_ref[...])
        krkr_b = krkr.astype(BF16)
        for h in range(MLA_HEADS):
            kfull_ref[:, h * HEAD_SLOT: h * HEAD_SLOT + MLA_NOPE] = (
                acc[:, h * HEAD_SLOT: h * HEAD_SLOT + MLA_NOPE].astype(BF16))
            kfull_ref[:, h * HEAD_SLOT + MLA_NOPE:(h + 1) * HEAD_SLOT] = krkr_b
            v_ref[h * MLA_V:(h + 1) * MLA_V, :] = acc[:, h * HEAD_SLOT + MLA_NOPE:(h + 1) * HEAD_SLOT].T.astype(BF16)


def mla_kv(proj, row0, n_rows, kv_norm, w_kv_b, tab, layer, lat_prev, kr_prev, *, with_kv, tm=512):
    tm = _tile(math.gcd(n_rows, row0) if row0 else n_rows, tm, SUBLANES)
    rb0 = row0 // tm
    out_specs = [pl.BlockSpec((None, tm, MLA_KV_RANK), lambda i: (layer, i, 0)),
                 pl.BlockSpec((None, tm, MLA_ROPE), lambda i: (layer, i, 0))]
    out_shape = [jax.ShapeDtypeStruct(lat_prev.shape, F32), jax.ShapeDtypeStruct(kr_prev.shape, F32)]
    if with_kv:
        out_specs += [pl.BlockSpec((tm, MLA_HEADS * HEAD_SLOT), lambda i: (i, 0)),
                      pl.BlockSpec((MLA_HEADS * MLA_V, tm), lambda i: (0, i))]
        out_shape += [jax.ShapeDtypeStruct((n_rows, MLA_HEADS * HEAD_SLOT), BF16),
                      jax.ShapeDtypeStruct((MLA_HEADS * MLA_V, n_rows), BF16)]
    return pl.pallas_call(
        functools.partial(_mla_kv_kernel, with_kv=with_kv),
        grid=(n_rows // tm,),
        in_specs=[pl.BlockSpec((tm, MLA_KV_RANK), lambda i: (rb0 + i, COL_KVA // MLA_KV_RANK)),
                  pl.BlockSpec((tm, LANES), lambda i: (rb0 + i, COL_KR // LANES)),
                  pl.BlockSpec((1, MLA_KV_RANK), lambda i: (0, 0)),
                  pl.BlockSpec((MLA_KV_RANK, MLA_HEADS * HEAD_SLOT), lambda i: (0, 0)),
                  pl.BlockSpec((tm, LANES), lambda i: (rb0 + i, 0)),
                  _ANY_SPEC, _ANY_SPEC],
        out_specs=out_specs,
        out_shape=out_shape,
        input_output_aliases={5: 0, 6: 1},
        compiler_params=_params(("parallel",)),
        name="mla_kv",
    )(proj, proj, kv_norm.reshape(1, -1), w_kv_b, tab, lat_prev, kr_prev)


def _flash_kernel(qi_ref, ki_ref, q_ref, k_ref, vt_ref, prev_ref, o_ref, m_ref, l_ref, acc_ref, s_ref, *, tq, tk,
                  hh):
    t = pl.program_id(2)
    qi = qi_ref[t]
    ki = ki_ref[t]
    last_k = ((qi + 1) * tq - 1) // tk

    @pl.when(ki == 0)
    def _():
        m_ref[...] = jnp.full(m_ref.shape, -jnp.inf, F32)
        l_ref[...] = jnp.zeros(l_ref.shape, F32)
        acc_ref[...] = jnp.zeros(acc_ref.shape, F32)

    def update(masked):
        for hd in range(hh):
            s_ref[hd] = lax.dot_general(k_ref[:, hd * HEAD_SLOT:(hd + 1) * HEAD_SLOT],
                                        q_ref[:, hd * HEAD_SLOT:(hd + 1) * HEAD_SLOT], NT_DIMS,
                                        preferred_element_type=F32)
        for hd in range(hh):
            st = s_ref[hd]
            if masked:
                kpos = ki * tk + lax.broadcasted_iota(jnp.int32, (tk, tq), 0)
                qpos = qi * tq + lax.broadcasted_iota(jnp.int32, (tk, tq), 1)
                st = jnp.where(kpos <= qpos, st, -jnp.inf)
            m_prev = m_ref[hd]
            m_new = jnp.maximum(m_prev, jnp.max(st, axis=0, keepdims=True))
            alpha = jnp.exp2(m_prev - m_new)
            p = jnp.exp2(st - m_new)
            l_ref[hd] = alpha * l_ref[hd] + jnp.sum(p, axis=0, keepdims=True)
            acc_ref[hd] = alpha * acc_ref[hd] + jnp.dot(
                vt_ref[hd * MLA_V:(hd + 1) * MLA_V, :], p.astype(BF16), preferred_element_type=F32)
            m_ref[hd] = m_new

    @pl.when((ki + 1) * tk - 1 > qi * tq)
    def _():
        update(True)

    @pl.when((ki + 1) * tk - 1 <= qi * tq)
    def _():
        update(False)

    @pl.when(ki == last_k)
    def _():
        for hd in range(hh):
            o_ref[:, hd * MLA_V:(hd + 1) * MLA_V] = (acc_ref[hd] / l_ref[hd]).T.astype(o_ref.dtype)


def mla_prompt_attention(q_full, k_full, v_t, n_batch, seq, o_prev, *, tq=512, tk=512, hh=4):
    tq, tk = _tile(seq, tq, LANES), _tile(seq, tk, LANES)
    nq, nk = seq // tq, seq // tk
    pairs = [(a, b) for a in range(nq) for b in range(((a + 1) * tq - 1) // tk + 1)]
    qi_tab = jnp.asarray([p[0] for p in pairs], jnp.int32)
    ki_tab = jnp.asarray([p[1] for p in pairs], jnp.int32)
    grid_spec = pltpu.PrefetchScalarGridSpec(
        num_scalar_prefetch=2,
        grid=(n_batch, MLA_HEADS // hh, len(pairs)),
        in_specs=[pl.BlockSpec((tq, hh * HEAD_SLOT), lambda b, h, t, qt, kt: (b * nq + qt[t], h)),
                  pl.BlockSpec((tk, hh * HEAD_SLOT), lambda b, h, t, qt, kt: (b * nk + kt[t], h)),
                  pl.BlockSpec((hh * MLA_V, tk), lambda b, h, t, qt, kt: (h, b * nk + kt[t])),
                  _ANY_SPEC],
        out_specs=pl.BlockSpec((tq, hh * MLA_V), lambda b, h, t, qt, kt: (b * nq + qt[t], h)),
        scratch_shapes=[pltpu.VMEM((hh, 1, tq), F32), pltpu.VMEM((hh, 1, tq), F32),
                        pltpu.VMEM((hh, MLA_V, tq), F32), pltpu.VMEM((hh, tk, tq), F32)],
    )
    return pl.pallas_call(
        functools.partial(_flash_kernel, tq=tq, tk=tk, hh=hh),
        grid_spec=grid_spec,
        out_shape=jax.ShapeDtypeStruct(o_prev.shape, BF16),
        input_output_aliases={5: 0},
        compiler_params=_params(("parallel", "parallel", "arbitrary")),
        name="mla_prompt_flash",
    )(qi_tab, ki_tab, q_full, k_full, v_t, o_prev)


def _absorb_kernel(q_ref, w_ref, o_ref):
    qh = q_ref[...]
    q_lat = _dot_nt(qh[:, :MLA_NOPE], w_ref[...])
    out = jnp.concatenate([q_lat, qh[:, MLA_NOPE:].astype(F32)], axis=-1)
    o_ref[...] = out.reshape(o_ref.shape)


def mla_absorb(q_full, w_kv_b, row0, n_seq, t_len):
    rows = n_seq * t_len
    assert row0 % rows == 0
    width = MLA_KV_RANK + LANES
    return pl.pallas_call(
        _absorb_kernel,
        grid=(MLA_HEADS,),
        in_specs=[pl.BlockSpec((rows, HEAD_SLOT), lambda h: (row0 // rows, h)),
                  pl.BlockSpec((MLA_KV_RANK, MLA_NOPE), lambda h: (0, 2 * h))],
        out_specs=pl.BlockSpec((n_seq, None, t_len, width), lambda h: (0, h, 0, 0)),
        out_shape=jax.ShapeDtypeStruct((n_seq, MLA_HEADS, t_len, width), F32),
        compiler_params=_params(("parallel",)),
        name="mla_absorb",
    )(q_full, w_kv_b)


def _paged_attn_kernel(pt_ref, q_ref, cnew_ref, krnew_ref, clat_hbm, ckr_hbm, o_ref,
                       cf32, krf32, cbuf, m_ref, l_ref, acc_ref, csem, krsem, *, G, layer, t_len, n_steps, n_seq):
    b = pl.program_id(0)
    g = pl.program_id(1)
    step = b * n_steps + g
    slot = step % 2
    page = cbuf.shape[0] // G

    def page_copies(bb, gg, sl):
        copies = []
        for i in range(G):
            pidx = pt_ref[bb, gg * G + i]
            copies.append(pltpu.make_async_copy(
                clat_hbm.at[layer, pidx], cf32.at[sl, pl.ds(i * page, page), :], csem.at[sl]))
            copies.append(pltpu.make_async_copy(
                ckr_hbm.at[layer, pidx], krf32.at[sl, :, pl.ds(i * page, page)], krsem.at[sl]))
        return copies

    @pl.when(step == 0)
    def _():
        for cp in page_copies(b, g, slot):
            cp.start()

    @pl.when(step + 1 < n_seq * n_steps)
    def _():
        wrap = g + 1 == n_steps
        for cp in page_copies(jnp.where(wrap, b + 1, b), jnp.where(wrap, 0, g + 1), 1 - slot):
            cp.start()

    q = q_ref[0]
    q_lat = q[:, :MLA_KV_RANK]
    q_rope = q[:, MLA_KV_RANK:MLA_KV_RANK + MLA_ROPE]

    @pl.when(g == 0)
    def _():
        c_new = cnew_ref[...]
        kr_new = krnew_ref[...]
        tq = lax.broadcasted_iota(jnp.int32, (q.shape[0], 1), 0) % t_len
        cols = []
        for j in range(t_len):
            sj = (jnp.sum(q_lat * c_new[j:j + 1, :], axis=-1, keepdims=True)
                  + jnp.sum(q_rope * kr_new[j:j + 1, :], axis=-1, keepdims=True))
            cols.append(jnp.where(tq >= j, sj, -jnp.inf))
        m0 = cols[0]
        for j in range(1, t_len):
            m0 = jnp.maximum(m0, cols[j])
        l0 = jnp.zeros_like(m0)
        acc0 = jnp.zeros(acc_ref.shape, F32)
        for j in range(t_len):
            pj = jnp.exp2(cols[j] - m0)
            l0 = l0 + pj
            acc0 = acc0 + pj * c_new[j:j + 1, :]
        m_ref[...] = m0
        l_ref[...] = l0
        acc_ref[...] = acc0

    pltpu.make_async_copy(cf32.at[slot], cf32.at[slot], csem.at[slot]).wait()
    pltpu.make_async_copy(krf32.at[slot], krf32.at[slot], krsem.at[slot]).wait()

    q_lat_b = q_lat.astype(BF16)
    q_rope_b = q_rope.astype(BF16)
    s_parts = []
    for i in range(0, G, 2):
        cb = cf32[slot, pl.ds(i * page, 2 * page), :].astype(BF16)
        cbuf[i * page:(i + 2) * page, :] = cb
        krt = krf32[slot, :, pl.ds(i * page, 2 * page)].astype(BF16)
        s_parts.append(lax.dot_general(q_lat_b, cb, NT_DIMS, preferred_element_type=F32)
                       + jnp.dot(q_rope_b, krt, preferred_element_type=F32))
    s = jnp.concatenate(s_parts, axis=1)
    m_prev = m_ref[...]
    m_new = jnp.maximum(m_prev, jnp.max(s, axis=-1, keepdims=True))
    alpha = jnp.exp2(m_prev - m_new)
    p = jnp.exp2(s - m_new)
    l_ref[...] = alpha * l_ref[...] + jnp.sum(p, axis=-1, keepdims=True)
    acc_ref[...] = alpha * acc_ref[...] + jnp.dot(p.astype(BF16), cbuf[...], preferred_element_type=F32)
    m_ref[...] = m_new

    @pl.when(g == n_steps - 1)
    def _():
        o_ref[0] = acc_ref[...] / l_ref[...]


def mla_paged_attention(q_abs, lat_s, kr_s, cache_latent, cache_krope_t, layer, page_table, *, G=32):
    n_seq, heads, t_len, width = q_abs.shape
    n_pages = page_table.shape[1]
    page = cache_latent.shape[2]
    G = _tile(n_pages, G, 2)
    n_steps = n_pages // G
    rows = heads * t_len
    q3 = q_abs.reshape(n_seq, rows, width)
    grid_spec = pltpu.PrefetchScalarGridSpec(
        num_scalar_prefetch=1,
        grid=(n_seq, n_steps),
        in_specs=[pl.BlockSpec((1, rows, width), lambda b, g, pt: (b, 0, 0)),
                  pl.BlockSpec((None, t_len, MLA_KV_RANK), lambda b, g, pt: (layer, b, 0)),
                  pl.BlockSpec((None, t_len, MLA_ROPE), lambda b, g, pt: (layer, b, 0)),
                  _ANY_SPEC, _ANY_SPEC],
        out_specs=pl.BlockSpec((1, rows, MLA_KV_RANK), lambda b, g, pt: (b, 0, 0)),
        scratch_shapes=[pltpu.VMEM((2, G * page, MLA_KV_RANK), F32),
                        pltpu.VMEM((2, MLA_ROPE, G * page), F32),
                        pltpu.VMEM((G * page, MLA_KV_RANK), BF16),
                        pltpu.VMEM((rows, 1), F32), pltpu.VMEM((rows, 1), F32),
                        pltpu.VMEM((rows, MLA_KV_RANK), F32),
                        pltpu.SemaphoreType.DMA((2,)), pltpu.SemaphoreType.DMA((2,))],
    )
    return pl.pallas_call(
        functools.partial(_paged_attn_kernel, G=G, layer=layer, t_len=t_len, n_steps=n_steps, n_seq=n_seq),
        grid_spec=grid_spec,
        out_shape=jax.ShapeDtypeStruct((n_seq, rows, MLA_KV_RANK), F32),
        compiler_params=_params(("arbitrary", "arbitrary")),
        name="mla_paged_attention",
    )(page_table, q3, lat_s, kr_s, cache_latent, cache_krope_t)


def _uv_kernel(o_ref, w_ref, prev_ref, out_ref):
    o = o_ref[...]
    out_ref[...] = _dot(o.reshape(o.shape[0] * o.shape[1], o.shape[2]), w_ref[...]).astype(out_ref.dtype)


def mla_uv(o_lat, w_kv_b, row0, n_seq, t_len, ob_prev):
    o4 = o_lat.reshape(n_seq, MLA_HEADS, t_len, MLA_KV_RANK)
    rows = n_seq * t_len
    assert row0 % rows == 0
    return pl.pallas_call(
        _uv_kernel,
        grid=(MLA_HEADS,),
        in_specs=[pl.BlockSpec((n_seq, None, t_len, MLA_KV_RANK), lambda h: (0, h, 0, 0)),
                  pl.BlockSpec((MLA_KV_RANK, MLA_V), lambda h: (0, 2 * h + 1)),
                  _ANY_SPEC],
        out_specs=pl.BlockSpec((rows, MLA_V), lambda h: (row0 // rows, h)),
        out_shape=jax.ShapeDtypeStruct(ob_prev.shape, BF16),
        input_output_aliases={2: 0},
        compiler_params=_params(("parallel",)),
        name="mla_uv",
    )(o4, w_kv_b, ob_prev)


def _gmlp_kernel(*refs, emit_vn):
    gu_ref, gv_ref, g_ref, w_ref, bias_ref = refs[:5]
    o_ref = refs[-2] if emit_vn else refs[-1]
    vn = _rms(_gelu(gv_ref[...]), g_ref[...])
    if emit_vn:
        refs[-1][...] = vn
    u = _gelu(gu_ref[...])
    for grp in range(CM_GROUPS):
        sl = slice(grp * LANES, (grp + 1) * LANES)
        z = _dot(w_ref[grp], vn[:, sl]) + bias_ref[:, sl]
        o_ref[:, sl] = (u[:, sl] * z).astype(o_ref.dtype)


def gmlp(proj, row0, n_rows, v_norm, w_mix, bias_full, oc_prev, *, emit_vn):
    width = CM_GROUPS * LANES
    assert row0 % CM_CHUNK == 0 and n_rows % CM_CHUNK == 0
    rb0 = row0 // CM_CHUNK
    in_specs = [pl.BlockSpec((CM_CHUNK, width), lambda i: (rb0 + i, COL_GU // width)),
                pl.BlockSpec((CM_CHUNK, width), lambda i: (rb0 + i, COL_GV // width)),
                pl.BlockSpec((1, width), lambda i: (0, 0)),
                pl.BlockSpec((CM_GROUPS, CM_CHUNK, CM_CHUNK), lambda i: (0, 0, 0)),
                pl.BlockSpec((CM_CHUNK, width), lambda i: (0, 0))]
    args = [proj, proj, v_norm.reshape(1, -1), w_mix, bias_full]
    aliases = {}
    if oc_prev is not None:
        aliases[len(args)] = 0
        in_specs.append(_ANY_SPEC)
        args.append(oc_prev)
    out_specs = [pl.BlockSpec((CM_CHUNK, width), lambda i: (rb0 + i, 0))]
    out_shape = [jax.ShapeDtypeStruct((proj.shape[0], width), BF16)]
    if emit_vn:
        out_specs.append(pl.BlockSpec((CM_CHUNK, width), lambda i: (i, 0)))
        out_shape.append(jax.ShapeDtypeStruct((n_rows, width), F32))
    return pl.pallas_call(
        functools.partial(_gmlp_kernel, emit_vn=emit_vn),
        grid=(n_rows // CM_CHUNK,),
        in_specs=in_specs,
        out_specs=out_specs,
        out_shape=out_shape,
        input_output_aliases=aliases,
        compiler_params=_params(("parallel",)),
        name="gmlp",
    )(*args)


def _merge_kernel(oa_ref, ob_ref, oc_ref, g0_ref, g1_ref, g2_ref, w_ref, o_ref):
    acc = None
    for br, (b_ref, g_ref) in enumerate(((oa_ref, g0_ref), (ob_ref, g1_ref), (oc_ref, g2_ref))):
        part = g_ref[...] * jnp.dot(b_ref[...], w_ref[br], preferred_element_type=F32)
        acc = part if acc is None else acc + part
    o_ref[...] = acc.astype(o_ref.dtype)


def branch_merge(o_a, o_b, o_c, gates, w_branch, *, tm=1024, tn=512):
    m, bw = o_a.shape
    d = w_branch.shape[2]
    tm, tn = _tile(m, tm, SUBLANES), _tile(d, tn, LANES)
    nt = d // tn
    bspec = pl.BlockSpec((tm, bw), lambda i, j: (i, 0))

    def gspec(br):
        return pl.BlockSpec((tm, tn), lambda i, j: (i, br * nt + j))

    return pl.pallas_call(
        _merge_kernel,
        grid=(m // tm, nt),
        in_specs=[bspec, bspec, bspec, gspec(0), gspec(1), gspec(2),
                  pl.BlockSpec((N_BRANCH, bw, tn), lambda i, j: (0, 0, j))],
        out_specs=pl.BlockSpec((tm, tn), lambda i, j: (i, j)),
        out_shape=jax.ShapeDtypeStruct((m, d), BF16),
        compiler_params=_params(("parallel", "arbitrary")),
        name="branch_merge",
    )(o_a, o_b, o_c, gates, gates, gates, w_branch)


def _out_proj_kernel(mix_ref, w_ref, g_ref, gn_ref, *refs, n_first):
    o_ref, hn_ref = refs[-2:]
    if n_first is None:
        x = refs[0][...]
    else:
        x = jnp.where(pl.program_id(0) < n_first, refs[0][...], refs[1][...])
    y = jnp.dot(mix_ref[...], w_ref[...], preferred_element_type=F32)
    x_new = x + _rms(y, g_ref[...])
    o_ref[...] = x_new
    hn_ref[...] = _rms(x_new, gn_ref[...]).astype(BF16)


def out_proj(mix, x, w_out, g, g_next, *, tm=512):
    m, d = mix.shape
    row = lambda t: pl.BlockSpec((t, d), lambda i: (i, 0))
    vec = pl.BlockSpec((1, d), lambda i: (0, 0))
    if isinstance(x, tuple):
        tm = _tile(math.gcd(x[0].shape[0], x[1].shape[0]), tm, SUBLANES)
        n_first, x_specs = _two_source_specs(x, tm)
        x_args = list(x)
    else:
        tm = _tile(m, tm, SUBLANES)
        n_first, x_specs, x_args = None, [row(tm)], [x]
    return pl.pallas_call(
        functools.partial(_out_proj_kernel, n_first=n_first),
        grid=(m // tm,),
        in_specs=[row(tm), pl.BlockSpec((d, d), lambda i: (0, 0)), vec, vec] + x_specs,
        out_specs=[row(tm), row(tm)],
        out_shape=[jax.ShapeDtypeStruct((m, d), F32), jax.ShapeDtypeStruct((m, d), BF16)],
        compiler_params=_params(("parallel",)),
        name="out_proj",
    )(mix, w_out, g.reshape(1, -1), g_next.reshape(1, -1), *x_args)


def _ffn_up_kernel(*refs, t_len, per_seq_prev, tiles_per_seq, kc, rb):
    h_ref, wg_ref, wv_ref, cw_ref, cb_ref = refs[:5]
    if per_seq_prev:
        pg_ref, pv_ref = refs[5:7]
        act_ref, tailg_ref, tailv_ref, wb_ref = refs[-4:]
    else:
        act_ref, tailg_ref, tailv_ref, wb_ref, carry_ref, shift_ref = refs[-6:]
    i = pl.program_id(1)
    tm, tn = act_ref.shape

    @pl.when(i == 0)
    def _():
        wb_ref[0] = wg_ref[...].astype(BF16)
        wb_ref[1] = wv_ref[...].astype(BF16)
        if not per_seq_prev:
            carry_ref[...] = jnp.zeros(carry_ref.shape, F32)

    if not per_seq_prev:
        seq_start = (i % tiles_per_seq) == 0

    def conv(u, which, sl, p):
        if per_seq_prev:
            u = u.reshape(rb // t_len, t_len, u.shape[-1])
            tpos = lax.broadcasted_iota(jnp.int32, (1, t_len, 1), 1)
            back1 = p[:, 1:2, :]
            back2 = jnp.where(tpos == 0, p[:, 0:1, :], p[:, 1:2, :])
            axis = 1
            m1 = jnp.where(tpos == 0, back1, pltpu.roll(u, 1, axis))
            m2 = jnp.where(tpos <= 1, back2, pltpu.roll(u, 2, axis))
        else:
            shift_ref[which, 0:SUBLANES, :] = p
            shift_ref[which, SUBLANES:SUBLANES + rb, :] = u
            m1 = shift_ref[which, SUBLANES - 1:SUBLANES - 1 + rb, :]
            m2 = shift_ref[which, SUBLANES - 2:SUBLANES - 2 + rb, :]
        y = (cb_ref[which:which + 1, sl] + cw_ref[which, 0:1, sl] * m2 + cw_ref[which, 1:2, sl] * m1
             + cw_ref[which, 2:3, sl] * u)
        return y.reshape(rb, y.shape[-1])

    def compute():
        tails = []
        for c0 in range(0, tn, kc):
            sl = slice(c0, c0 + kc)
            wg = wb_ref[0, :, sl]
            wv = wb_ref[1, :, sl]
            if not per_seq_prev:
                pg = jnp.where(seq_start, 0.0, carry_ref[0][:, sl])
                pv = jnp.where(seq_start, 0.0, carry_ref[1][:, sl])
            for r0 in range(0, tm, rb):
                hr = h_ref[r0:r0 + rb, :]
                ug = jnp.dot(hr, wg, preferred_element_type=F32)
                uv = jnp.dot(hr, wv, preferred_element_type=F32)
                if per_seq_prev:
                    s0, s1 = r0 // t_len, (r0 + rb) // t_len
                    pg = pg_ref[s0:s1, :, sl]
                    pv = pv_ref[s0:s1, :, sl]
                act = _gelu(conv(ug, 0, sl, pg)) * conv(uv, 1, sl, pv)
                act_ref[r0:r0 + rb, sl] = act.astype(act_ref.dtype)
                if per_seq_prev:
                    tailg_ref[s0:s1, :, sl] = ug.reshape(rb // t_len, t_len, kc)[:, t_len - (CONV_W - 1):, :]
                    tailv_ref[s0:s1, :, sl] = uv.reshape(rb // t_len, t_len, kc)[:, t_len - (CONV_W - 1):, :]
                else:
                    pg, pv = ug[rb - SUBLANES:, :], uv[rb - SUBLANES:, :]
            if not per_seq_prev:
                tailg_ref[:, sl] = pg
                tailv_ref[:, sl] = pv
                tails.append((sl, pg, pv))
        for sl, tg, tv in tails:
            carry_ref[0, :, sl] = tg
            carry_ref[1, :, sl] = tv

    compute()


def ffn_up(hn, w_up_all, layer, conv_w2, conv_b2, prev_state, row0, n_rows, t_len, act_prev, *, tm=1024,
           tn=512, kc=256, rb=512):
    d = hn.shape[1]
    d_ff = w_up_all.shape[2] // 2
    per_seq_prev = t_len == SUBLANES
    tm = _tile(n_rows if per_seq_prev else t_len, tm, SUBLANES)
    tn = _tile(d_ff, tn, LANES)
    kc = _tile(tn, kc, LANES)
    rb = _tile(tm, rb, t_len if per_seq_prev else SUBLANES)
    n_j = d_ff // tn
    n_i = n_rows // tm
    assert row0 % tm == 0 and (tm % t_len == 0 or t_len % tm == 0)
    rb0 = row0 // tm
    in_specs = [pl.BlockSpec((tm, d), lambda j, i: (rb0 + i, 0)),
                pl.BlockSpec((None, d, tn), lambda j, i: (layer, 0, j)),
                pl.BlockSpec((None, d, tn), lambda j, i: (layer, 0, n_j + j)),
                pl.BlockSpec((2, CONV_W, tn), lambda j, i: (0, 0, j)),
                pl.BlockSpec((2, tn), lambda j, i: (0, j))]
    args = [hn, w_up_all, w_up_all, conv_w2, conv_b2]
    scratch = [pltpu.VMEM((2, d, tn), BF16)]
    if per_seq_prev:
        ns = tm // t_len
        in_specs += [pl.BlockSpec((None, ns, CONV_W - 1, tn), lambda j, i: (layer, i, 0, j)),
                     pl.BlockSpec((None, ns, CONV_W - 1, tn), lambda j, i: (layer, i, 0, n_j + j))]
        args += [prev_state, prev_state]
        tail_spec = pl.BlockSpec((ns, CONV_W - 1, tn), lambda j, i: (i, 0, j))
        tail_shape = jax.ShapeDtypeStruct((n_rows // t_len, CONV_W - 1, d_ff), F32)
    else:
        scratch.append(pltpu.VMEM((2, SUBLANES, tn), F32))
        scratch.append(pltpu.VMEM((2, SUBLANES + rb, kc), F32))
        tail_spec = pl.BlockSpec((None, SUBLANES, tn), lambda j, i: (i, 0, j))
        tail_shape = jax.ShapeDtypeStruct((n_i, SUBLANES, d_ff), F32)
    aliases = {len(args): 0}
    in_specs.append(_ANY_SPEC)
    args.append(act_prev)
    return pl.pallas_call(
        functools.partial(_ffn_up_kernel, t_len=t_len, per_seq_prev=per_seq_prev,
                          tiles_per_seq=max(t_len // tm, 1), kc=kc, rb=rb),
        grid=(n_j, n_i),
        in_specs=in_specs,
        out_specs=[pl.BlockSpec((tm, tn), lambda j, i: (rb0 + i, j)), tail_spec, tail_spec],
        out_shape=[jax.ShapeDtypeStruct(act_prev.shape, BF16), tail_shape, tail_shape],
        scratch_shapes=scratch,
        input_output_aliases=aliases,
        compiler_params=_params(("arbitrary", "arbitrary")),
        name="ffn_up",
    )(*args)


def _ffn_down_kernel(a_ref, w_ref, x_ref, g_ref, o_ref, acc_ref, *, n_k):
    kk = pl.program_id(1)
    part = jnp.dot(a_ref[...], w_ref[...], preferred_element_type=F32)
    if n_k == 1:
        o_ref[...] = x_ref[...] + _rms(part, g_ref[...])
        return

    @pl.when(kk == 0)
    def _():
        acc_ref[...] = part

    @pl.when((kk > 0) & (kk < n_k - 1))
    def _():
        acc_ref[...] += part

    @pl.when(kk == n_k - 1)
    def _():
        o_ref[...] = x_ref[...] + _rms(acc_ref[...] + part, g_ref[...])


def ffn_down(act, w_down, x, g, *, tm=512, tk=2816):
    d_ff, d = w_down.shape
    m = x.shape[0]
    tm = _tile(m, tm, SUBLANES)
    tk = _tile(d_ff, tk, LANES)
    n_k = d_ff // tk
    return pl.pallas_call(
        functools.partial(_ffn_down_kernel, n_k=n_k),
        grid=(m // tm, n_k),
        in_specs=[pl.BlockSpec((tm, tk), lambda i, k: (i, k)),
                  pl.BlockSpec((tk, d), lambda i, k: (k, 0)),
                  pl.BlockSpec((tm, d), lambda i, k: (i, 0)),
                  pl.BlockSpec((1, d), lambda i, k: (0, 0))],
        out_specs=pl.BlockSpec((tm, d), lambda i, k: (i, 0)),
        out_shape=jax.ShapeDtypeStruct(x.shape, F32),
        scratch_shapes=[pltpu.VMEM((tm, d), F32)],
        compiler_params=_params(("parallel", "arbitrary")),
        name="ffn_down",
    )(act, w_down, x, g.reshape(1, -1))


def _ple_kernel(x_ref, pe_ref, g_ref, wg_ref, wp_ref, gn_ref, *o_refs, n_first):
    x = x_ref[...]
    gate = _sigmoid(_dot(_rms(x, g_ref[...]), wg_ref[...]))
    y = x + _dot(pe_ref[...], wp_ref[...]) * gate
    if n_first is None:
        o_refs[0][...] = y
        o_refs[1][...] = _rms(y, gn_ref[...]).astype(BF16)
        return
    i = pl.program_id(0)

    @pl.when(i < n_first)
    def _():
        o_refs[0][...] = y

    @pl.when(i >= n_first)
    def _():
        o_refs[1][...] = y


def ple(x, pe, g, w_gate, w_proj, g_next, *, split_rows=None, tm=512):
    m, d = x.shape
    pd = pe.shape[1]
    tm = _tile(m if split_rows is None else math.gcd(split_rows, m - split_rows), tm, SUBLANES)
    if split_rows is None:
        n_first = None
        out_specs = [pl.BlockSpec((tm, d), lambda i: (i, 0)), pl.BlockSpec((tm, d), lambda i: (i, 0))]
        out_shape = [jax.ShapeDtypeStruct((m, d), F32), jax.ShapeDtypeStruct((m, d), BF16)]
    else:
        n_first = split_rows // tm
        out_specs = [pl.BlockSpec((tm, d), lambda i: (jnp.minimum(i, n_first - 1), 0)),
                     pl.BlockSpec((tm, d), lambda i: (jnp.maximum(i - n_first, 0), 0))]
        out_shape = [jax.ShapeDtypeStruct((split_rows, d), F32), jax.ShapeDtypeStruct((m - split_rows, d), F32)]
    return pl.pallas_call(
        functools.partial(_ple_kernel, n_first=n_first),
        grid=(m // tm,),
        in_specs=[pl.BlockSpec((tm, d), lambda i: (i, 0)),
                  pl.BlockSpec((tm, pd), lambda i: (i, 0)),
                  pl.BlockSpec((1, d), lambda i: (0, 0)),
                  pl.BlockSpec((d, d), lambda i: (0, 0)),
                  pl.BlockSpec((pd, d), lambda i: (0, 0)),
                  pl.BlockSpec((1, d), lambda i: (0, 0))],
        out_specs=out_specs,
        out_shape=out_shape,
        compiler_params=_params(("arbitrary",)),
        name="ple",
    )(x, pe, g.reshape(1, -1), w_gate, w_proj, g_next.reshape(1, -1))


def _rope_table(pos):
    inv = 1.0 / (ROPE_THETA ** (jnp.arange(0, MLA_ROPE, 2, dtype=F32) / MLA_ROPE))
    ang = pos.astype(F32)[:, None] * inv[None, :]
    cos, sin = jnp.cos(ang), jnp.sin(ang)
    return jnp.concatenate([cos, cos, -sin, sin], axis=-1)


def _swap_halves(w, width):
    half = width // 2
    return jnp.concatenate([w[..., half:], w[..., :half]], axis=-1)


def kernel(x_prompt, x_sample, cache_latent, cache_krope, state_hgrn, state_ffn_conv, page_table, p_prompt, p_sample, ln_mix_pre, ln_mix_post, ln_ffn_pre, ln_ffn_post, ln_ple, w_in, hgrn_lower_bounds, hgrn_norm, mla_q_norm, mla_kv_norm, w_q_b, w_kv_b, cm_v_norm, cm_w_s, cm_b, w_merge_gate, w_branch, w_out, w_ffn_up, ffn_conv_w, ffn_conv_b, w_ffn_down, w_ple_proj, w_ple_gate):
    n_b, seq, d = x_prompt.shape
    n_db, t_len, _ = x_sample.shape
    depth = w_in.shape[0]
    d_ff = w_ffn_down.shape[1]
    page = cache_latent.shape[2]
    past_len = page_table.shape[1] * page
    rows_p, rows_s = n_b * seq, n_db * t_len
    rows = rows_p + rows_s
    hk = HG_HEADS * HG_KEY

    lb_soft = jax.nn.softmax(hgrn_lower_bounds.astype(F32), axis=0)
    lbs = jnp.concatenate([jnp.zeros_like(lb_soft[:1]), jnp.cumsum(lb_soft[1:], axis=0)], axis=0)

    tab = jnp.concatenate([jnp.tile(_rope_table(jnp.arange(seq)), (n_b, 1)),
                           jnp.tile(_rope_table(past_len + jnp.arange(t_len)), (n_db, 1))], axis=0)
    cache_krope_t = jnp.swapaxes(cache_krope, 2, 3)
    w_in_t = jnp.swapaxes(w_in, 1, 2)

    x = (x_prompt.reshape(rows_p, d), x_sample.reshape(rows_s, d))
    tri = jnp.tril(jnp.ones((CM_CHUNK, CM_CHUNK), F32))
    seq_per_chunk = CM_CHUNK // t_len

    lat_p = jnp.zeros((depth, rows_p, MLA_KV_RANK), F32)
    kr_p = jnp.zeros((depth, rows_p, MLA_ROPE), F32)
    lat_s = jnp.zeros((depth, rows_s, MLA_KV_RANK), F32)
    kr_s = jnp.zeros((depth, rows_s, MLA_ROPE), F32)
    hg_p = jnp.zeros((depth, n_b, HG_HEADS, HG_KEY, HG_VAL), F32)
    hg_s = jnp.zeros((depth, n_db, HG_HEADS, HG_KEY, HG_VAL), F32)
    cv_p, cv_s, v_s = [], [], []

    for i in range(depth):
        wq = w_q_b[i].reshape(MLA_Q_RANK, MLA_HEADS, MLA_NOPE + MLA_ROPE)
        wq_rope = wq[..., MLA_NOPE:]
        w_q_ext = jnp.concatenate(
            [wq[..., :MLA_NOPE].reshape(MLA_Q_RANK, -1),
             jnp.concatenate([wq_rope, _swap_halves(wq_rope, MLA_ROPE)], axis=-1).reshape(MLA_Q_RANK, -1)],
            axis=1).astype(BF16)
        w_kvb = w_kv_b[i].astype(BF16)
        lb = lbs[i]
        lb3 = jnp.stack([lb, jnp.log(lb), jnp.log1p(-lb)], axis=0)
        w_mix_p = cm_w_s[i] * tri
        w_mix_s = jnp.einsum("ab,gts->gatbs", jnp.eye(seq_per_chunk, dtype=F32),
                             w_mix_p[:, :t_len, :t_len]).reshape(CM_GROUPS, CM_CHUNK, CM_CHUNK)
        bias_p = jnp.repeat(cm_b[i].T, LANES, axis=1)
        bias_s = jnp.tile(bias_p[:t_len], (seq_per_chunk, 1))
        conv_w2 = ffn_conv_w[i].reshape(CONV_W, 2, d_ff).transpose(1, 0, 2)
        conv_b2 = ffn_conv_b[i].reshape(2, d_ff)

        zeros_branch = jnp.zeros((rows, HG_HEADS * HG_VAL), BF16)
        if i == 0:
            hn = rms_cast(x, ln_mix_pre[i])
        proj = matmul_ws_t(hn, w_in_t, i, IN_TILE_SRC, tn=IN_TILE)
        gates = matmul_ws(hn, w_merge_gate, i, act="sigmoid")

        o_a, hg_p = hgrn2(proj, 0, n_b, seq, lb3, hgrn_norm[i], None, i, zeros_branch, hg_p,
                          C=_tile(seq, 128, SUBLANES), nb=1, hh=4)
        o_a, hg_s = hgrn2(proj, rows_p, n_db, t_len, lb3, hgrn_norm[i], state_hgrn, i, o_a, hg_s,
                          C=t_len, nb=_tile(n_db, LANES // t_len), hh=2)

        q_full = mla_q(proj, mla_q_norm[i], w_q_ext, tab)
        lat_p, kr_p, k_full, v_all = mla_kv(proj, 0, rows_p, mla_kv_norm[i], w_kvb, tab, i, lat_p, kr_p,
                                            with_kv=True)
        lat_s, kr_s = mla_kv(proj, rows_p, rows_s, mla_kv_norm[i], w_kvb, tab, i, lat_s, kr_s, with_kv=False)
        o_b = mla_prompt_attention(q_full, k_full, v_all, n_b, seq, zeros_branch)
        q_abs = mla_absorb(q_full, w_kvb, rows_p, n_db, t_len)
        o_lat = mla_paged_attention(q_abs, lat_s, kr_s, cache_latent, cache_krope_t, i, page_table)
        o_b = mla_uv(o_lat, w_kvb, rows_p, n_db, t_len, o_b)

        (o_c,) = gmlp(proj, 0, rows_p, cm_v_norm[i], w_mix_p, bias_p, zeros_branch, emit_vn=False)
        o_c, vn_s = gmlp(proj, rows_p, rows_s, cm_v_norm[i], w_mix_s, bias_s, o_c, emit_vn=True)

        mix = branch_merge(o_a, o_b, o_c, gates, w_branch[i].astype(BF16))
        x, hn_ffn = out_proj(mix, x, w_out[i].astype(BF16), ln_mix_post[i], ln_ffn_pre[i])

        act, tg_p, tv_p = ffn_up(hn_ffn, w_ffn_up, i, conv_w2, conv_b2, None, 0, rows_p, seq,
                                 jnp.zeros((rows, d_ff), BF16))
        act, tg_s, tv_s = ffn_up(hn_ffn, w_ffn_up, i, conv_w2, conv_b2, state_ffn_conv, rows_p, rows_s, t_len, act)
        x = ffn_down(act, w_ffn_down[i].astype(BF16), x, ln_ffn_post[i])

        pe = jnp.concatenate([p_prompt[i].reshape(rows_p, -1), p_sample[i].reshape(rows_s, -1)], axis=0)
        last = i == depth - 1
        x = ple(x, pe, ln_ple[i], w_ple_gate[i].astype(BF16), w_ple_proj[i].astype(BF16),
                ln_ple[i] if last else ln_mix_pre[i + 1], split_rows=rows_p if last else None)
        if not last:
            x, hn = x

        tails_p = jnp.concatenate([tg_p, tv_p], axis=-1).reshape(n_b, -1, SUBLANES, 2 * d_ff)
        cv_p.append(tails_p[:, -1, SUBLANES - (CONV_W - 1):, :])
        cv_s.append(jnp.concatenate([tg_s, tv_s], axis=-1))
        v_s.append(vn_s.reshape(n_db, t_len, -1))

    y_p, y_s = x
    return (y_p.reshape(n_b, seq, d), y_s.reshape(n_db, t_len, d),
            lat_p.reshape(depth, n_b, seq, -1), kr_p.reshape(depth, n_b, seq, -1),
            lat_s.reshape(depth, n_db, t_len, -1), kr_s.reshape(depth, n_db, t_len, -1),
            hg_p, hg_s, jnp.stack(cv_p), jnp.stack(cv_s), jnp.stack(v_s))
```
